```python
import math
import jax, jax.numpy as jnp
from jax import lax
import numpy as np

D_MODEL = 1024
BATCH = 8
SEQ = 2048
DEPTH = 4

CTX_LEN = 256
GRID_W = 64
NORM_EPS = 1e-6
N_MOD = 6

D_HYENA = D_MODEL // 2
D_HGRN = D_MODEL // 2
D_MIX = D_HYENA + D_HGRN
D_IN = 3 * D_HYENA + 5 * D_HGRN

HY_ORDER = 2
SHORT_CONV = 3
HY_BANDS = 16
HY_FEAT = 2 * HY_BANDS + 1
HY_FILTER_HIDDEN = 64
HY_TARGET = 1e-2
HY_FAST_PCT = 0.3
HY_SLOW_PCT = 1.5

HGRN_HEAD_DIM = 128
HGRN_HEADS = D_HGRN // HGRN_HEAD_DIM
HGRN_CHUNK = 64

N_EXPERTS = 16
N_GROUPS = 4
EXPERTS_PER_GROUP = N_EXPERTS // N_GROUPS
TOP_K = 2
D_EXPERT = D_MODEL // 4

kernel_name = "hyena_hgrn2_grouped_moe_prefix_trunk"


def _rms(x, w=None):
    xf = x.astype(jnp.float32)
    y = xf * lax.rsqrt(jnp.mean(xf * xf, axis=-1, keepdims=True) + NORM_EPS)
    if w is not None:
        y = y * w.astype(jnp.float32)
    return y.astype(x.dtype)


def _short_conv(u, w, b):
    up = jnp.pad(u, ((0, 0), (1, 1), (0, 0)))
    return up[:, :-2] * w[0] + up[:, 1:-1] * w[1] + up[:, 2:] * w[2] + b


def _short_conv_grid(u, w, b):
    bsz, n, ch = u.shape
    rows = n // GRID_W
    return _short_conv(u.reshape(bsz * rows, GRID_W, ch), w, b).reshape(bsz, n, ch)


def _hyena_filters(L, w1, b1, w2, b2, w3, b3):
    t = jnp.linspace(0.0, 1.0, L, dtype=jnp.float32)
    n = jnp.arange(L, dtype=jnp.float32)
    freqs = jnp.linspace(1e-4, HY_BANDS - 1, HY_BANDS, dtype=jnp.float32)
    ang = (2.0 * math.pi / L) * n[:, None] * freqs[None, :]
    z = jnp.concatenate([t[:, None], jnp.cos(ang), -jnp.sin(ang)], axis=-1)
    h = jnp.sin(z @ w1 + b1)
    h = jnp.sin(h @ w2 + b2)
    h = (h @ w3 + b3).astype(jnp.float32).reshape(L, HY_ORDER, 2, D_HYENA)
    deltas = jnp.linspace(math.log(HY_TARGET) / HY_FAST_PCT, math.log(HY_TARGET) / HY_SLOW_PCT,
                          D_HYENA, dtype=jnp.float32)
    window = jnp.exp(-t[:, None] * jnp.abs(deltas)[None, :])
    h = h * window[:, None, None, :]
    return h / jnp.sum(jnp.abs(h), axis=(0, 2), keepdims=True)


def _bidir_fftconv(u, h, d):
    L = u.shape[1]
    k = jnp.concatenate([h[:, 0], jnp.zeros_like(h[:1, 0]), h[:0:-1, 1]], axis=0)
    uf = jnp.fft.rfft(u.astype(jnp.float32), n=2 * L, axis=1)
    kf = jnp.fft.rfft(k, n=2 * L, axis=0)
    y = jnp.fft.irfft(uf * kf[None], n=2 * L, axis=1)[:, :L]
    return (y + u.astype(jnp.float32) * d.astype(jnp.float32)).astype(u.dtype)


def _hyena(p, conv_w, conv_b, w1, b1, w2, b2, w3, b3, d, grid):
    L = p.shape[1]
    u = _short_conv_grid(p, conv_w, conv_b) if grid else _short_conv(p, conv_w, conv_b)
    v, x1, x2 = jnp.split(u, 3, axis=-1)
    h = _hyena_filters(L, w1, b1, w2, b2, w3, b3)
    z = x1 * _bidir_fftconv(v, h[:, 0], d[0])
    return x2 * _bidir_fftconv(z, h[:, 1], d[1])


def _heads(a):
    bsz, L, _ = a.shape
    return a.reshape(bsz, L, HGRN_HEADS, HGRN_HEAD_DIM).transpose(0, 2, 1, 3).astype(jnp.float32)


def _forget(logit, lb):
    lb = lb[None, :, None, :]
    f = lb + (1.0 - lb) * jax.nn.sigmoid(logit)
    return 1.0 - f, jnp.log(f)


def _hgrn_scan(q, k, logf, v, s0):
    bsz, nh, L, dk = q.shape
    dv = v.shape[-1]
    nc = L // HGRN_CHUNK

    def chunks(a):
        return a.reshape(bsz, nh, nc, HGRN_CHUNK, a.shape[-1]).transpose(2, 0, 1, 3, 4)

    lower = jnp.tril(jnp.ones((HGRN_CHUNK, HGRN_CHUNK), dtype=bool))[:, :, None]

    def step(s, inp):
        qc, kc, gc, vc = inp
        b = jnp.cumsum(gc, axis=2)
        o_inter = jnp.einsum("bhtk,bhkv->bhtv", qc * jnp.exp(b), s)
        rel = b[:, :, :, None, :] - b[:, :, None, :, :]
        decay = jnp.exp(jnp.where(lower, rel, -jnp.inf))
        scores = jnp.einsum("bhtk,bhtsk,bhsk->bhts", qc, decay, kc)
        o = o_inter + jnp.einsum("bhts,bhsv->bhtv", scores, vc)
        b_end = b[:, :, -1:, :]
        s_new = (jnp.exp(b_end[:, :, 0, :, None]) * s
                 + jnp.einsum("bhsk,bhsv->bhkv", kc * jnp.exp(b_end - b), vc))
        return s_new, o

    s_fin, o = lax.scan(step, s0, (chunks(q), chunks(k), chunks(logf), chunks(v)))
    return o.transpose(1, 2, 0, 3, 4).reshape(bsz, nh, L, dv), s_fin


def _hgrn_mixer(p, lb_f, lb_b, s_f0, s_b0):
    q, i, g, ff, fb = jnp.split(p, 5, axis=-1)
    q = jax.nn.silu(_heads(q))
    v = _heads(i)
    k_f, logf_f = _forget(_heads(ff), lb_f)
    k_b, logf_b = _forget(_heads(fb), lb_b)
    o_f, s_f = _hgrn_scan(q, k_f, logf_f, v, s_f0)
    o_b, s_b = _hgrn_scan(q[:, :, ::-1], k_b[:, :, ::-1], logf_b[:, :, ::-1], v[:, :, ::-1], s_b0)
    return o_f + o_b[:, :, ::-1], g, s_f, s_b


def _hgrn_out(o, g, norm_w):
    bsz, nh, L, dv = o.shape
    o = _rms(o.transpose(0, 2, 1, 3), norm_w).reshape(bsz, L, nh * dv)
    return (o * jax.nn.silu(g.astype(jnp.float32))).astype(g.dtype)


def _moe(h, router_w, router_bias, w_gate, w_up, w_down):
    bsz, L, d = h.shape
    t = h.reshape(bsz * L, d)
    scores = jax.nn.softmax((t @ router_w).astype(jnp.float32), axis=-1)
    sel = (scores + router_bias.astype(jnp.float32)).reshape(-1, N_GROUPS, EXPERTS_PER_GROUP)
    group_score = jnp.sum(lax.top_k(sel, TOP_K)[0], axis=-1)
    best = jnp.argmax(group_score, axis=-1)
    in_group = (jnp.arange(N_GROUPS)[None, :] == best[:, None])[:, :, None]
    sel = jnp.where(in_group, sel, -jnp.inf).reshape(-1, N_EXPERTS)
    _, idx = lax.top_k(sel, TOP_K)
    w = jnp.take_along_axis(scores, idx, axis=-1)
    w = w / jnp.sum(w, axis=-1, keepdims=True)
    gates = jnp.sum(jax.nn.one_hot(idx, N_EXPERTS, dtype=jnp.float32) * w[..., None], axis=1)
    act = jax.nn.silu(jnp.einsum("nd,edf->nef", t, w_gate)) * jnp.einsum("nd,edf->nef", t, w_up)
    act = act * gates.astype(h.dtype)[..., None]
    return jnp.einsum("nef,efd->nd", act, w_down).reshape(bsz, L, d)


def setup_inputs(seed: int = 0) -> dict:
    key = jax.random.key(seed)
    ks = jax.random.split(key, 26)
    D = D_MODEL

    def nrm(k, shape, s):
        return jax.random.normal(k, shape, jnp.float32) * s

    return {
        "x": nrm(ks[0], (BATCH, SEQ, D), 1.0),
        "c": nrm(ks[1], (BATCH, D), 1.0),
        "ctx": nrm(ks[2], (BATCH, CTX_LEN, D), 1.0),
        "c_ctx": nrm(ks[3], (D,), 1.0),
        "w_mod": nrm(ks[4], (DEPTH, D, N_MOD * D), 0.5 * D ** -0.5),
        "b_mod": nrm(ks[5], (DEPTH, N_MOD * D), 0.02),
        "w_in": nrm(ks[6], (DEPTH, D, D_IN), D ** -0.5),
        "w_out": nrm(ks[7], (DEPTH, D_MIX, D), D_MIX ** -0.5),
        "hy_conv_w": nrm(ks[8], (DEPTH, SHORT_CONV, 3 * D_HYENA), SHORT_CONV ** -0.5),
        "hy_conv_b": nrm(ks[9], (DEPTH, 3 * D_HYENA), 0.02),
        "hy_w1": nrm(ks[10], (DEPTH, HY_FEAT, HY_FILTER_HIDDEN), HY_FEAT ** -0.5),
        "hy_b1": nrm(ks[11], (DEPTH, HY_FILTER_HIDDEN), 0.1),
        "hy_w2": nrm(ks[12], (DEPTH, HY_FILTER_HIDDEN, HY_FILTER_HIDDEN), HY_FILTER_HIDDEN ** -0.5),
        "hy_b2": nrm(ks[13], (DEPTH, HY_FILTER_HIDDEN), 0.1),
        "hy_w3": nrm(ks[14], (DEPTH, HY_FILTER_HIDDEN, HY_ORDER * 2 * D_HYENA), HY_FILTER_HIDDEN ** -0.5),
        "hy_b3": nrm(ks[15], (DEPTH, HY_ORDER * 2 * D_HYENA), 0.02),
        "hy_bias": nrm(ks[16], (DEPTH, HY_ORDER, D_HYENA), 0.5),
        "hgrn_lower_bounds": nrm(ks[17], (2, DEPTH, D_HGRN), 0.5),
        "hgrn_norm_w": 1.0 + nrm(ks[18], (DEPTH, HGRN_HEAD_DIM), 0.02),
        "router_w": nrm(ks[19], (D, N_EXPERTS), D ** -0.5),
        "router_bias": nrm(ks[20], (N_EXPERTS,), 0.01),
        "moe_w_gate": nrm(ks[21], (DEPTH, N_EXPERTS, D, D_EXPERT), D ** -0.5),
        "moe_w_up": nrm(ks[22], (DEPTH, N_EXPERTS, D, D_EXPERT), D ** -0.5),
        "moe_w_down": nrm(ks[23], (DEPTH, N_EXPERTS, D_EXPERT, D), D_EXPERT ** -0.5),
        "final_norm_w": 1.0 + nrm(ks[24], (D,), 0.02),
    }


def reference(x, c, ctx, c_ctx, w_mod, b_mod, w_in, w_out, hy_conv_w, hy_conv_b,
              hy_w1, hy_b1, hy_w2, hy_b2, hy_w3, hy_b3, hy_bias, hgrn_lower_bounds,
              hgrn_norm_w, router_w, router_bias, moe_w_gate, moe_w_up, moe_w_down,
              final_norm_w):
    bsz = x.shape[0]
    lb_soft = jax.nn.softmax(hgrn_lower_bounds.astype(jnp.float32), axis=1)
    lower = jnp.cumsum(lb_soft, axis=1) - lb_soft[:, :1]
    zero_state = jnp.zeros((bsz, HGRN_HEADS, HGRN_HEAD_DIM, HGRN_HEAD_DIM), jnp.float32)

    xc = ctx
    for l in range(DEPTH):
        last = l == DEPTH - 1
        mod_lat = jax.nn.silu(c) @ w_mod[l] + b_mod[l]
        mod_ctx = jax.nn.silu(c_ctx) @ w_mod[l] + b_mod[l]
        sh1, sc1, g1, sh2, sc2, g2 = jnp.split(mod_lat[:, None, :], N_MOD, axis=-1)
        sh1c, sc1c, g1c, sh2c, sc2c, g2c = jnp.split(mod_ctx, N_MOD, axis=-1)

        p_lat = (_rms(x) * (1.0 + sc1) + sh1) @ w_in[l]
        p_ctx = (_rms(xc) * (1.0 + sc1c) + sh1c) @ w_in[l]
        hy_lat, hg_lat = p_lat[..., :3 * D_HYENA], p_lat[..., 3 * D_HYENA:]
        hy_ctx, hg_ctx = p_ctx[..., :3 * D_HYENA], p_ctx[..., 3 * D_HYENA:]

        lb_f = lower[0, l].reshape(HGRN_HEADS, HGRN_HEAD_DIM)
        lb_b = lower[1, l].reshape(HGRN_HEADS, HGRN_HEAD_DIM)

        o_ctx, g_ctx, s_f, s_b = _hgrn_mixer(hg_ctx, lb_f, lb_b, zero_state, zero_state)
        o_lat, g_lat, _, _ = _hgrn_mixer(hg_lat, lb_f, lb_b, s_f, s_b)
        hg_out_lat = _hgrn_out(o_lat, g_lat, hgrn_norm_w[l])

        hy_out_lat = _hyena(hy_lat, hy_conv_w[l], hy_conv_b[l], hy_w1[l], hy_b1[l], hy_w2[l],
                            hy_b2[l], hy_w3[l], hy_b3[l], hy_bias[l], grid=True)

        x = x + g1 * (jnp.concatenate([hy_out_lat, hg_out_lat], axis=-1) @ w_out[l])
        x = x + g2 * _moe(_rms(x) * (1.0 + sc2) + sh2, router_w, router_bias,
                          moe_w_gate[l], moe_w_up[l], moe_w_down[l])

        if not last:
            hy_out_ctx = _hyena(hy_ctx, hy_conv_w[l], hy_conv_b[l], hy_w1[l], hy_b1[l], hy_w2[l],
                                hy_b2[l], hy_w3[l], hy_b3[l], hy_bias[l], grid=False)
            hg_out_ctx = _hgrn_out(o_ctx, g_ctx, hgrn_norm_w[l])
            xc = xc + g1c * (jnp.concatenate([hy_out_ctx, hg_out_ctx], axis=-1) @ w_out[l])
            xc = xc + g2c * _moe(_rms(xc) * (1.0 + sc2c) + sh2c, router_w, router_bias,
                                 moe_w_gate[l], moe_w_up[l], moe_w_down[l])

    return _rms(x, final_norm_w)
```

```python
import functools
import math

import numpy as np
import jax
import jax.numpy as jnp
from jax import lax
from jax.experimental import pallas as pl
from jax.experimental.pallas import tpu as pltpu

F32 = jnp.float32
BF16 = jnp.bfloat16
HIGHEST = lax.Precision.HIGHEST

GRID_W = 64
NORM_EPS = 1e-6
N_MOD = 6
HY_ORDER = 2
HY_BANDS = 16
HY_TARGET = 1e-2
HY_FAST_PCT = 0.3
HY_SLOW_PCT = 1.5
HEAD_DIM = 128
N_EXPERTS = 16
N_GROUPS = 4
EXPERTS_PER_GROUP = N_EXPERTS // N_GROUPS
LANES = 128
MOD_ROWS = 16
HGRN_CHUNK = 128
VMEM_LIMIT = 56 << 20


def _cparams(*sem):
    return pltpu.CompilerParams(dimension_semantics=sem, vmem_limit_bytes=VMEM_LIMIT)


def _sigmoid(x):
    return 1.0 / (1.0 + jnp.exp(-x))


def _dot(a, b, **kw):
    return jnp.dot(a, b, preferred_element_type=F32, **kw)


def _dot_nt(a, b):
    return lax.dot_general(a, b, (((1,), (1,)), ((), ())), preferred_element_type=F32)


def _mod_kernel(c_ref, w_ref, b_ref, o_ref):
    c = c_ref[...]
    cs = c * _sigmoid(c)
    o_ref[0] = _dot(cs, w_ref[0], precision=HIGHEST) + b_ref[0]


def _modulation(cc, w_mod, b_mod):
    depth, d, n = w_mod.shape
    tn = n // 3
    return pl.pallas_call(
        _mod_kernel,
        out_shape=jax.ShapeDtypeStruct((depth, MOD_ROWS, n), F32),
        grid=(depth, n // tn),
        in_specs=[
            pl.BlockSpec((MOD_ROWS, d), lambda l, j: (0, 0)),
            pl.BlockSpec((1, d, tn), lambda l, j: (l, 0, j)),
            pl.BlockSpec((1, 1, tn), lambda l, j: (l, 0, j)),
        ],
        out_specs=pl.BlockSpec((1, MOD_ROWS, tn), lambda l, j: (l, 0, j)),
        compiler_params=_cparams("parallel", "parallel"),
        name="modulation",
    )(cc, w_mod, b_mod.reshape(depth, 1, n))


def _dft_kernel(ac_ref, as_ref, bc_ref, bs_ref, c_ref, s_ref, *, n1):
    bc = bc_ref[...]
    bs = bs_ref[...]
    ac = ac_ref[...]
    as_ = as_ref[...]
    for j in range(n1):
        a_c = ac[:, j:j + 1]
        a_s = as_[:, j:j + 1]
        c_ref[:, j * LANES:(j + 1) * LANES] = (a_c * bc - a_s * bs).astype(BF16)
        s_ref[:, j * LANES:(j + 1) * LANES] = (a_s * bc + a_c * bs).astype(BF16)


def _dft_tables(L):
    n1 = L // LANES
    period = 4 * L
    r = np.arange(L, dtype=np.int64)[:, None]
    c1 = np.arange(n1, dtype=np.int64)[None, :]
    c0 = np.arange(LANES, dtype=np.int64)[None, :]

    def cs(phase):
        ang = (phase % period).astype(np.float64) * (2.0 * np.pi / period)
        return np.cos(ang).astype(np.float32), np.sin(ang).astype(np.float32)

    fa = cs((2 * r + 1) * (LANES * c1))
    fb = cs((2 * r + 1) * c0)
    ta = cs((2 * LANES * c1) * r)
    tb = cs((2 * c0 + 1) * r)
    return (fa, fb), (ta, tb)


def _dft_matrices(L):
    n1 = L // LANES
    tr = min(L, 256)
    outs = []
    for (ac, as_), (bc, bs) in _dft_tables(L):
        c, s = pl.pallas_call(
            functools.partial(_dft_kernel, n1=n1),
            out_shape=(jax.ShapeDtypeStruct((L, L), BF16), jax.ShapeDtypeStruct((L, L), BF16)),
            grid=(L // tr,),
            in_specs=[
                pl.BlockSpec((tr, n1), lambda i: (i, 0)),
                pl.BlockSpec((tr, n1), lambda i: (i, 0)),
                pl.BlockSpec((tr, LANES), lambda i: (i, 0)),
                pl.BlockSpec((tr, LANES), lambda i: (i, 0)),
            ],
            out_specs=(pl.BlockSpec((tr, L), lambda i: (i, 0)), pl.BlockSpec((tr, L), lambda i: (i, 0))),
            compiler_params=_cparams("parallel"),
            name="dft_tables",
        )(jnp.asarray(ac), jnp.asarray(as_), jnp.asarray(bc), jnp.asarray(bs))
        outs.append((c, s))
    return outs


def _inproj_kernel(x_ref, mod_ref, w_ref, cw_ref, cb_ref, u_ref, hg_ref, *, period, n_hy, tn):
    x = x_ref[0]
    tm = x.shape[0]
    xn = x * lax.rsqrt(jnp.mean(x * x, axis=-1, keepdims=True) + NORM_EPS)
    m = mod_ref[0]
    a = (xn * (1.0 + m[1:2]) + m[0:1]).astype(BF16)
    pos = lax.broadcasted_iota(jnp.int32, (tm, 1), 0) % period
    first = pos == 0
    last = pos == period - 1
    n_total = w_ref.shape[1]
    for j in range(n_total // tn):
        p = _dot(a, w_ref[:, j * tn:(j + 1) * tn])
        if j * tn < n_hy:
            cw = cw_ref[:, j * tn:(j + 1) * tn]
            prev = jnp.where(first, 0.0, pltpu.roll(p, 1, 0))
            nxt = jnp.where(last, 0.0, pltpu.roll(p, tm - 1, 0))
            u_ref[0, :, j * tn:(j + 1) * tn] = (
                prev * cw[0:1] + p * cw[1:2] + nxt * cw[2:3] + cb_ref[:, j * tn:(j + 1) * tn])
        else:
            hg_ref[0, :, j * tn - n_hy:(j + 1) * tn - n_hy] = p


def _inproj(x, mod_l, mod_row, w_in, conv_w, conv_b, *, period, tm):
    bsz, L, d = x.shape
    n_hy = conv_w.shape[1]
    n_all = w_in.shape[1]
    tn = n_hy // 3
    assert tm % period == 0 or period == L == tm
    return pl.pallas_call(
        functools.partial(_inproj_kernel, period=period, n_hy=n_hy, tn=tn),
        out_shape=(jax.ShapeDtypeStruct((bsz, L, n_hy), F32),
                   jax.ShapeDtypeStruct((bsz, L, n_all - n_hy), F32)),
        grid=(bsz, L // tm),
        in_specs=[
            pl.BlockSpec((1, tm, d), lambda b, i: (b, i, 0)),
            pl.BlockSpec((1, N_MOD, d), lambda b, i: (mod_row(b), 0, 0)),
            pl.BlockSpec((d, n_all), lambda b, i: (0, 0)),
            pl.BlockSpec((3, n_hy), lambda b, i: (0, 0)),
            pl.BlockSpec((1, n_hy), lambda b, i: (0, 0)),
        ],
        out_specs=(pl.BlockSpec((1, tm, n_hy), lambda b, i: (b, i, 0)),
                   pl.BlockSpec((1, tm, n_all - n_hy), lambda b, i: (b, i, 0))),
        compiler_params=_cparams("parallel", "parallel"),
        name="inproj",
    )(x, mod_l, w_in, conv_w, conv_b.reshape(1, n_hy))


def _filter_kernel(z_ref, w1_ref, b1_ref, w2_ref, b2_ref, w3f_ref, b3f_ref, w3b_ref, b3b_ref,
                   dl_ref, o_ref):
    z = z_ref[...]
    h = jnp.sin(_dot(z, w1_ref[...], precision=HIGHEST) + b1_ref[...])
    h = jnp.sin(_dot(h, w2_ref[...], precision=HIGHEST) + b2_ref[...])
    hf = _dot(h, w3f_ref[...], precision=HIGHEST) + b3f_ref[...]
    hb = _dot(h, w3b_ref[...], precision=HIGHEST) + b3b_ref[...]
    win = jnp.exp(-z[:, 0:1] * dl_ref[...])
    hf = hf * win
    hb = hb * win
    nrm = (jnp.sum(jnp.abs(hf), axis=0, keepdims=True)
           + jnp.sum(jnp.abs(hb), axis=0, keepdims=True))
    inv = 1.0 / nrm
    hf = hf * inv
    row = lax.broadcasted_iota(jnp.int32, (z.shape[0], 1), 0)
    hb0 = jnp.where(row == 0, 0.0, hb * inv)
    o_ref[0, 0] = hf + hb0
    o_ref[0, 1] = hf - hb0


def _hyena_filter_taps(L, w1, b1, w2, b2, w3, b3):
    nfeat, hid = w1.shape
    c = w3.shape[1] // (2 * HY_ORDER)
    tc = min(c, 256)
    nct = c // tc
    t = jnp.linspace(0.0, 1.0, L, dtype=F32)
    n = jnp.arange(L, dtype=F32)
    freqs = jnp.linspace(1e-4, HY_BANDS - 1, HY_BANDS, dtype=F32)
    ang = (2.0 * math.pi / L) * n[:, None] * freqs[None, :]
    z = jnp.concatenate([t[:, None], jnp.cos(ang), -jnp.sin(ang)], axis=-1)
    z = jnp.pad(z, ((0, 0), (0, LANES - nfeat)))
    w1p = jnp.pad(w1, ((0, LANES - nfeat), (0, 0)))
    deltas = jnp.abs(jnp.linspace(math.log(HY_TARGET) / HY_FAST_PCT, math.log(HY_TARGET) / HY_SLOW_PCT,
                                  c, dtype=F32)).reshape(1, c)
    full = lambda shape: pl.BlockSpec(shape, lambda o, j: (0,) * len(shape))
    return pl.pallas_call(
        _filter_kernel,
        out_shape=jax.ShapeDtypeStruct((HY_ORDER, 2, L, c), F32),
        grid=(HY_ORDER, nct),
        in_specs=[
            full((L, LANES)), full((LANES, hid)), full((1, hid)), full((hid, hid)), full((1, hid)),
            pl.BlockSpec((hid, tc), lambda o, j: (0, o * 2 * nct + j)),
            pl.BlockSpec((1, tc), lambda o, j: (0, o * 2 * nct + j)),
            pl.BlockSpec((hid, tc), lambda o, j: (0, o * 2 * nct + nct + j)),
            pl.BlockSpec((1, tc), lambda o, j: (0, o * 2 * nct + nct + j)),
            pl.BlockSpec((1, tc), lambda o, j: (0, j)),
        ],
        out_specs=pl.BlockSpec((1, 2, L, tc), lambda o, j: (o, 0, 0, j)),
        compiler_params=_cparams("parallel", "parallel"),
        name="hyena_filter",
    )(z, w1p, b1.reshape(1, hid), w2, b2.reshape(1, hid), w3, b3.reshape(1, -1), w3, b3.reshape(1, -1),
      deltas)


def _kspec_kernel(cf_ref, sf_ref, h_ref, o_ref, *, scale):
    o_ref[0, 0] = _dot(cf_ref[...], h_ref[0, 0].astype(BF16)) * scale
    o_ref[0, 1] = _dot(sf_ref[...], h_ref[0, 1].astype(BF16)) * scale


def _filter_spectrum(cft, sft, taps):
    _, _, L, c = taps.shape
    tk = min(L, 512)
    return pl.pallas_call(
        functools.partial(_kspec_kernel, scale=1.0 / L),
        out_shape=jax.ShapeDtypeStruct((HY_ORDER, 2, L, c), F32),
        grid=(HY_ORDER, L // tk),
        in_specs=[
            pl.BlockSpec((tk, L), lambda o, i: (i, 0)),
            pl.BlockSpec((tk, L), lambda o, i: (i, 0)),
            pl.BlockSpec((1, 2, L, c), lambda o, i: (o, 0, 0, 0)),
        ],
        out_specs=pl.BlockSpec((1, 2, tk, c), lambda o, i: (o, 0, i, 0)),
        compiler_params=_cparams("parallel", "parallel"),
        name="hyena_filter_spectrum",
    )(cft, sft, taps)


def _hy_fwd_kernel(cf_ref, sf_ref, v_ref, k_ref, p_ref, q_ref):
    v = v_ref[0].astype(BF16)
    a = _dot(cf_ref[...], v)
    b = _dot(sf_ref[...], v)
    kr = k_ref[0, 0]
    ks = k_ref[0, 1]
    p_ref[0] = (a * kr - b * ks).astype(BF16)
    q_ref[0] = (a * ks + b * kr).astype(BF16)


def _hy_forward(cft, sft, src, src_col, kspec, order):
    bsz, L, _ = src.shape
    c = kspec.shape[-1]
    tk = min(L, 512)
    return pl.pallas_call(
        _hy_fwd_kernel,
        out_shape=(jax.ShapeDtypeStruct((bsz, L, c), BF16), jax.ShapeDtypeStruct((bsz, L, c), BF16)),
        grid=(L // tk, bsz),
        in_specs=[
            pl.BlockSpec((tk, L), lambda i, b: (i, 0)),
            pl.BlockSpec((tk, L), lambda i, b: (i, 0)),
            pl.BlockSpec((1, L, c), lambda i, b: (b, 0, src_col)),
            pl.BlockSpec((1, 2, tk, c), lambda i, b: (order, 0, i, 0)),
        ],
        out_specs=(pl.BlockSpec((1, tk, c), lambda i, b: (b, i, 0)),
                   pl.BlockSpec((1, tk, c), lambda i, b: (b, i, 0))),
        compiler_params=_cparams("parallel", "parallel"),
        name="hyena_spectrum",
    )(cft, sft, src, kspec)


def _hy_inv_kernel(cf_ref, sf_ref, p_ref, q_ref, gate_ref, src_ref, d_ref, o_ref):
    conv = _dot(cf_ref[...], p_ref[0]) + _dot(sf_ref[...], q_ref[0])
    o_ref[0] = (gate_ref[0] * (conv + src_ref[0] * d_ref[0])).astype(o_ref.dtype)


def _hy_inverse(cf, sf, p, q, gate, gate_col, src, src_col, d, order, out_dtype):
    bsz, L, c = p.shape
    tm = min(L, 512)
    return pl.pallas_call(
        _hy_inv_kernel,
        out_shape=jax.ShapeDtypeStruct((bsz, L, c), out_dtype),
        grid=(L // tm, bsz),
        in_specs=[
            pl.BlockSpec((tm, L), lambda i, b: (i, 0)),
            pl.BlockSpec((tm, L), lambda i, b: (i, 0)),
            pl.BlockSpec((1, L, c), lambda i, b: (b, 0, 0)),
            pl.BlockSpec((1, L, c), lambda i, b: (b, 0, 0)),
            pl.BlockSpec((1, tm, c), lambda i, b: (b, i, gate_col)),
            pl.BlockSpec((1, tm, c), lambda i, b: (b, i, src_col)),
            pl.BlockSpec((1, 1, c), lambda i, b: (order, 0, 0)),
        ],
        out_specs=pl.BlockSpec((1, tm, c), lambda i, b: (b, i, 0)),
        compiler_params=_cparams("parallel", "parallel"),
        name="hyena_inverse",
    )(cf, sf, p, q, gate, src, d.reshape(HY_ORDER, 1, c))


def _hyena(u, dft, taps_w, d):
    (cft, sft), (cf, sf) = dft
    L = u.shape[1]
    taps = _hyena_filter_taps(L, *taps_w)
    kspec = _filter_spectrum(cft, sft, taps)
    p, q = _hy_forward(cft, sft, u, 0, kspec, 0)
    z = _hy_inverse(cf, sf, p, q, u, 1, u, 0, d, 0, F32)
    p, q = _hy_forward(cft, sft, z, 0, kspec, 1)
    return _hy_inverse(cf, sf, p, q, u, 2, z, 0, d, 1, BF16)


def _anchor_rows(b, n, a):
    C = b.shape[0]
    if n >= 8:
        parts = [jnp.broadcast_to(b[s + a:s + a + 1, :], (n, b.shape[1])) for s in range(0, C, n)]
        return parts[0] if len(parts) == 1 else jnp.concatenate(parts, axis=0)
    pos = lax.broadcasted_iota(jnp.int32, (C, 1), 0) % n
    out = b
    for p_ in range(n):
        sh = a - p_
        if sh != 0:
            out = jnp.where(pos == p_, pltpu.roll(b, (-sh) % C, 0), out)
    return out


def _hgrn_chunk(q_raw, v, f_logit, lb, st_ref, lv, tri, *, reverse):
    C = q_raw.shape[0]
    q = q_raw * _sigmoid(q_raw)
    f = lb + (1.0 - lb) * _sigmoid(f_logit)
    kk = 1.0 - f
    g = jnp.log(f)
    g_hi = g.astype(BF16)
    g_lo = (g - g_hi.astype(F32)).astype(BF16)
    bb = _dot(tri, jnp.concatenate([g_hi, g_lo], axis=1))
    b = bb[:, :HEAD_DIM] + bb[:, HEAD_DIM:]

    scores = jnp.zeros((C, C), F32)
    n = 2
    level = 1
    while n <= C:
        a = n // 2 if reverse else n // 2 - 1
        e = jnp.exp(-jnp.abs(b - _anchor_rows(b, n, a)))
        pl_ = _dot_nt((q * e).astype(BF16), (kk * e).astype(BF16))
        scores = jnp.where(lv == level, pl_, scores)
        n *= 2
        level += 1

    st = st_ref[...]
    vb = v.astype(BF16)
    o = _dot(scores.astype(BF16), vb)
    o = o + jnp.sum(q * kk, axis=-1, keepdims=True) * v
    o = o + _dot_nt((q * jnp.exp(b)).astype(BF16), st.astype(BF16))
    b_end = b[0:1, :] if reverse else b[C - 1:C, :]
    kd = (kk * jnp.exp(b_end - b)).astype(BF16)
    st_ref[...] = st * jnp.exp(b_end) + _dot(v.T.astype(BF16), kd)
    return o


def _hgrn_kernel(qc, ic, gc, fc, bc, ql, il, gl, fl, bl, lb_ref, nw_ref, oc_ref, ol_ref,
                 of_s, ob_s, stf, stb, *, n_ctx, n_lat):
    C = HGRN_CHUNK
    row = lax.broadcasted_iota(jnp.int32, (C, C), 0)
    col = lax.broadcasted_iota(jnp.int32, (C, C), 1)
    x = row ^ col
    lvl = jnp.zeros((C, C), jnp.int32)
    n = 1
    while n < C:
        lvl = lvl + (x >= n).astype(jnp.int32)
        n *= 2
    lv_f = jnp.where(row > col, lvl, 0)
    lv_b = jnp.where(row < col, lvl, 0)
    tri_f = (col <= row).astype(BF16)
    tri_b = (col >= row).astype(BF16)
    lb_f = lb_ref[0, 0:1, :]
    lb_b = lb_ref[0, 1:2, :]
    stf[...] = jnp.zeros_like(stf)
    stb[...] = jnp.zeros_like(stb)

    def run(q_ref, i_ref, f_ref, b_ref, n_chunks, base):
        def body(j, carry):
            rf = pl.ds(pl.multiple_of(j * C, C), C)
            rb = pl.ds(pl.multiple_of((n_chunks - 1 - j) * C, C), C)
            of_s[pl.ds(pl.multiple_of(base + j * C, C), C), :] = _hgrn_chunk(
                q_ref[0, rf, :], i_ref[0, rf, :], f_ref[0, rf, :], lb_f, stf, lv_f, tri_f, reverse=False)
            ob_s[pl.ds(pl.multiple_of(base + (n_chunks - 1 - j) * C, C), C), :] = _hgrn_chunk(
                q_ref[0, rb, :], i_ref[0, rb, :], b_ref[0, rb, :], lb_b, stb, lv_b, tri_b, reverse=True)
            return carry
        lax.fori_loop(0, n_chunks, body, 0)

    run(qc, ic, fc, bc, n_ctx, 0)
    run(ql, il, fl, bl, n_lat, n_ctx * C)

    nw = nw_ref[...]

    def finish(g_ref, o_ref, n_chunks, base):
        def body(j, carry):
            r = pl.ds(pl.multiple_of(j * C, C), C)
            rs = pl.ds(pl.multiple_of(base + j * C, C), C)
            o = of_s[rs, :] + ob_s[rs, :]
            o = o * lax.rsqrt(jnp.mean(o * o, axis=-1, keepdims=True) + NORM_EPS) * nw
            g = g_ref[0, r, :]
            o_ref[0, r, :] = (o * (g * _sigmoid(g))).astype(o_ref.dtype)
            return carry
        lax.fori_loop(0, n_chunks, body, 0)

    finish(gc, oc_ref, n_ctx, 0)
    finish(gl, ol_ref, n_lat, n_ctx * C)


def _hgrn(hg_ctx, hg_lat, lb, norm_w):
    bsz, lc, n5 = hg_ctx.shape
    ll = hg_lat.shape[1]
    dg = n5 // 5
    nh = dg // HEAD_DIM
    C = HGRN_CHUNK
    assert lc % C == 0 and ll % C == 0

    def slab(L, part):
        return pl.BlockSpec((1, L, HEAD_DIM), lambda b, h: (b, 0, part * nh + h))

    return pl.pallas_call(
        functools.partial(_hgrn_kernel, n_ctx=lc // C, n_lat=ll // C),
        out_shape=(jax.ShapeDtypeStruct((bsz, lc, dg), BF16), jax.ShapeDtypeStruct((bsz, ll, dg), BF16)),
        grid=(bsz, nh),
        in_specs=[slab(lc, p) for p in range(5)] + [slab(ll, p) for p in range(5)] + [
            pl.BlockSpec((1, 2, HEAD_DIM), lambda b, h: (h, 0, 0)),
            pl.BlockSpec((1, HEAD_DIM), lambda b, h: (0, 0)),
        ],
        out_specs=(pl.BlockSpec((1, lc, HEAD_DIM), lambda b, h: (b, 0, h)),
                   pl.BlockSpec((1, ll, HEAD_DIM), lambda b, h: (b, 0, h))),
        scratch_shapes=[
            pltpu.VMEM((lc + ll, HEAD_DIM), F32),
            pltpu.VMEM((lc + ll, HEAD_DIM), F32),
            pltpu.VMEM((HEAD_DIM, HEAD_DIM), F32),
            pltpu.VMEM((HEAD_DIM, HEAD_DIM), F32),
        ],
        compiler_params=_cparams("parallel", "parallel"),
        name="hgrn2",
    )(*([hg_ctx] * 5), *([hg_lat] * 5), lb, norm_w.reshape(1, HEAD_DIM))


def _outproj_kernel(x_ref, hy_ref, hg_ref, mod_ref, w_ref, rw_ref, x1_ref, t_ref, lg_ref):
    c = hy_ref.shape[-1]
    mix = _dot(hy_ref[0], w_ref[:c, :]) + _dot(hg_ref[0], w_ref[c:, :])
    m = mod_ref[0]
    x1 = x_ref[0] + m[2:3] * mix
    x1_ref[0] = x1
    xn = x1 * lax.rsqrt(jnp.mean(x1 * x1, axis=-1, keepdims=True) + NORM_EPS)
    t = xn * (1.0 + m[4:5]) + m[3:4]
    t_ref[0] = t.astype(BF16)
    lg_ref[0] = _dot(t, rw_ref[...], precision=HIGHEST)


def _outproj(x, hy, hg, mod_l, mod_row, w_out, router_w, *, tm):
    bsz, L, d = x.shape
    c = hy.shape[-1]
    cg = hg.shape[-1]
    ne = router_w.shape[1]
    return pl.pallas_call(
        _outproj_kernel,
        out_shape=(jax.ShapeDtypeStruct((bsz, L, d), F32), jax.ShapeDtypeStruct((bsz, L, d), BF16),
                   jax.ShapeDtypeStruct((bsz, L, ne), F32)),
        grid=(bsz, L // tm),
        in_specs=[
            pl.BlockSpec((1, tm, d), lambda b, i: (b, i, 0)),
            pl.BlockSpec((1, tm, c), lambda b, i: (b, i, 0)),
            pl.BlockSpec((1, tm, cg), lambda b, i: (b, i, 0)),
            pl.BlockSpec((1, N_MOD, d), lambda b, i: (mod_row(b), 0, 0)),
            pl.BlockSpec((c + cg, d), lambda b, i: (0, 0)),
            pl.BlockSpec((d, ne), lambda b, i: (0, 0)),
        ],
        out_specs=(pl.BlockSpec((1, tm, d), lambda b, i: (b, i, 0)),
                   pl.BlockSpec((1, tm, d), lambda b, i: (b, i, 0)),
                   pl.BlockSpec((1, tm, ne), lambda b, i: (b, i, 0))),
        compiler_params=_cparams("parallel", "parallel"),
        name="outproj",
    )(x, hy, hg, mod_l, w_out, router_w)


def _first_argmax(vals, lane, sentinel):
    m = jnp.max(vals, axis=-1, keepdims=True)
    idx = jnp.min(jnp.where(vals == m, lane, sentinel), axis=-1, keepdims=True)
    return m, idx


def _moe_gates(logits, rbias):
    tm, ne = logits.shape
    neg = -jnp.inf
    mx = jnp.max(logits, axis=-1, keepdims=True)
    ex = jnp.exp(logits - mx)
    scores = ex / jnp.sum(ex, axis=-1, keepdims=True)
    sel = scores + rbias
    lane = lax.broadcasted_iota(jnp.int32, (tm, ne), 1)
    grp = lane // EXPERTS_PER_GROUP
    best_s = jnp.full((tm, 1), neg, F32)
    best_g = jnp.zeros((tm, 1), jnp.int32)
    for gi in range(N_GROUPS):
        mg = jnp.where(grp == gi, sel, neg)
        m1, i1 = _first_argmax(mg, lane, ne)
        m2 = jnp.max(jnp.where(lane == i1, neg, mg), axis=-1, keepdims=True)
        gs = m1 + m2
        upd = gs > best_s
        best_g = jnp.where(upd, gi, best_g)
        best_s = jnp.where(upd, gs, best_s)
    sg = jnp.where(grp == best_g, sel, neg)
    _, i1 = _first_argmax(sg, lane, ne)
    _, i2 = _first_argmax(jnp.where(lane == i1, neg, sg), lane, ne)
    chosen = (lane == i1) | (lane == i2)
    w = jnp.where(chosen, scores, 0.0)
    return w / jnp.sum(w, axis=-1, keepdims=True)


def _moe_kernel(t_ref, lg_ref, rb_ref, wg_ref, wu_ref, wd_ref, x1_ref, mod_ref, o_ref, gates_s, acc_s):
    e = pl.program_id(2)

    @pl.when(e == 0)
    def _():
        gates_s[...] = _moe_gates(lg_ref[0], rb_ref[...])
        acc_s[...] = jnp.zeros_like(acc_s)

    t = t_ref[0]
    gates = gates_s[...]
    lane = lax.broadcasted_iota(jnp.int32, gates.shape, 1)
    gcol = jnp.sum(jnp.where(lane == e, gates, 0.0), axis=-1, keepdims=True)
    hgate = _dot(t, wg_ref[0])
    hup = _dot(t, wu_ref[0])
    act = hgate * _sigmoid(hgate) * hup * gcol
    acc_s[...] += _dot(act.astype(BF16), wd_ref[0])

    @pl.when(e == pl.num_programs(2) - 1)
    def _():
        o_ref[0] = x1_ref[0] + mod_ref[0][5:6] * acc_s[...]


def _moe(t, logits, rbias, wg, wu, wd, x1, mod_l, mod_row, *, tm):
    bsz, L, d = x1.shape
    ne, _, f = wg.shape
    return pl.pallas_call(
        _moe_kernel,
        out_shape=jax.ShapeDtypeStruct((bsz, L, d), F32),
        grid=(bsz, L // tm, ne),
        in_specs=[
            pl.BlockSpec((1, tm, d), lambda b, i, e: (b, i, 0)),
            pl.BlockSpec((1, tm, ne), lambda b, i, e: (b, i, 0)),
            pl.BlockSpec((1, ne), lambda b, i, e: (0, 0)),
            pl.BlockSpec((1, d, f), lambda b, i, e: (e, 0, 0)),
            pl.BlockSpec((1, d, f), lambda b, i, e: (e, 0, 0)),
            pl.BlockSpec((1, f, d), lambda b, i, e: (e, 0, 0)),
            pl.BlockSpec((1, tm, d), lambda b, i, e: (b, i, 0)),
            pl.BlockSpec((1, N_MOD, d), lambda b, i, e: (mod_row(b), 0, 0)),
        ],
        out_specs=pl.BlockSpec((1, tm, d), lambda b, i, e: (b, i, 0)),
        scratch_shapes=[pltpu.VMEM((tm, ne), F32), pltpu.VMEM((tm, d), F32)],
        compiler_params=_cparams("parallel", "parallel", "arbitrary"),
        name="moe",
    )(t, logits, rbias.reshape(1, ne), wg, wu, wd, x1, mod_l)


def _final_norm_kernel(x_ref, w_ref, o_ref):
    x = x_ref[0]
    o_ref[0] = x * lax.rsqrt(jnp.mean(x * x, axis=-1, keepdims=True) + NORM_EPS) * w_ref[...]


def _final_norm(x, w, *, tm):
    bsz, L, d = x.shape
    return pl.pallas_call(
        _final_norm_kernel,
        out_shape=jax.ShapeDtypeStruct(x.shape, F32),
        grid=(bsz, L // tm),
        in_specs=[pl.BlockSpec((1, tm, d), lambda b, i: (b, i, 0)), pl.BlockSpec((1, d), lambda b, i: (0, 0))],
        out_specs=pl.BlockSpec((1, tm, d), lambda b, i: (b, i, 0)),
        compiler_params=_cparams("parallel", "parallel"),
        name="final_norm",
    )(x, w.reshape(1, d))


def kernel(x, c, ctx, c_ctx, w_mod, b_mod, w_in, w_out, hy_conv_w, hy_conv_b, hy_w1, hy_b1, hy_w2, hy_b2, hy_w3, hy_b3, hy_bias, hgrn_lower_bounds, hgrn_norm_w, router_w, router_bias, moe_w_gate, moe_w_up, moe_w_down, final_norm_w):
    bsz, seq, d = x.shape
    ctx_len = ctx.shape[1]
    depth = w_mod.shape[0]
    dg = hgrn_lower_bounds.shape[-1]
    nh = dg // HEAD_DIM
    assert bsz + 1 <= MOD_ROWS and seq % GRID_W == 0

    cc = jnp.zeros((MOD_ROWS, d), F32).at[:bsz].set(c).at[bsz].set(c_ctx)
    mod = _modulation(cc, w_mod, b_mod).reshape(depth, MOD_ROWS, N_MOD, d)
    lat_row = lambda b: b
    ctx_row = lambda b: bsz

    lb_soft = jax.nn.softmax(hgrn_lower_bounds.astype(F32), axis=1)
    lower = jnp.cumsum(lb_soft, axis=1) - lb_soft[:, :1]
    lower = lower.reshape(2, depth, nh, HEAD_DIM).transpose(1, 2, 0, 3)

    dft_lat = _dft_matrices(seq)
    dft_ctx = _dft_matrices(ctx_len)
    tm_lat = min(seq, 256)
    tm_moe = min(seq, 512)

    w_in_b = w_in.astype(BF16)
    w_out_b = w_out.astype(BF16)
    wg_b = moe_w_gate.astype(BF16)
    wu_b = moe_w_up.astype(BF16)
    wd_b = moe_w_down.astype(BF16)

    xc = ctx
    for l in range(depth):
        last = l == depth - 1
        u_lat, hg_lat = _inproj(x, mod[l], lat_row, w_in_b[l], hy_conv_w[l], hy_conv_b[l],
                                period=GRID_W, tm=tm_lat)
        u_ctx, hg_ctx = _inproj(xc, mod[l], ctx_row, w_in_b[l], hy_conv_w[l], hy_conv_b[l],
                                period=ctx_len, tm=ctx_len)
        o_ctx, o_lat = _hgrn(hg_ctx, hg_lat, lower[l], hgrn_norm_w[l])
        taps_w = (hy_w1[l], hy_b1[l], hy_w2[l], hy_b2[l], hy_w3[l], hy_b3[l])
        hy_lat = _hyena(u_lat, dft_lat, taps_w, hy_bias[l])
        x1, t, lg = _outproj(x, hy_lat, o_lat, mod[l], lat_row, w_out_b[l], router_w, tm=tm_lat)
        x = _moe(t, lg, router_bias, wg_b[l], wu_b[l], wd_b[l], x1, mod[l], lat_row, tm=tm_moe)
        if not last:
            hy_ctx = _hyena(u_ctx, dft_ctx, taps_w, hy_bias[l])
            x1, t, lg = _outproj(xc, hy_ctx, o_ctx, mod[l], ctx_row, w_out_b[l], router_w, tm=ctx_len)
            xc = _moe(t, lg, router_bias, wg_b[l], wu_b[l], wd_b[l], x1, mod[l], ctx_row, tm=ctx_len)

    return _final_norm(x, final_norm_w, tm=tm_lat)
```

```python
import functools
import math

import numpy as np
import jax
import jax.numpy as jnp
from jax import lax
from jax.experimental import pallas as pl
from jax.experimental.pallas import tpu as pltpu

F32 = jnp.float32
BF16 = jnp.bfloat16
HIGHEST = lax.Precision.HIGHEST

GRID_W = 64
NORM_EPS = 1e-6
N_MOD = 6
HY_ORDER = 2
HY_BANDS = 16
HY_TARGET = 1e-2
HY_FAST_PCT = 0.3
HY_SLOW_PCT = 1.5
HEAD_DIM = 128
N_EXPERTS = 16
N_GROUPS = 4
EXPERTS_PER_GROUP = N_EXPERTS // N_GROUPS
LANES = 128
MOD_ROWS = 16
HGRN_CHUNK = 128
VMEM_LIMIT = 56 << 20


def _cparams(*sem):
    return pltpu.CompilerParams(dimension_semantics=sem, vmem_limit_bytes=VMEM_LIMIT)


def _sigmoid(x):
    return 1.0 / (1.0 + jnp.exp(-x))


def _dot(a, b, **kw):
    return jnp.dot(a, b, preferred_element_type=F32, **kw)


def _resident(shape):
    return pl.BlockSpec(shape, lambda *_: (0,) * len(shape), pipeline_mode=pl.Buffered(1))


def _neg_abs(x):
    bits = lax.bitcast_convert_type(x, jnp.uint32) | jnp.uint32(0x80000000)
    return lax.bitcast_convert_type(bits, F32)


def _dot_nt(a, b):
    return lax.dot_general(a, b, (((1,), (1,)), ((), ())), preferred_element_type=F32)


def _mod_kernel(c_ref, w_ref, b_ref, o_ref):
    c = c_ref[...]
    cs = c * _sigmoid(c)
    o_ref[0] = _dot(cs, w_ref[0], precision=HIGHEST) + b_ref[0]


def _modulation(cc, w_mod, b_mod):
    depth, d, n = w_mod.shape
    tn = n // 3
    return pl.pallas_call(
        _mod_kernel,
        out_shape=jax.ShapeDtypeStruct((depth, MOD_ROWS, n), F32),
        grid=(depth, n // tn),
        in_specs=[
            pl.BlockSpec((MOD_ROWS, d), lambda l, j: (0, 0)),
            pl.BlockSpec((1, d, tn), lambda l, j: (l, 0, j)),
            pl.BlockSpec((1, 1, tn), lambda l, j: (l, 0, j)),
        ],
        out_specs=pl.BlockSpec((1, MOD_ROWS, tn), lambda l, j: (l, 0, j)),
        compiler_params=_cparams("parallel", "parallel"),
        name="modulation",
    )(cc, w_mod, b_mod.reshape(depth, 1, n))


def _dft_kernel(ac_ref, as_ref, bc_ref, bs_ref, c_ref, s_ref, *, n1):
    bc = bc_ref[...]
    bs = bs_ref[...]
    ac = ac_ref[...]
    as_ = as_ref[...]
    for j in range(n1):
        a_c = ac[:, j:j + 1]
        a_s = as_[:, j:j + 1]
        c_ref[:, j * LANES:(j + 1) * LANES] = (a_c * bc - a_s * bs).astype(BF16)
        s_ref[:, j * LANES:(j + 1) * LANES] = (a_s * bc + a_c * bs).astype(BF16)


def _dft_tables(L):
    n1 = L // LANES
    period = 4 * L
    r = np.arange(L, dtype=np.int64)[:, None]
    c1 = np.arange(n1, dtype=np.int64)[None, :]
    c0 = np.arange(LANES, dtype=np.int64)[None, :]

    def cs(phase):
        ang = (phase % period).astype(np.float64) * (2.0 * np.pi / period)
        return np.cos(ang).astype(np.float32), np.sin(ang).astype(np.float32)

    fa = cs((2 * r + 1) * (LANES * c1))
    fb = cs((2 * r + 1) * c0)
    ta = cs((2 * LANES * c1) * r)
    tb = cs((2 * c0 + 1) * r)
    return (fa, fb), (ta, tb)


def _dft_matrices(L):
    n1 = L // LANES
    tr = min(L, 256)
    outs = []
    for (ac, as_), (bc, bs) in _dft_tables(L):
        c, s = pl.pallas_call(
            functools.partial(_dft_kernel, n1=n1),
            out_shape=(jax.ShapeDtypeStruct((L, L), BF16), jax.ShapeDtypeStruct((L, L), BF16)),
            grid=(L // tr,),
            in_specs=[
                pl.BlockSpec((tr, n1), lambda i: (i, 0)),
                pl.BlockSpec((tr, n1), lambda i: (i, 0)),
                pl.BlockSpec((tr, LANES), lambda i: (i, 0)),
                pl.BlockSpec((tr, LANES), lambda i: (i, 0)),
            ],
            out_specs=(pl.BlockSpec((tr, L), lambda i: (i, 0)), pl.BlockSpec((tr, L), lambda i: (i, 0))),
            compiler_params=_cparams("parallel"),
            name="dft_tables",
        )(jnp.asarray(ac), jnp.asarray(as_), jnp.asarray(bc), jnp.asarray(bs))
        outs.append((c, s))
    return outs


def _inproj_kernel(x_ref, mod_ref, w_ref, cw_ref, cb_ref, u_ref, hg_ref, *, period, n_hy, tn):
    x = x_ref[0]
    tm = x.shape[0]
    xn = x * lax.rsqrt(jnp.mean(x * x, axis=-1, keepdims=True) + NORM_EPS)
    m = mod_ref[0]
    a = (xn * (1.0 + m[1:2]) + m[0:1]).astype(BF16)
    pos = lax.broadcasted_iota(jnp.int32, (tm, 1), 0) % period
    first = pos == 0
    last = pos == period - 1
    n_total = w_ref.shape[1]
    for j in range(n_total // tn):
        p = _dot(a, w_ref[:, j * tn:(j + 1) * tn])
        if j * tn < n_hy:
            cw = cw_ref[:, j * tn:(j + 1) * tn]
            prev = jnp.where(first, 0.0, pltpu.roll(p, 1, 0))
            nxt = jnp.where(last, 0.0, pltpu.roll(p, tm - 1, 0))
            u_ref[0, :, j * tn:(j + 1) * tn] = (
                prev * cw[0:1] + p * cw[1:2] + nxt * cw[2:3] + cb_ref[:, j * tn:(j + 1) * tn])
        else:
            hg_ref[0, :, j * tn - n_hy:(j + 1) * tn - n_hy] = p


def _inproj(x, mod_l, mod_row, w_in, conv_w, conv_b, *, period, tm):
    bsz, L, d = x.shape
    n_hy = conv_w.shape[1]
    n_all = w_in.shape[1]
    tn = n_hy // 3
    assert tm % period == 0 or period == L == tm
    return pl.pallas_call(
        functools.partial(_inproj_kernel, period=period, n_hy=n_hy, tn=tn),
        out_shape=(jax.ShapeDtypeStruct((bsz, L, n_hy), F32),
                   jax.ShapeDtypeStruct((bsz, L, n_all - n_hy), F32)),
        grid=(bsz, L // tm),
        in_specs=[
            pl.BlockSpec((1, tm, d), lambda b, i: (b, i, 0)),
            pl.BlockSpec((1, N_MOD, d), lambda b, i: (mod_row(b), 0, 0)),
            _resident((d, n_all)),
            _resident((3, n_hy)),
            _resident((1, n_hy)),
        ],
        out_specs=(pl.BlockSpec((1, tm, n_hy), lambda b, i: (b, i, 0)),
                   pl.BlockSpec((1, tm, n_all - n_hy), lambda b, i: (b, i, 0))),
        compiler_params=_cparams("parallel", "parallel"),
        name="inproj",
    )(x, mod_l, w_in, conv_w, conv_b.reshape(1, n_hy))


def _filter_kernel(z_ref, w1_ref, b1_ref, w2_ref, b2_ref, w3f_ref, b3f_ref, w3b_ref, b3b_ref,
                   dl_ref, o_ref):
    z = z_ref[...]
    h = jnp.sin(_dot(z, w1_ref[...], precision=HIGHEST) + b1_ref[...])
    h = jnp.sin(_dot(h, w2_ref[...], precision=HIGHEST) + b2_ref[...])
    hf = _dot(h, w3f_ref[...], precision=HIGHEST) + b3f_ref[...]
    hb = _dot(h, w3b_ref[...], precision=HIGHEST) + b3b_ref[...]
    win = jnp.exp(-z[:, 0:1] * dl_ref[...])
    hf = hf * win
    hb = hb * win
    nrm = (jnp.sum(jnp.abs(hf), axis=0, keepdims=True)
           + jnp.sum(jnp.abs(hb), axis=0, keepdims=True))
    inv = 1.0 / nrm
    hf = hf * inv
    row = lax.broadcasted_iota(jnp.int32, (z.shape[0], 1), 0)
    hb0 = jnp.where(row == 0, 0.0, hb * inv)
    o_ref[0, 0] = hf + hb0
    o_ref[0, 1] = hf - hb0


def _hyena_filter_taps(L, w1, b1, w2, b2, w3, b3):
    nfeat, hid = w1.shape
    c = w3.shape[1] // (2 * HY_ORDER)
    tc = min(c, 256)
    nct = c // tc
    t = jnp.linspace(0.0, 1.0, L, dtype=F32)
    n = jnp.arange(L, dtype=F32)
    freqs = jnp.linspace(1e-4, HY_BANDS - 1, HY_BANDS, dtype=F32)
    ang = (2.0 * math.pi / L) * n[:, None] * freqs[None, :]
    z = jnp.concatenate([t[:, None], jnp.cos(ang), -jnp.sin(ang)], axis=-1)
    z = jnp.pad(z, ((0, 0), (0, LANES - nfeat)))
    w1p = jnp.pad(w1, ((0, LANES - nfeat), (0, 0)))
    deltas = jnp.abs(jnp.linspace(math.log(HY_TARGET) / HY_FAST_PCT, math.log(HY_TARGET) / HY_SLOW_PCT,
                                  c, dtype=F32)).reshape(1, c)
    full = lambda shape: pl.BlockSpec(shape, lambda o, j: (0,) * len(shape))
    return pl.pallas_call(
        _filter_kernel,
        out_shape=jax.ShapeDtypeStruct((HY_ORDER, 2, L, c), F32),
        grid=(HY_ORDER, nct),
        in_specs=[
            full((L, LANES)), full((LANES, hid)), full((1, hid)), full((hid, hid)), full((1, hid)),
            pl.BlockSpec((hid, tc), lambda o, j: (0, o * 2 * nct + j)),
            pl.BlockSpec((1, tc), lambda o, j: (0, o * 2 * nct + j)),
            pl.BlockSpec((hid, tc), lambda o, j: (0, o * 2 * nct + nct + j)),
            pl.BlockSpec((1, tc), lambda o, j: (0, o * 2 * nct + nct + j)),
            pl.BlockSpec((1, tc), lambda o, j: (0, j)),
        ],
        out_specs=pl.BlockSpec((1, 2, L, tc), lambda o, j: (o, 0, 0, j)),
        compiler_params=_cparams("parallel", "parallel"),
        name="hyena_filter",
    )(z, w1p, b1.reshape(1, hid), w2, b2.reshape(1, hid), w3, b3.reshape(1, -1), w3, b3.reshape(1, -1),
      deltas)


def _kspec_kernel(cf_ref, sf_ref, h_ref, o_ref, *, scale):
    o_ref[0, 0] = _dot(cf_ref[...], h_ref[0, 0].astype(BF16)) * scale
    o_ref[0, 1] = _dot(sf_ref[...], h_ref[0, 1].astype(BF16)) * scale


def _filter_spectrum(cft, sft, taps):
    _, _, L, c = taps.shape
    tk = min(L, 512)
    return pl.pallas_call(
        functools.partial(_kspec_kernel, scale=1.0 / L),
        out_shape=jax.ShapeDtypeStruct((HY_ORDER, 2, L, c), F32),
        grid=(HY_ORDER, L // tk),
        in_specs=[
            pl.BlockSpec((tk, L), lambda o, i: (i, 0)),
            pl.BlockSpec((tk, L), lambda o, i: (i, 0)),
            pl.BlockSpec((1, 2, L, c), lambda o, i: (o, 0, 0, 0)),
        ],
        out_specs=pl.BlockSpec((1, 2, tk, c), lambda o, i: (o, 0, i, 0)),
        compiler_params=_cparams("parallel", "parallel"),
        name="hyena_filter_spectrum",
    )(cft, sft, taps)


def _hy_fwd_kernel(cf_ref, sf_ref, v_ref, k_ref, p_ref, q_ref):
    v = v_ref[0].astype(BF16)
    a = _dot(cf_ref[...], v)
    b = _dot(sf_ref[...], v)
    kr = k_ref[0, 0]
    ks = k_ref[0, 1]
    p_ref[0] = (a * kr - b * ks).astype(BF16)
    q_ref[0] = (a * ks + b * kr).astype(BF16)


def _hy_forward(cft, sft, src, src_col, kspec, order):
    bsz, L, _ = src.shape
    c = kspec.shape[-1]
    tk = min(L, 512)
    return pl.pallas_call(
        _hy_fwd_kernel,
        out_shape=(jax.ShapeDtypeStruct((bsz, L, c), BF16), jax.ShapeDtypeStruct((bsz, L, c), BF16)),
        grid=(L // tk, bsz),
        in_specs=[
            pl.BlockSpec((tk, L), lambda i, b: (i, 0)),
            pl.BlockSpec((tk, L), lambda i, b: (i, 0)),
            pl.BlockSpec((1, L, c), lambda i, b: (b, 0, src_col)),
            pl.BlockSpec((1, 2, tk, c), lambda i, b: (order, 0, i, 0)),
        ],
        out_specs=(pl.BlockSpec((1, tk, c), lambda i, b: (b, i, 0)),
                   pl.BlockSpec((1, tk, c), lambda i, b: (b, i, 0))),
        compiler_params=_cparams("parallel", "parallel"),
        name="hyena_spectrum",
    )(cft, sft, src, kspec)


def _hy_inv_kernel(cf_ref, sf_ref, p_ref, q_ref, gate_ref, src_ref, d_ref, o_ref):
    conv = _dot(cf_ref[...], p_ref[0]) + _dot(sf_ref[...], q_ref[0])
    o_ref[0] = (gate_ref[0] * (conv + src_ref[0] * d_ref[0])).astype(o_ref.dtype)


def _hy_inverse(cf, sf, p, q, gate, gate_col, src, src_col, d, order, out_dtype):
    bsz, L, c = p.shape
    tm = min(L, 512)
    return pl.pallas_call(
        _hy_inv_kernel,
        out_shape=jax.ShapeDtypeStruct((bsz, L, c), out_dtype),
        grid=(L // tm, bsz),
        in_specs=[
            pl.BlockSpec((tm, L), lambda i, b: (i, 0)),
            pl.BlockSpec((tm, L), lambda i, b: (i, 0)),
            pl.BlockSpec((1, L, c), lambda i, b: (b, 0, 0)),
            pl.BlockSpec((1, L, c), lambda i, b: (b, 0, 0)),
            pl.BlockSpec((1, tm, c), lambda i, b: (b, i, gate_col)),
            pl.BlockSpec((1, tm, c), lambda i, b: (b, i, src_col)),
            pl.BlockSpec((1, 1, c), lambda i, b: (order, 0, 0)),
        ],
        out_specs=pl.BlockSpec((1, tm, c), lambda i, b: (b, i, 0)),
        compiler_params=_cparams("parallel", "parallel"),
        name="hyena_inverse",
    )(cf, sf, p, q, gate, src, d.reshape(HY_ORDER, 1, c))


def _hyena(u, dft, taps_w, d):
    (cft, sft), (cf, sf) = dft
    L = u.shape[1]
    taps = _hyena_filter_taps(L, *taps_w)
    kspec = _filter_spectrum(cft, sft, taps)
    p, q = _hy_forward(cft, sft, u, 0, kspec, 0)
    z = _hy_inverse(cf, sf, p, q, u, 1, u, 0, d, 0, F32)
    p, q = _hy_forward(cft, sft, z, 0, kspec, 1)
    return _hy_inverse(cf, sf, p, q, u, 2, z, 0, d, 1, BF16)


def _anchor_rows(b, n, a):
    C = b.shape[0]
    if n >= 8:
        parts = [jnp.broadcast_to(b[s + a:s + a + 1, :], (n, b.shape[1])) for s in range(0, C, n)]
        return parts[0] if len(parts) == 1 else jnp.concatenate(parts, axis=0)
    pos = lax.broadcasted_iota(jnp.int32, (C, 1), 0) % n
    out = b
    for p_ in range(n):
        sh = a - p_
        if sh != 0:
            out = jnp.where(pos == p_, pltpu.roll(b, (-sh) % C, 0), out)
    return out


def _hgrn_chunk(q_raw, v, f_logit, lb, st_ref, lv, tri, *, reverse):
    C = q_raw.shape[0]
    q = q_raw * _sigmoid(q_raw)
    f = lb + (1.0 - lb) * _sigmoid(f_logit)
    kk = 1.0 - f
    g = jnp.log2(f)
    g_hi = g.astype(BF16)
    g_lo = (g - g_hi.astype(F32)).astype(BF16)
    bb = _dot(tri, jnp.concatenate([g_hi, g_lo], axis=1))
    b = bb[:, :HEAD_DIM] + bb[:, HEAD_DIM:]

    qb = q.astype(BF16)
    kb = kk.astype(BF16)
    scores = jnp.zeros((C, C), F32)
    n = 2
    level = 1
    while n <= C:
        a = n // 2 if reverse else n // 2 - 1
        e = jnp.exp2(_neg_abs(b - _anchor_rows(b, n, a))).astype(BF16)
        scores = jnp.where(lv == level, _dot_nt(qb * e, kb * e), scores)
        n *= 2
        level += 1

    st = st_ref[...]
    vb = v.astype(BF16)
    o = _dot(scores.astype(BF16), vb)
    o = o + jnp.sum(q * kk, axis=-1, keepdims=True) * v
    o = o + _dot_nt(qb * jnp.exp2(b).astype(BF16), st.astype(BF16))
    b_end = b[0:1, :] if reverse else b[C - 1:C, :]
    kd = kb * jnp.exp2(b_end - b).astype(BF16)
    st_ref[...] = st * jnp.exp2(b_end) + _dot(v.T.astype(BF16), kd)
    return o


def _hgrn_kernel(qc, ic, gc, fc, bc, ql, il, gl, fl, bl, lb_ref, nw_ref, oc_ref, ol_ref,
                 of_s, ob_s, stf, stb, *, n_ctx, n_lat):
    C = HGRN_CHUNK
    row = lax.broadcasted_iota(jnp.int32, (C, C), 0)
    col = lax.broadcasted_iota(jnp.int32, (C, C), 1)
    x = row ^ col
    lvl = jnp.zeros((C, C), jnp.int32)
    n = 1
    while n < C:
        lvl = lvl + (x >= n).astype(jnp.int32)
        n *= 2
    lv_f = jnp.where(row > col, lvl, 0)
    lv_b = jnp.where(row < col, lvl, 0)
    tri_f = (col <= row).astype(BF16)
    tri_b = (col >= row).astype(BF16)
    lb_f = lb_ref[0, 0:1, :]
    lb_b = lb_ref[0, 1:2, :]
    stf[...] = jnp.zeros_like(stf)
    stb[...] = jnp.zeros_like(stb)

    def run(q_ref, i_ref, f_ref, b_ref, n_chunks, base):
        def body(j, carry):
            rf = pl.ds(pl.multiple_of(j * C, C), C)
            rb = pl.ds(pl.multiple_of((n_chunks - 1 - j) * C, C), C)
            of_s[pl.ds(pl.multiple_of(base + j * C, C), C), :] = _hgrn_chunk(
                q_ref[0, rf, :], i_ref[0, rf, :], f_ref[0, rf, :], lb_f, stf, lv_f, tri_f, reverse=False)
            ob_s[pl.ds(pl.multiple_of(base + (n_chunks - 1 - j) * C, C), C), :] = _hgrn_chunk(
                q_ref[0, rb, :], i_ref[0, rb, :], b_ref[0, rb, :], lb_b, stb, lv_b, tri_b, reverse=True)
            return carry
        lax.fori_loop(0, n_chunks, body, 0, unroll=2)

    run(qc, ic, fc, bc, n_ctx, 0)
    run(ql, il, fl, bl, n_lat, n_ctx * C)

    nw = nw_ref[...]

    def finish(g_ref, o_ref, n_chunks, base):
        def body(j, carry):
            r = pl.ds(pl.multiple_of(j * C, C), C)
            rs = pl.ds(pl.multiple_of(base + j * C, C), C)
            o = of_s[rs, :] + ob_s[rs, :]
            o = o * lax.rsqrt(jnp.mean(o * o, axis=-1, keepdims=True) + NORM_EPS) * nw
            g = g_ref[0, r, :]
            o_ref[0, r, :] = (o * (g * _sigmoid(g))).astype(o_ref.dtype)
            return carry
        lax.fori_loop(0, n_chunks, body, 0)

    finish(gc, oc_ref, n_ctx, 0)
    finish(gl, ol_ref, n_lat, n_ctx * C)


def _hgrn(hg_ctx, hg_lat, lb, norm_w):
    bsz, lc, n5 = hg_ctx.shape
    ll = hg_lat.shape[1]
    dg = n5 // 5
    nh = dg // HEAD_DIM
    C = HGRN_CHUNK
    assert lc % C == 0 and ll % C == 0

    def slab(L, part):
        return pl.BlockSpec((1, L, HEAD_DIM), lambda b, h: (b, 0, part * nh + h))

    return pl.pallas_call(
        functools.partial(_hgrn_kernel, n_ctx=lc // C, n_lat=ll // C),
        out_shape=(jax.ShapeDtypeStruct((bsz, lc, dg), BF16), jax.ShapeDtypeStruct((bsz, ll, dg), BF16)),
        grid=(bsz, nh),
        in_specs=[slab(lc, p) for p in range(5)] + [slab(ll, p) for p in range(5)] + [
            pl.BlockSpec((1, 2, HEAD_DIM), lambda b, h: (h, 0, 0)),
            pl.BlockSpec((1, HEAD_DIM), lambda b, h: (0, 0)),
        ],
        out_specs=(pl.BlockSpec((1, lc, HEAD_DIM), lambda b, h: (b, 0, h)),
                   pl.BlockSpec((1, ll, HEAD_DIM), lambda b, h: (b, 0, h))),
        scratch_shapes=[
            pltpu.VMEM((lc + ll, HEAD_DIM), F32),
            pltpu.VMEM((lc + ll, HEAD_DIM), F32),
            pltpu.VMEM((HEAD_DIM, HEAD_DIM), F32),
            pltpu.VMEM((HEAD_DIM, HEAD_DIM), F32),
        ],
        compiler_params=_cparams("parallel", "parallel"),
        name="hgrn2",
    )(*([hg_ctx] * 5), *([hg_lat] * 5), lb, norm_w.reshape(1, HEAD_DIM))


def _outproj_kernel(x_ref, hy_ref, hg_ref, mod_ref, w_ref, rw_ref, x1_ref, t_ref, lg_ref):
    c = hy_ref.shape[-1]
    mix = _dot(hy_ref[0], w_ref[:c, :]) + _dot(hg_ref[0], w_ref[c:, :])
    m = mod_ref[0]
    x1 = x_ref[0] + m[2:3] * mix
    x1_ref[0] = x1
    xn = x1 * lax.rsqrt(jnp.mean(x1 * x1, axis=-1, keepdims=True) + NORM_EPS)
    t = xn * (1.0 + m[4:5]) + m[3:4]
    t_hi = t.astype(BF16)
    t_ref[0] = t_hi
    t_lo = (t - t_hi.astype(F32)).astype(BF16)
    r = _dot(t_hi, rw_ref[...])
    lg = r[:, :LANES] + r[:, LANES:] + _dot(t_lo, rw_ref[:, :LANES])
    lg_ref[0] = lg[:, :lg_ref.shape[-1]]


def _outproj(x, hy, hg, mod_l, mod_row, w_out, router_w, *, tm):
    bsz, L, d = x.shape
    c = hy.shape[-1]
    cg = hg.shape[-1]
    ne = router_w.shape[1]
    rw_hi = router_w.astype(BF16)
    rw_lo = (router_w - rw_hi.astype(F32)).astype(BF16)
    rw = jnp.zeros((d, 2 * LANES), BF16).at[:, :ne].set(rw_hi).at[:, LANES:LANES + ne].set(rw_lo)
    return pl.pallas_call(
        _outproj_kernel,
        out_shape=(jax.ShapeDtypeStruct((bsz, L, d), F32), jax.ShapeDtypeStruct((bsz, L, d), BF16),
                   jax.ShapeDtypeStruct((bsz, L, ne), F32)),
        grid=(bsz, L // tm),
        in_specs=[
            pl.BlockSpec((1, tm, d), lambda b, i: (b, i, 0)),
            pl.BlockSpec((1, tm, c), lambda b, i: (b, i, 0)),
            pl.BlockSpec((1, tm, cg), lambda b, i: (b, i, 0)),
            pl.BlockSpec((1, N_MOD, d), lambda b, i: (mod_row(b), 0, 0)),
            _resident((c + cg, d)),
            _resident((d, 2 * LANES)),
        ],
        out_specs=(pl.BlockSpec((1, tm, d), lambda b, i: (b, i, 0)),
                   pl.BlockSpec((1, tm, d), lambda b, i: (b, i, 0)),
                   pl.BlockSpec((1, tm, ne), lambda b, i: (b, i, 0))),
        compiler_params=_cparams("parallel", "parallel"),
        name="outproj",
    )(x, hy, hg, mod_l, w_out, rw)


def _first_argmax(vals, lane, sentinel):
    m = jnp.max(vals, axis=-1, keepdims=True)
    idx = jnp.min(jnp.where(vals == m, lane, sentinel), axis=-1, keepdims=True)
    return m, idx


def _moe_gates(logits, rbias):
    tm, ne = logits.shape
    neg = -jnp.inf
    mx = jnp.max(logits, axis=-1, keepdims=True)
    ex = jnp.exp(logits - mx)
    scores = ex / jnp.sum(ex, axis=-1, keepdims=True)
    sel = scores + rbias
    lane = lax.broadcasted_iota(jnp.int32, (tm, ne), 1)
    grp = lane // EXPERTS_PER_GROUP
    best_s = jnp.full((tm, 1), neg, F32)
    best_g = jnp.zeros((tm, 1), jnp.int32)
    for gi in range(N_GROUPS):
        mg = jnp.where(grp == gi, sel, neg)
        m1, i1 = _first_argmax(mg, lane, ne)
        m2 = jnp.max(jnp.where(lane == i1, neg, mg), axis=-1, keepdims=True)
        gs = m1 + m2
        upd = gs > best_s
        best_g = jnp.where(upd, gi, best_g)
        best_s = jnp.where(upd, gs, best_s)
    sg = jnp.where(grp == best_g, sel, neg)
    _, i1 = _first_argmax(sg, lane, ne)
    _, i2 = _first_argmax(jnp.where(lane == i1, neg, sg), lane, ne)
    chosen = (lane == i1) | (lane == i2)
    w = jnp.where(chosen, scores, 0.0)
    return w / jnp.sum(w, axis=-1, keepdims=True)


def _moe_kernel(t_ref, lg_ref, rb_ref, wg_ref, wu_ref, wd_ref, x1_ref, mod_ref, o_ref, act_s):
    t = t_ref[0]
    gates = _moe_gates(lg_ref[0], rb_ref[...])
    ne, _, f = wg_ref.shape
    for e in range(ne):
        hgate = _dot(t, wg_ref[e])
        hup = _dot(t, wu_ref[e])
        act_s[:, e * f:(e + 1) * f] = (hgate * _sigmoid(hgate) * hup * gates[:, e:e + 1]).astype(BF16)
    o_ref[0] = x1_ref[0] + mod_ref[0][5:6] * _dot(act_s[...], wd_ref[...])


def _moe(t, logits, rbias, wg, wu, wd, x1, mod_l, mod_row, *, tm):
    bsz, L, d = x1.shape
    ne, _, f = wg.shape
    return pl.pallas_call(
        _moe_kernel,
        out_shape=jax.ShapeDtypeStruct((bsz, L, d), F32),
        grid=(bsz, L // tm),
        in_specs=[
            pl.BlockSpec((1, tm, d), lambda b, i: (b, i, 0)),
            pl.BlockSpec((1, tm, ne), lambda b, i: (b, i, 0)),
            _resident((1, ne)),
            _resident((ne, d, f)),
            _resident((ne, d, f)),
            _resident((ne * f, d)),
            pl.BlockSpec((1, tm, d), lambda b, i: (b, i, 0)),
            pl.BlockSpec((1, N_MOD, d), lambda b, i: (mod_row(b), 0, 0)),
        ],
        out_specs=pl.BlockSpec((1, tm, d), lambda b, i: (b, i, 0)),
        scratch_shapes=[pltpu.VMEM((tm, ne * f), BF16)],
        compiler_params=_cparams("parallel", "parallel"),
        name="moe",
    )(t, logits, rbias.reshape(1, ne), wg, wu, wd.reshape(ne * f, d), x1, mod_l)


def _final_norm_kernel(x_ref, w_ref, o_ref):
    x = x_ref[0]
    o_ref[0] = x * lax.rsqrt(jnp.mean(x * x, axis=-1, keepdims=True) + NORM_EPS) * w_ref[...]


def _final_norm(x, w, *, tm):
    bsz, L, d = x.shape
    return pl.pallas_call(
        _final_norm_kernel,
        out_shape=jax.ShapeDtypeStruct(x.shape, F32),
        grid=(bsz, L // tm),
        in_specs=[pl.BlockSpec((1, tm, d), lambda b, i: (b, i, 0)), pl.BlockSpec((1, d), lambda b, i: (0, 0))],
        out_specs=pl.BlockSpec((1, tm, d), lambda b, i: (b, i, 0)),
        compiler_params=_cparams("parallel", "parallel"),
        name="final_norm",
    )(x, w.reshape(1, d))


def kernel(x, c, ctx, c_ctx, w_mod, b_mod, w_in, w_out, hy_conv_w, hy_conv_b, hy_w1, hy_b1, hy_w2, hy_b2, hy_w3, hy_b3, hy_bias, hgrn_lower_bounds, hgrn_norm_w, router_w, router_bias, moe_w_gate, moe_w_up, moe_w_down, final_norm_w):
    bsz, seq, d = x.shape
    ctx_len = ctx.shape[1]
    depth = w_mod.shape[0]
    dg = hgrn_lower_bounds.shape[-1]
    nh = dg // HEAD_DIM
    assert bsz + 1 <= MOD_ROWS and seq % GRID_W == 0

    cc = jnp.zeros((MOD_ROWS, d), F32).at[:bsz].set(c).at[bsz].set(c_ctx)
    mod = _modulation(cc, w_mod, b_mod).reshape(depth, MOD_ROWS, N_MOD, d)
    lat_row = lambda b: b
    ctx_row = lambda b: bsz

    lb_soft = jax.nn.softmax(hgrn_lower_bounds.astype(F32), axis=1)
    lower = jnp.cumsum(lb_soft, axis=1) - lb_soft[:, :1]
    lower = lower.reshape(2, depth, nh, HEAD_DIM).transpose(1, 2, 0, 3)

    dft_lat = _dft_matrices(seq)
    dft_ctx = _dft_matrices(ctx_len)
    tm_lat = min(seq, 512)
    tm_moe = min(seq, 512)

    w_in_b = w_in.astype(BF16)
    w_out_b = w_out.astype(BF16)
    wg_b = moe_w_gate.astype(BF16)
    wu_b = moe_w_up.astype(BF16)
    wd_b = moe_w_down.astype(BF16)

    xc = ctx
    for l in range(depth):
        last = l == depth - 1
        u_lat, hg_lat = _inproj(x, mod[l], lat_row, w_in_b[l], hy_conv_w[l], hy_conv_b[l],
                                period=GRID_W, tm=tm_lat)
        u_ctx, hg_ctx = _inproj(xc, mod[l], ctx_row, w_in_b[l], hy_conv_w[l], hy_conv_b[l],
                                period=ctx_len, tm=ctx_len)
        o_ctx, o_lat = _hgrn(hg_ctx, hg_lat, lower[l], hgrn_norm_w[l])
        taps_w = (hy_w1[l], hy_b1[l], hy_w2[l], hy_b2[l], hy_w3[l], hy_b3[l])
        hy_lat = _hyena(u_lat, dft_lat, taps_w, hy_bias[l])
        x1, t, lg = _outproj(x, hy_lat, o_lat, mod[l], lat_row, w_out_b[l], router_w, tm=tm_lat)
        x = _moe(t, lg, router_bias, wg_b[l], wu_b[l], wd_b[l], x1, mod[l], lat_row, tm=tm_moe)
        if not last:
            hy_ctx = _hyena(u_ctx, dft_ctx, taps_w, hy_bias[l])
            x1, t, lg = _outproj(xc, hy_ctx, o_ctx, mod[l], ctx_row, w_out_b[l], router_w, tm=ctx_len)
            xc = _moe(t, lg, router_bias, wg_b[l], wu_b[l], wd_b[l], x1, mod[l], ctx_row, tm=ctx_len)

    return _final_norm(x, final_norm_w, tm=tm_lat)
```

```python
import functools
import math

import numpy as np
import jax
import jax.numpy as jnp
from jax import lax
from jax.experimental import pallas as pl
from jax.experimental.pallas import tpu as pltpu

F32 = jnp.float32
BF16 = jnp.bfloat16
HIGHEST = lax.Precision.HIGHEST

GRID_W = 64
NORM_EPS = 1e-6
N_MOD = 6
HY_ORDER = 2
HY_BANDS = 16
HY_TARGET = 1e-2
HY_FAST_PCT = 0.3
HY_SLOW_PCT = 1.5
HEAD_DIM = 128
N_EXPERTS = 16
N_GROUPS = 4
EXPERTS_PER_GROUP = N_EXPERTS // N_GROUPS
LANES = 128
MOD_ROWS = 16
HGRN_CHUNK = 128
HGRN_GROUP = 4
HGRN_FINE_LEVELS = (2, 4)
VMEM_LIMIT = 56 << 20


def _cparams(*sem, flags=None):
    return pltpu.CompilerParams(dimension_semantics=sem, vmem_limit_bytes=VMEM_LIMIT, flags=flags)


def _sigmoid(x):
    return 1.0 / (1.0 + jnp.exp(-x))


def _dot(a, b, **kw):
    return jnp.dot(a, b, preferred_element_type=F32, **kw)


def _resident(shape):
    return pl.BlockSpec(shape, lambda *_: (0,) * len(shape), pipeline_mode=pl.Buffered(1))


def _neg_abs(x):
    bits = lax.bitcast_convert_type(x, jnp.uint32) | jnp.uint32(0x80000000)
    return lax.bitcast_convert_type(bits, F32)


def _dot_nt(a, b):
    return lax.dot_general(a, b, (((1,), (1,)), ((), ())), preferred_element_type=F32)


def _mod_kernel(c_ref, w_ref, b_ref, o_ref):
    c = c_ref[...]
    cs = c * _sigmoid(c)
    o_ref[0] = _dot(cs, w_ref[0], precision=HIGHEST) + b_ref[0]


def _modulation(cc, w_mod, b_mod):
    depth, d, n = w_mod.shape
    tn = n // 3
    return pl.pallas_call(
        _mod_kernel,
        out_shape=jax.ShapeDtypeStruct((depth, MOD_ROWS, n), F32),
        grid=(depth, n // tn),
        in_specs=[
            pl.BlockSpec((MOD_ROWS, d), lambda l, j: (0, 0)),
            pl.BlockSpec((1, d, tn), lambda l, j: (l, 0, j)),
            pl.BlockSpec((1, 1, tn), lambda l, j: (l, 0, j)),
        ],
        out_specs=pl.BlockSpec((1, MOD_ROWS, tn), lambda l, j: (l, 0, j)),
        compiler_params=_cparams("parallel", "parallel"),
        name="modulation",
    )(cc, w_mod, b_mod.reshape(depth, 1, n))


def _dft_kernel(ac_ref, as_ref, bc_ref, bs_ref, c_ref, s_ref, *, n1):
    bc = bc_ref[...]
    bs = bs_ref[...]
    ac = ac_ref[...]
    as_ = as_ref[...]
    for j in range(n1):
        a_c = ac[:, j:j + 1]
        a_s = as_[:, j:j + 1]
        c_ref[:, j * LANES:(j + 1) * LANES] = (a_c * bc - a_s * bs).astype(BF16)
        s_ref[:, j * LANES:(j + 1) * LANES] = (a_s * bc + a_c * bs).astype(BF16)


def _dft_tables(L):
    n1 = L // LANES
    period = 4 * L
    r = np.arange(L, dtype=np.int64)[:, None]
    c1 = np.arange(n1, dtype=np.int64)[None, :]
    c0 = np.arange(LANES, dtype=np.int64)[None, :]

    def cs(phase):
        ang = (phase % period).astype(np.float64) * (2.0 * np.pi / period)
        return np.cos(ang).astype(np.float32), np.sin(ang).astype(np.float32)

    fa = cs((2 * r + 1) * (LANES * c1))
    fb = cs((2 * r + 1) * c0)
    ta = cs((2 * LANES * c1) * r)
    tb = cs((2 * c0 + 1) * r)
    return (fa, fb), (ta, tb)


def _dft_matrices(L):
    n1 = L // LANES
    tr = min(L, 256)
    outs = []
    for (ac, as_), (bc, bs) in _dft_tables(L):
        c, s = pl.pallas_call(
            functools.partial(_dft_kernel, n1=n1),
            out_shape=(jax.ShapeDtypeStruct((L, L), BF16), jax.ShapeDtypeStruct((L, L), BF16)),
            grid=(L // tr,),
            in_specs=[
                pl.BlockSpec((tr, n1), lambda i: (i, 0)),
                pl.BlockSpec((tr, n1), lambda i: (i, 0)),
                pl.BlockSpec((tr, LANES), lambda i: (i, 0)),
                pl.BlockSpec((tr, LANES), lambda i: (i, 0)),
            ],
            out_specs=(pl.BlockSpec((tr, L), lambda i: (i, 0)), pl.BlockSpec((tr, L), lambda i: (i, 0))),
            compiler_params=_cparams("parallel"),
            name="dft_tables",
        )(jnp.asarray(ac), jnp.asarray(as_), jnp.asarray(bc), jnp.asarray(bs))
        outs.append((c, s))
    return outs


def _inproj_kernel(x_ref, mod_ref, w_ref, cw_ref, cb_ref, u_ref, hg_ref, *, period, n_hy, tn):
    x = x_ref[0]
    tm = x.shape[0]
    xn = x * lax.rsqrt(jnp.mean(x * x, axis=-1, keepdims=True) + NORM_EPS)
    m = mod_ref[0]
    a = (xn * (1.0 + m[1:2]) + m[0:1]).astype(BF16)
    pos = lax.broadcasted_iota(jnp.int32, (tm, 1), 0) % period
    first = pos == 0
    last = pos == period - 1
    n_total = w_ref.shape[1]
    for j in range(n_total // tn):
        p = _dot(a, w_ref[:, j * tn:(j + 1) * tn])
        if j * tn < n_hy:
            cw = cw_ref[:, j * tn:(j + 1) * tn]
            prev = jnp.where(first, 0.0, pltpu.roll(p, 1, 0))
            nxt = jnp.where(last, 0.0, pltpu.roll(p, tm - 1, 0))
            u_ref[0, :, j * tn:(j + 1) * tn] = (
                prev * cw[0:1] + p * cw[1:2] + nxt * cw[2:3] + cb_ref[:, j * tn:(j + 1) * tn]
            ).astype(u_ref.dtype)
        else:
            hg_ref[0, :, j * tn - n_hy:(j + 1) * tn - n_hy] = p


def _inproj(x, mod_l, mod_row, w_in, conv_w, conv_b, *, period, tm):
    bsz, L, d = x.shape
    n_hy = conv_w.shape[1]
    n_all = w_in.shape[1]
    tn = n_hy // 3
    assert tm % period == 0 or period == L == tm
    return pl.pallas_call(
        functools.partial(_inproj_kernel, period=period, n_hy=n_hy, tn=tn),
        out_shape=(jax.ShapeDtypeStruct((bsz, L, n_hy), BF16),
                   jax.ShapeDtypeStruct((bsz, L, n_all - n_hy), F32)),
        grid=(bsz, L // tm),
        in_specs=[
            pl.BlockSpec((1, tm, d), lambda b, i: (b, i, 0)),
            pl.BlockSpec((1, N_MOD, d), lambda b, i: (mod_row(b), 0, 0)),
            _resident((d, n_all)),
            _resident((3, n_hy)),
            _resident((1, n_hy)),
        ],
        out_specs=(pl.BlockSpec((1, tm, n_hy), lambda b, i: (b, i, 0)),
                   pl.BlockSpec((1, tm, n_all - n_hy), lambda b, i: (b, i, 0))),
        compiler_params=_cparams("parallel", "parallel"),
        name="inproj",
    )(x, mod_l, w_in, conv_w, conv_b.reshape(1, n_hy))


def _filter_kernel(z_ref, w1_ref, b1_ref, w2_ref, b2_ref, w3f_ref, b3f_ref, w3b_ref, b3b_ref,
                   dl_ref, o_ref):
    z = z_ref[...]
    h = jnp.sin(_dot(z, w1_ref[...], precision=HIGHEST) + b1_ref[...])
    h = jnp.sin(_dot(h, w2_ref[...], precision=HIGHEST) + b2_ref[...])
    hf = _dot(h, w3f_ref[...], precision=HIGHEST) + b3f_ref[...]
    hb = _dot(h, w3b_ref[...], precision=HIGHEST) + b3b_ref[...]
    win = jnp.exp(-z[:, 0:1] * dl_ref[...])
    hf = hf * win
    hb = hb * win
    nrm = (jnp.sum(jnp.abs(hf), axis=0, keepdims=True)
           + jnp.sum(jnp.abs(hb), axis=0, keepdims=True))
    inv = 1.0 / nrm
    hf = hf * inv
    row = lax.broadcasted_iota(jnp.int32, (z.shape[0], 1), 0)
    hb0 = jnp.where(row == 0, 0.0, hb * inv)
    o_ref[0, 0] = hf + hb0
    o_ref[0, 1] = hf - hb0


def _hyena_filter_taps(L, w1, b1, w2, b2, w3, b3):
    nfeat, hid = w1.shape
    c = w3.shape[1] // (2 * HY_ORDER)
    tc = min(c, 256)
    nct = c // tc
    t = jnp.linspace(0.0, 1.0, L, dtype=F32)
    n = jnp.arange(L, dtype=F32)
    freqs = jnp.linspace(1e-4, HY_BANDS - 1, HY_BANDS, dtype=F32)
    ang = (2.0 * math.pi / L) * n[:, None] * freqs[None, :]
    z = jnp.concatenate([t[:, None], jnp.cos(ang), -jnp.sin(ang)], axis=-1)
    z = jnp.pad(z, ((0, 0), (0, LANES - nfeat)))
    w1p = jnp.pad(w1, ((0, LANES - nfeat), (0, 0)))
    deltas = jnp.abs(jnp.linspace(math.log(HY_TARGET) / HY_FAST_PCT, math.log(HY_TARGET) / HY_SLOW_PCT,
                                  c, dtype=F32)).reshape(1, c)
    full = lambda shape: pl.BlockSpec(shape, lambda o, j: (0,) * len(shape))
    return pl.pallas_call(
        _filter_kernel,
        out_shape=jax.ShapeDtypeStruct((HY_ORDER, 2, L, c), F32),
        grid=(HY_ORDER, nct),
        in_specs=[
            full((L, LANES)), full((LANES, hid)), full((1, hid)), full((hid, hid)), full((1, hid)),
            pl.BlockSpec((hid, tc), lambda o, j: (0, o * 2 * nct + j)),
            pl.BlockSpec((1, tc), lambda o, j: (0, o * 2 * nct + j)),
            pl.BlockSpec((hid, tc), lambda o, j: (0, o * 2 * nct + nct + j)),
            pl.BlockSpec((1, tc), lambda o, j: (0, o * 2 * nct + nct + j)),
            pl.BlockSpec((1, tc), lambda o, j: (0, j)),
        ],
        out_specs=pl.BlockSpec((1, 2, L, tc), lambda o, j: (o, 0, 0, j)),
        compiler_params=_cparams("parallel", "parallel"),
        name="hyena_filter",
    )(z, w1p, b1.reshape(1, hid), w2, b2.reshape(1, hid), w3, b3.reshape(1, -1), w3, b3.reshape(1, -1),
      deltas)


def _kspec_kernel(cf_ref, sf_ref, h_ref, o_ref, *, scale):
    o_ref[0, 0] = _dot(cf_ref[...], h_ref[0, 0].astype(BF16)) * scale
    o_ref[0, 1] = _dot(sf_ref[...], h_ref[0, 1].astype(BF16)) * scale


def _filter_spectrum(cft, sft, taps):
    _, _, L, c = taps.shape
    tk = min(L, 512)
    return pl.pallas_call(
        functools.partial(_kspec_kernel, scale=1.0 / L),
        out_shape=jax.ShapeDtypeStruct((HY_ORDER, 2, L, c), F32),
        grid=(HY_ORDER, L // tk),
        in_specs=[
            pl.BlockSpec((tk, L), lambda o, i: (i, 0)),
            pl.BlockSpec((tk, L), lambda o, i: (i, 0)),
            pl.BlockSpec((1, 2, L, c), lambda o, i: (o, 0, 0, 0)),
        ],
        out_specs=pl.BlockSpec((1, 2, tk, c), lambda o, i: (o, 0, i, 0)),
        compiler_params=_cparams("parallel", "parallel"),
        name="hyena_filter_spectrum",
    )(cft, sft, taps)


def _hy_fwd_kernel(cf_ref, sf_ref, v_ref, k_ref, p_ref, q_ref):
    v = v_ref[0].astype(BF16)
    a = _dot(cf_ref[...], v)
    b = _dot(sf_ref[...], v)
    kr = k_ref[0, 0]
    ks = k_ref[0, 1]
    p_ref[0] = (a * kr - b * ks).astype(BF16)
    q_ref[0] = (a * ks + b * kr).astype(BF16)


def _hy_forward(cft, sft, src, src_col, kspec, order):
    bsz, L, _ = src.shape
    c = kspec.shape[-1]
    tk = min(L, 1024)
    return pl.pallas_call(
        _hy_fwd_kernel,
        out_shape=(jax.ShapeDtypeStruct((bsz, L, c), BF16), jax.ShapeDtypeStruct((bsz, L, c), BF16)),
        grid=(L // tk, bsz),
        in_specs=[
            pl.BlockSpec((tk, L), lambda i, b: (i, 0)),
            pl.BlockSpec((tk, L), lambda i, b: (i, 0)),
            pl.BlockSpec((1, L, c), lambda i, b: (b, 0, src_col)),
            pl.BlockSpec((1, 2, tk, c), lambda i, b: (order, 0, i, 0)),
        ],
        out_specs=(pl.BlockSpec((1, tk, c), lambda i, b: (b, i, 0)),
                   pl.BlockSpec((1, tk, c), lambda i, b: (b, i, 0))),
        compiler_params=_cparams("parallel", "parallel"),
        name="hyena_spectrum",
    )(cft, sft, src, kspec)


def _hy_inv_kernel(cf_ref, sf_ref, p_ref, q_ref, gate_ref, src_ref, d_ref, o_ref):
    conv = _dot(cf_ref[...], p_ref[0]) + _dot(sf_ref[...], q_ref[0])
    o_ref[0] = (gate_ref[0] * (conv + src_ref[0] * d_ref[0])).astype(o_ref.dtype)


def _hy_inverse(cf, sf, p, q, gate, gate_col, src, src_col, d, order, out_dtype):
    bsz, L, c = p.shape
    tm = min(L, 1024)
    return pl.pallas_call(
        _hy_inv_kernel,
        out_shape=jax.ShapeDtypeStruct((bsz, L, c), out_dtype),
        grid=(L // tm, bsz),
        in_specs=[
            pl.BlockSpec((tm, L), lambda i, b: (i, 0)),
            pl.BlockSpec((tm, L), lambda i, b: (i, 0)),
            pl.BlockSpec((1, L, c), lambda i, b: (b, 0, 0)),
            pl.BlockSpec((1, L, c), lambda i, b: (b, 0, 0)),
            pl.BlockSpec((1, tm, c), lambda i, b: (b, i, gate_col)),
            pl.BlockSpec((1, tm, c), lambda i, b: (b, i, src_col)),
            pl.BlockSpec((1, 1, c), lambda i, b: (order, 0, 0)),
        ],
        out_specs=pl.BlockSpec((1, tm, c), lambda i, b: (b, i, 0)),
        compiler_params=_cparams("parallel", "parallel"),
        name="hyena_inverse",
    )(cf, sf, p, q, gate, src, d.reshape(HY_ORDER, 1, c))


def _hyena(u, dft, taps_w, d):
    (cft, sft), (cf, sf) = dft
    L = u.shape[1]
    taps = _hyena_filter_taps(L, *taps_w)
    kspec = _filter_spectrum(cft, sft, taps)
    p, q = _hy_forward(cft, sft, u, 0, kspec, 0)
    z = _hy_inverse(cf, sf, p, q, u, 1, u, 0, d, 0, BF16)
    p, q = _hy_forward(cft, sft, z, 0, kspec, 1)
    return _hy_inverse(cf, sf, p, q, u, 2, z, 0, d, 1, BF16)


def _anchor_rows(b, n, a):
    assert n % 8 == 0
    parts = [jnp.broadcast_to(b[s + a:s + a + 1, :], (n, b.shape[1])) for s in range(0, b.shape[0], n)]
    return parts[0] if len(parts) == 1 else jnp.concatenate(parts, axis=0)


def _hgrn_group(q_raw, v, f_logit, lb, st_ref, lv, tri, *, reverse):
    C = HGRN_CHUNK
    G = q_raw.shape[0] // C
    rows = lambda x, i: x[i * C:(i + 1) * C]
    q = q_raw * _sigmoid(q_raw)
    f = lb + (1.0 - lb) * _sigmoid(f_logit)
    kk = 1.0 - f
    g = jnp.log2(f)
    g_top = lax.bitcast_convert_type(
        lax.bitcast_convert_type(g, jnp.uint32) & jnp.uint32(0xFFFF0000), F32)
    gg = jnp.concatenate([g_top.astype(BF16), (g - g_top).astype(BF16)], axis=1)
    bb = [_dot(tri, rows(gg, i)) for i in range(G)]
    part = lambda k: jnp.concatenate(
        [x[k * C:(k + 1) * C, :HEAD_DIM] + x[k * C:(k + 1) * C, HEAD_DIM:] for x in bb], axis=0)
    b = part(0)
    fine = {n_: part(k + 1) for k, n_ in enumerate(HGRN_FINE_LEVELS)}

    qb = q.astype(BF16)
    kb = kk.astype(BF16)
    scores = [jnp.zeros((C, C), F32)] * G
    n = 2
    level = 1
    while n <= C:
        a = n // 2 if reverse else n // 2 - 1
        d = fine[n] if n in fine else b - _anchor_rows(b, n, a)
        e = jnp.exp2(_neg_abs(d)).astype(BF16)
        qe = qb * e
        ke = kb * e
        scores = [jnp.where(lv == level, _dot_nt(rows(qe, i), rows(ke, i)), scores[i]) for i in range(G)]
        n *= 2
        level += 1

    vb = v.astype(BF16)
    diag = jnp.sum(q * kk, axis=-1, keepdims=True) * v
    o = [_dot(scores[i].astype(BF16), rows(vb, i)) + rows(diag, i) for i in range(G)]
    b_end = _anchor_rows(b, C, 0 if reverse else C - 1)
    qd = qb * jnp.exp2(b).astype(BF16)
    kd = kb * jnp.exp2(b_end - b).astype(BF16)
    dec = jnp.exp2(b_end)
    kv = [_dot(rows(v, i).T.astype(BF16), rows(kd, i)) for i in range(G)]
    st = st_ref[...]
    for i in (range(G - 1, -1, -1) if reverse else range(G)):
        o[i] = o[i] + _dot_nt(rows(qd, i), st.astype(BF16))
        st = st * dec[i * C:i * C + 1] + kv[i]
    st_ref[...] = st
    return jnp.concatenate(o, axis=0)


def _hgrn_kernel(qc, ic, gc, fc, bc, ql, il, gl, fl, bl, lb_ref, nw_ref, oc_ref, ol_ref,
                 of_s, ob_s, stf, stb, *, n_ctx, n_lat):
    C = HGRN_CHUNK
    row = lax.broadcasted_iota(jnp.int32, (C, C), 0)
    col = lax.broadcasted_iota(jnp.int32, (C, C), 1)
    x = row ^ col
    lvl = jnp.zeros((C, C), jnp.int32)
    n = 1
    while n < C:
        lvl = lvl + (x >= n).astype(jnp.int32)
        n *= 2
    lv_f = jnp.where(row > col, lvl, 0)
    lv_b = jnp.where(row < col, lvl, 0)
    def cum_matrix(reverse):
        cum = lambda r: (col >= r) if reverse else (col <= r)
        mats = [cum(row).astype(F32)]
        for n_ in HGRN_FINE_LEVELS:
            anchor = row - row % n_ + (n_ // 2 if reverse else n_ // 2 - 1)
            mats.append(mats[0] - cum(anchor).astype(F32))
        return jnp.concatenate(mats, axis=0).astype(BF16)

    tri_f = cum_matrix(False)
    tri_b = cum_matrix(True)
    lb_f = lb_ref[0, 0:1, :]
    lb_b = lb_ref[0, 1:2, :]
    stf[...] = jnp.zeros_like(stf)
    stb[...] = jnp.zeros_like(stb)

    def run(q_ref, i_ref, f_ref, b_ref, n_chunks, base):
        G = min(n_chunks, HGRN_GROUP)
        R = G * C
        n_groups = n_chunks // G

        def body(j, carry):
            off_f = pl.multiple_of(j * R, R)
            off_b = pl.multiple_of((n_groups - 1 - j) * R, R)
            rf = pl.ds(off_f, R)
            rb = pl.ds(off_b, R)
            of_s[pl.ds(base + off_f, R), :] = _hgrn_group(
                q_ref[0, rf, :], i_ref[0, rf, :], f_ref[0, rf, :], lb_f, stf, lv_f, tri_f, reverse=False)
            ob_s[pl.ds(base + off_b, R), :] = _hgrn_group(
                q_ref[0, rb, :], i_ref[0, rb, :], b_ref[0, rb, :], lb_b, stb, lv_b, tri_b, reverse=True)
            return carry
        lax.fori_loop(0, n_groups, body, 0)

    run(qc, ic, fc, bc, n_ctx, 0)
    run(ql, il, fl, bl, n_lat, n_ctx * C)

    nw = nw_ref[...]

    def finish(g_ref, o_ref, n_chunks, base):
        def body(j, carry):
            r = pl.ds(pl.multiple_of(j * C, C), C)
            rs = pl.ds(pl.multiple_of(base + j * C, C), C)
            o = of_s[rs, :] + ob_s[rs, :]
            o = o * lax.rsqrt(jnp.mean(o * o, axis=-1, keepdims=True) + NORM_EPS) * nw
            g = g_ref[0, r, :]
            o_ref[0, r, :] = (o * (g * _sigmoid(g))).astype(o_ref.dtype)
            return carry
        lax.fori_loop(0, n_chunks, body, 0)

    finish(gc, oc_ref, n_ctx, 0)
    finish(gl, ol_ref, n_lat, n_ctx * C)


def _hgrn(hg_ctx, hg_lat, lb, norm_w):
    bsz, lc, n5 = hg_ctx.shape
    ll = hg_lat.shape[1]
    dg = n5 // 5
    nh = dg // HEAD_DIM
    C = HGRN_CHUNK
    assert lc % C == 0 and ll % C == 0
    assert all((n // C) % min(n // C, HGRN_GROUP) == 0 for n in (lc, ll))

    def slab(L, part):
        return pl.BlockSpec((1, L, HEAD_DIM), lambda b, h: (b, 0, part * nh + h))

    return pl.pallas_call(
        functools.partial(_hgrn_kernel, n_ctx=lc // C, n_lat=ll // C),
        out_shape=(jax.ShapeDtypeStruct((bsz, lc, dg), BF16), jax.ShapeDtypeStruct((bsz, ll, dg), BF16)),
        grid=(bsz, nh),
        in_specs=[slab(lc, p) for p in range(5)] + [slab(ll, p) for p in range(5)] + [
            pl.BlockSpec((1, 2, HEAD_DIM), lambda b, h: (h, 0, 0)),
            pl.BlockSpec((1, HEAD_DIM), lambda b, h: (0, 0)),
        ],
        out_specs=(pl.BlockSpec((1, lc, HEAD_DIM), lambda b, h: (b, 0, h)),
                   pl.BlockSpec((1, ll, HEAD_DIM), lambda b, h: (b, 0, h))),
        scratch_shapes=[
            pltpu.VMEM((lc + ll, HEAD_DIM), F32),
            pltpu.VMEM((lc + ll, HEAD_DIM), F32),
            pltpu.VMEM((HEAD_DIM, HEAD_DIM), F32),
            pltpu.VMEM((HEAD_DIM, HEAD_DIM), F32),
        ],
        compiler_params=_cparams("parallel", "parallel"),
        name="hgrn2",
    )(*([hg_ctx] * 5), *([hg_lat] * 5), lb, norm_w.reshape(1, HEAD_DIM))


def _outproj_kernel(x_ref, hy_ref, hg_ref, mod_ref, w_ref, rw_ref, x1_ref, t_ref, lg_ref):
    c = hy_ref.shape[-1]
    mix = _dot(hy_ref[0], w_ref[:c, :]) + _dot(hg_ref[0], w_ref[c:, :])
    m = mod_ref[0]
    x1 = x_ref[0] + m[2:3] * mix
    x1_ref[0] = x1
    xn = x1 * lax.rsqrt(jnp.mean(x1 * x1, axis=-1, keepdims=True) + NORM_EPS)
    t = xn * (1.0 + m[4:5]) + m[3:4]
    t_hi = t.astype(BF16)
    t_ref[0] = t_hi
    t_lo = (t - t_hi.astype(F32)).astype(BF16)
    r = _dot(t_hi, rw_ref[...])
    lg = r[:, :LANES] + r[:, LANES:] + _dot(t_lo, rw_ref[:, :LANES])
    lg_ref[0] = lg[:, :lg_ref.shape[-1]]


def _outproj(x, hy, hg, mod_l, mod_row, w_out, router_w, *, tm):
    bsz, L, d = x.shape
    c = hy.shape[-1]
    cg = hg.shape[-1]
    ne = router_w.shape[1]
    rw_hi = router_w.astype(BF16)
    rw_lo = (router_w - rw_hi.astype(F32)).astype(BF16)
    rw = jnp.zeros((d, 2 * LANES), BF16).at[:, :ne].set(rw_hi).at[:, LANES:LANES + ne].set(rw_lo)
    return pl.pallas_call(
        _outproj_kernel,
        out_shape=(jax.ShapeDtypeStruct((bsz, L, d), F32), jax.ShapeDtypeStruct((bsz, L, d), BF16),
                   jax.ShapeDtypeStruct((bsz, L, ne), F32)),
        grid=(bsz, L // tm),
        in_specs=[
            pl.BlockSpec((1, tm, d), lambda b, i: (b, i, 0)),
            pl.BlockSpec((1, tm, c), lambda b, i: (b, i, 0)),
            pl.BlockSpec((1, tm, cg), lambda b, i: (b, i, 0)),
            pl.BlockSpec((1, N_MOD, d), lambda b, i: (mod_row(b), 0, 0)),
            _resident((c + cg, d)),
            _resident((d, 2 * LANES)),
        ],
        out_specs=(pl.BlockSpec((1, tm, d), lambda b, i: (b, i, 0)),
                   pl.BlockSpec((1, tm, d), lambda b, i: (b, i, 0)),
                   pl.BlockSpec((1, tm, ne), lambda b, i: (b, i, 0))),
        compiler_params=_cparams("parallel", "parallel"),
        name="outproj",
    )(x, hy, hg, mod_l, w_out, rw)


def _first_argmax(vals, lane, sentinel):
    m = jnp.max(vals, axis=-1, keepdims=True)
    idx = jnp.min(jnp.where(vals == m, lane, sentinel), axis=-1, keepdims=True)
    return m, idx


def _moe_gates(logits, rbias):
    tm, ne = logits.shape
    neg = -jnp.inf
    mx = jnp.max(logits, axis=-1, keepdims=True)
    ex = jnp.exp(logits - mx)
    scores = ex / jnp.sum(ex, axis=-1, keepdims=True)
    sel = scores + rbias
    lane = lax.broadcasted_iota(jnp.int32, (tm, ne), 1)
    grp = lane // EXPERTS_PER_GROUP
    best_s = jnp.full((tm, 1), neg, F32)
    best_g = jnp.zeros((tm, 1), jnp.int32)
    for gi in range(N_GROUPS):
        mg = jnp.where(grp == gi, sel, neg)
        m1, i1 = _first_argmax(mg, lane, ne)
        m2 = jnp.max(jnp.where(lane == i1, neg, mg), axis=-1, keepdims=True)
        gs = m1 + m2
        upd = gs > best_s
        best_g = jnp.where(upd, gi, best_g)
        best_s = jnp.where(upd, gs, best_s)
    sg = jnp.where(grp == best_g, sel, neg)
    _, i1 = _first_argmax(sg, lane, ne)
    _, i2 = _first_argmax(jnp.where(lane == i1, neg, sg), lane, ne)
    chosen = (lane == i1) | (lane == i2)
    w = jnp.where(chosen, scores, 0.0)
    return w / jnp.sum(w, axis=-1, keepdims=True)


def _moe_kernel(t_ref, lg_ref, rb_ref, wg_ref, wu_ref, wd_ref, x1_ref, mod_ref, o_ref, act_s):
    t = t_ref[0]
    gates = _moe_gates(lg_ref[0], rb_ref[...])
    ne, _, f = wg_ref.shape
    for e in range(ne):
        hgate = _dot(t, wg_ref[e])
        hup = _dot(t, wu_ref[e])
        act_s[:, e * f:(e + 1) * f] = (hgate * _sigmoid(hgate) * hup * gates[:, e:e + 1]).astype(BF16)
    o_ref[0] = x1_ref[0] + mod_ref[0][5:6] * _dot(act_s[...], wd_ref[...])


def _moe(t, logits, rbias, wg, wu, wd, x1, mod_l, mod_row, *, tm):
    bsz, L, d = x1.shape
    ne, _, f = wg.shape
    return pl.pallas_call(
        _moe_kernel,
        out_shape=jax.ShapeDtypeStruct((bsz, L, d), F32),
        grid=(bsz, L // tm),
        in_specs=[
            pl.BlockSpec((1, tm, d), lambda b, i: (b, i, 0)),
            pl.BlockSpec((1, tm, ne), lambda b, i: (b, i, 0)),
            _resident((1, ne)),
            _resident((ne, d, f)),
            _resident((ne, d, f)),
            _resident((ne * f, d)),
            pl.BlockSpec((1, tm, d), lambda b, i: (b, i, 0)),
            pl.BlockSpec((1, N_MOD, d), lambda b, i: (mod_row(b), 0, 0)),
        ],
        out_specs=pl.BlockSpec((1, tm, d), lambda b, i: (b, i, 0)),
        scratch_shapes=[pltpu.VMEM((tm, ne * f), BF16)],
        compiler_params=_cparams("parallel", "parallel"),
        name="moe",
    )(t, logits, rbias.reshape(1, ne), wg, wu, wd.reshape(ne * f, d), x1, mod_l)


def _final_norm_kernel(x_ref, w_ref, o_ref):
    x = x_ref[0]
    o_ref[0] = x * lax.rsqrt(jnp.mean(x * x, axis=-1, keepdims=True) + NORM_EPS) * w_ref[...]


def _final_norm(x, w, *, tm):
    bsz, L, d = x.shape
    return pl.pallas_call(
        _final_norm_kernel,
        out_shape=jax.ShapeDtypeStruct(x.shape, F32),
        grid=(bsz, L // tm),
        in_specs=[pl.BlockSpec((1, tm, d), lambda b, i: (b, i, 0)), pl.BlockSpec((1, d), lambda b, i: (0, 0))],
        out_specs=pl.BlockSpec((1, tm, d), lambda b, i: (b, i, 0)),
        compiler_params=_cparams("parallel", "parallel"),
        name="final_norm",
    )(x, w.reshape(1, d))


def kernel(x, c, ctx, c_ctx, w_mod, b_mod, w_in, w_out, hy_conv_w, hy_conv_b, hy_w1, hy_b1, hy_w2, hy_b2, hy_w3, hy_b3, hy_bias, hgrn_lower_bounds, hgrn_norm_w, router_w, router_bias, moe_w_gate, moe_w_up, moe_w_down, final_norm_w):
    bsz, seq, d = x.shape
    ctx_len = ctx.shape[1]
    depth = w_mod.shape[0]
    dg = hgrn_lower_bounds.shape[-1]
    nh = dg // HEAD_DIM
    assert bsz + 1 <= MOD_ROWS and seq % GRID_W == 0

    cc = jnp.zeros((MOD_ROWS, d), F32).at[:bsz].set(c).at[bsz].set(c_ctx)
    mod = _modulation(cc, w_mod, b_mod).reshape(depth, MOD_ROWS, N_MOD, d)
    lat_row = lambda b: b
    ctx_row = lambda b: bsz

    lb_soft = jax.nn.softmax(hgrn_lower_bounds.astype(F32), axis=1)
    lower = jnp.cumsum(lb_soft, axis=1) - lb_soft[:, :1]
    lower = lower.reshape(2, depth, nh, HEAD_DIM).transpose(1, 2, 0, 3)

    dft_lat = _dft_matrices(seq)
    dft_ctx = _dft_matrices(ctx_len)
    tm_lat = min(seq, 512)
    tm_moe = min(seq, 512)

    w_in_b = w_in.astype(BF16)
    w_out_b = w_out.astype(BF16)
    wg_b = moe_w_gate.astype(BF16)
    wu_b = moe_w_up.astype(BF16)
    wd_b = moe_w_down.astype(BF16)

    xc = ctx
    for l in range(depth):
        last = l == depth - 1
        u_lat, hg_lat = _inproj(x, mod[l], lat_row, w_in_b[l], hy_conv_w[l], hy_conv_b[l],
                                period=GRID_W, tm=tm_lat)
        u_ctx, hg_ctx = _inproj(xc, mod[l], ctx_row, w_in_b[l], hy_conv_w[l], hy_conv_b[l],
                                period=ctx_len, tm=ctx_len)
        o_ctx, o_lat = _hgrn(hg_ctx, hg_lat, lower[l], hgrn_norm_w[l])
        taps_w = (hy_w1[l], hy_b1[l], hy_w2[l], hy_b2[l], hy_w3[l], hy_b3[l])
        hy_lat = _hyena(u_lat, dft_lat, taps_w, hy_bias[l])
        x1, t, lg = _outproj(x, hy_lat, o_lat, mod[l], lat_row, w_out_b[l], router_w, tm=tm_lat)
        x = _moe(t, lg, router_bias, wg_b[l], wu_b[l], wd_b[l], x1, mod[l], lat_row, tm=tm_moe)
        if not last:
            hy_ctx = _hyena(u_ctx, dft_ctx, taps_w, hy_bias[l])
            x1, t, lg = _outproj(xc, hy_ctx, o_ctx, mod[l], ctx_row, w_out_b[l], router_w, tm=ctx_len)
            xc = _moe(t, lg, router_bias, wg_b[l], wu_b[l], wd_b[l], x1, mod[l], ctx_row, tm=ctx_len)

    return _final_norm(x, final_norm_w, tm=tm_lat)
```

```python
import functools
import math

import numpy as np
import jax
import jax.numpy as jnp
from jax import lax
from jax.experimental import pallas as pl
from jax.experimental.pallas import tpu as pltpu

F32 = jnp.float32
BF16 = jnp.bfloat16
HIGHEST = lax.Precision.HIGHEST

GRID_W = 64
NORM_EPS = 1e-6
N_MOD = 6
HY_ORDER = 2
HY_BANDS = 16
HY_TARGET = 1e-2
HY_FAST_PCT = 0.3
HY_SLOW_PCT = 1.5
HEAD_DIM = 128
N_EXPERTS = 16
N_GROUPS = 4
EXPERTS_PER_GROUP = N_EXPERTS // N_GROUPS
LANES = 128
MOD_ROWS = 16
HGRN_CHUNK = 128
HGRN_GROUP = 4
HGRN_FINE_LEVELS = (2, 4)
VMEM_LIMIT = 56 << 20


def _cparams(*sem, flags=None):
    return pltpu.CompilerParams(dimension_semantics=sem, vmem_limit_bytes=VMEM_LIMIT, flags=flags)


def _sigmoid(x):
    return 1.0 / (1.0 + jnp.exp(-x))


def _dot(a, b, **kw):
    return jnp.dot(a, b, preferred_element_type=F32, **kw)


def _resident(shape):
    return pl.BlockSpec(shape, lambda *_: (0,) * len(shape), pipeline_mode=pl.Buffered(1))


def _neg_abs(x):
    bits = lax.bitcast_convert_type(x, jnp.uint32) | jnp.uint32(0x80000000)
    return lax.bitcast_convert_type(bits, F32)


def _dot_nt(a, b):
    return lax.dot_general(a, b, (((1,), (1,)), ((), ())), preferred_element_type=F32)


def _mod_kernel(c_ref, w_ref, b_ref, o_ref):
    c = c_ref[...]
    cs = c * _sigmoid(c)
    o_ref[0] = _dot(cs, w_ref[0], precision=HIGHEST) + b_ref[0]


def _modulation(cc, w_mod, b_mod):
    depth, d, n = w_mod.shape
    tn = n // 3
    return pl.pallas_call(
        _mod_kernel,
        out_shape=jax.ShapeDtypeStruct((depth, MOD_ROWS, n), F32),
        grid=(depth, n // tn),
        in_specs=[
            pl.BlockSpec((MOD_ROWS, d), lambda l, j: (0, 0)),
            pl.BlockSpec((1, d, tn), lambda l, j: (l, 0, j)),
            pl.BlockSpec((1, 1, tn), lambda l, j: (l, 0, j)),
        ],
        out_specs=pl.BlockSpec((1, MOD_ROWS, tn), lambda l, j: (l, 0, j)),
        compiler_params=_cparams("parallel", "parallel"),
        name="modulation",
    )(cc, w_mod, b_mod.reshape(depth, 1, n))


def _dft_kernel(ac_ref, as_ref, bc_ref, bs_ref, c_ref, s_ref, *, n1):
    bc = bc_ref[...]
    bs = bs_ref[...]
    ac = ac_ref[...]
    as_ = as_ref[...]
    for j in range(n1):
        a_c = ac[:, j:j + 1]
        a_s = as_[:, j:j + 1]
        c_ref[:, j * LANES:(j + 1) * LANES] = (a_c * bc - a_s * bs).astype(BF16)
        s_ref[:, j * LANES:(j + 1) * LANES] = (a_s * bc + a_c * bs).astype(BF16)


def _dft_tables(L):
    n1 = L // LANES
    period = 4 * L
    r = np.arange(L, dtype=np.int64)[:, None]
    c1 = np.arange(n1, dtype=np.int64)[None, :]
    c0 = np.arange(LANES, dtype=np.int64)[None, :]

    def cs(phase):
        ang = (phase % period).astype(np.float64) * (2.0 * np.pi / period)
        return np.cos(ang).astype(np.float32), np.sin(ang).astype(np.float32)

    fa = cs((2 * r + 1) * (LANES * c1))
    fb = cs((2 * r + 1) * c0)
    ta = cs((2 * LANES * c1) * r)
    tb = cs((2 * c0 + 1) * r)
    return (fa, fb), (ta, tb)


def _dft_matrices(L):
    n1 = L // LANES
    tr = min(L, 256)
    outs = []
    for (ac, as_), (bc, bs) in _dft_tables(L):
        c, s = pl.pallas_call(
            functools.partial(_dft_kernel, n1=n1),
            out_shape=(jax.ShapeDtypeStruct((L, L), BF16), jax.ShapeDtypeStruct((L, L), BF16)),
            grid=(L // tr,),
            in_specs=[
                pl.BlockSpec((tr, n1), lambda i: (i, 0)),
                pl.BlockSpec((tr, n1), lambda i: (i, 0)),
                pl.BlockSpec((tr, LANES), lambda i: (i, 0)),
                pl.BlockSpec((tr, LANES), lambda i: (i, 0)),
            ],
            out_specs=(pl.BlockSpec((tr, L), lambda i: (i, 0)), pl.BlockSpec((tr, L), lambda i: (i, 0))),
            compiler_params=_cparams("parallel"),
            name="dft_tables",
        )(jnp.asarray(ac), jnp.asarray(as_), jnp.asarray(bc), jnp.asarray(bs))
        outs.append((c, s))
    return outs


def _inproj_kernel(x_ref, mod_ref, w_ref, cw_ref, cb_ref, u_ref, hg_ref, *, period, n_hy, tn):
    x = x_ref[0]
    tm = x.shape[0]
    xn = x * lax.rsqrt(jnp.mean(x * x, axis=-1, keepdims=True) + NORM_EPS)
    m = mod_ref[0]
    a = (xn * (1.0 + m[1:2]) + m[0:1]).astype(BF16)
    pos = lax.broadcasted_iota(jnp.int32, (tm, 1), 0) % period
    first = pos == 0
    last = pos == period - 1
    n_total = w_ref.shape[1]
    for j in range(n_total // tn):
        p = _dot(a, w_ref[:, j * tn:(j + 1) * tn])
        if j * tn < n_hy:
            cw = cw_ref[:, j * tn:(j + 1) * tn]
            prev = jnp.where(first, 0.0, pltpu.roll(p, 1, 0))
            nxt = jnp.where(last, 0.0, pltpu.roll(p, tm - 1, 0))
            u_ref[0, :, j * tn:(j + 1) * tn] = (
                prev * cw[0:1] + p * cw[1:2] + nxt * cw[2:3] + cb_ref[:, j * tn:(j + 1) * tn]
            ).astype(u_ref.dtype)
        else:
            hg_ref[0, :, j * tn - n_hy:(j + 1) * tn - n_hy] = p.astype(hg_ref.dtype)


def _inproj(x, mod_l, mod_row, w_in, conv_w, conv_b, *, period, tm):
    bsz, L, d = x.shape
    n_hy = conv_w.shape[1]
    n_all = w_in.shape[1]
    tn = n_hy // 3
    assert tm % period == 0 or period == L == tm
    return pl.pallas_call(
        functools.partial(_inproj_kernel, period=period, n_hy=n_hy, tn=tn),
        out_shape=(jax.ShapeDtypeStruct((bsz, L, n_hy), BF16),
                   jax.ShapeDtypeStruct((bsz, L, n_all - n_hy), BF16)),
        grid=(bsz, L // tm),
        in_specs=[
            pl.BlockSpec((1, tm, d), lambda b, i: (b, i, 0)),
            pl.BlockSpec((1, N_MOD, d), lambda b, i: (mod_row(b), 0, 0)),
            _resident((d, n_all)),
            _resident((3, n_hy)),
            _resident((1, n_hy)),
        ],
        out_specs=(pl.BlockSpec((1, tm, n_hy), lambda b, i: (b, i, 0)),
                   pl.BlockSpec((1, tm, n_all - n_hy), lambda b, i: (b, i, 0))),
        compiler_params=_cparams("parallel", "parallel"),
        name="inproj",
    )(x, mod_l, w_in, conv_w, conv_b.reshape(1, n_hy))


def _filter_kernel(z_ref, w1_ref, b1_ref, w2_ref, b2_ref, w3f_ref, b3f_ref, w3b_ref, b3b_ref,
                   dl_ref, o_ref, h_s):
    z = z_ref[...]

    @pl.when((pl.program_id(0) == 0) & (pl.program_id(1) == 0))
    def _():
        h1 = jnp.sin(_dot(z, w1_ref[...], precision=HIGHEST) + b1_ref[...])
        h_s[...] = jnp.sin(_dot(h1, w2_ref[...], precision=HIGHEST) + b2_ref[...])

    h = h_s[...]
    hf = _dot(h, w3f_ref[...], precision=HIGHEST) + b3f_ref[...]
    hb = _dot(h, w3b_ref[...], precision=HIGHEST) + b3b_ref[...]
    win = jnp.exp(-z[:, 0:1] * dl_ref[...])
    hf = hf * win
    hb = hb * win
    nrm = (jnp.sum(jnp.abs(hf), axis=0, keepdims=True)
           + jnp.sum(jnp.abs(hb), axis=0, keepdims=True))
    inv = 1.0 / nrm
    hf = hf * inv
    row = lax.broadcasted_iota(jnp.int32, (z.shape[0], 1), 0)
    hb0 = jnp.where(row == 0, 0.0, hb * inv)
    o_ref[0, 0] = hf + hb0
    o_ref[0, 1] = hf - hb0


def _hyena_filter_taps(L, w1, b1, w2, b2, w3, b3):
    nfeat, hid = w1.shape
    c = w3.shape[1] // (2 * HY_ORDER)
    tc = min(c, 256)
    nct = c // tc
    t = jnp.linspace(0.0, 1.0, L, dtype=F32)
    n = jnp.arange(L, dtype=F32)
    freqs = jnp.linspace(1e-4, HY_BANDS - 1, HY_BANDS, dtype=F32)
    ang = (2.0 * math.pi / L) * n[:, None] * freqs[None, :]
    z = jnp.concatenate([t[:, None], jnp.cos(ang), -jnp.sin(ang)], axis=-1)
    z = jnp.pad(z, ((0, 0), (0, LANES - nfeat)))
    w1p = jnp.pad(w1, ((0, LANES - nfeat), (0, 0)))
    deltas = jnp.abs(jnp.linspace(math.log(HY_TARGET) / HY_FAST_PCT, math.log(HY_TARGET) / HY_SLOW_PCT,
                                  c, dtype=F32)).reshape(1, c)
    full = lambda shape: pl.BlockSpec(shape, lambda o, j: (0,) * len(shape))
    return pl.pallas_call(
        _filter_kernel,
        out_shape=jax.ShapeDtypeStruct((HY_ORDER, 2, L, c), F32),
        grid=(HY_ORDER, nct),
        in_specs=[
            full((L, LANES)), full((LANES, hid)), full((1, hid)), full((hid, hid)), full((1, hid)),
            pl.BlockSpec((hid, tc), lambda o, j: (0, o * 2 * nct + j)),
            pl.BlockSpec((1, tc), lambda o, j: (0, o * 2 * nct + j)),
            pl.BlockSpec((hid, tc), lambda o, j: (0, o * 2 * nct + nct + j)),
            pl.BlockSpec((1, tc), lambda o, j: (0, o * 2 * nct + nct + j)),
            pl.BlockSpec((1, tc), lambda o, j: (0, j)),
        ],
        out_specs=pl.BlockSpec((1, 2, L, tc), lambda o, j: (o, 0, 0, j)),
        scratch_shapes=[pltpu.VMEM((L, hid), F32)],
        compiler_params=_cparams("arbitrary", "arbitrary"),
        name="hyena_filter",
    )(z, w1p, b1.reshape(1, hid), w2, b2.reshape(1, hid), w3, b3.reshape(1, -1), w3, b3.reshape(1, -1),
      deltas)


def _kspec_kernel(cf_ref, sf_ref, h_ref, o_ref, *, scale):
    o_ref[0, 0] = _dot(cf_ref[...], h_ref[0, 0].astype(BF16)) * scale
    o_ref[0, 1] = _dot(sf_ref[...], h_ref[0, 1].astype(BF16)) * scale


def _filter_spectrum(cft, sft, taps):
    _, _, L, c = taps.shape
    tk = min(L, 512)
    return pl.pallas_call(
        functools.partial(_kspec_kernel, scale=1.0 / L),
        out_shape=jax.ShapeDtypeStruct((HY_ORDER, 2, L, c), F32),
        grid=(HY_ORDER, L // tk),
        in_specs=[
            pl.BlockSpec((tk, L), lambda o, i: (i, 0)),
            pl.BlockSpec((tk, L), lambda o, i: (i, 0)),
            pl.BlockSpec((1, 2, L, c), lambda o, i: (o, 0, 0, 0)),
        ],
        out_specs=pl.BlockSpec((1, 2, tk, c), lambda o, i: (o, 0, i, 0)),
        compiler_params=_cparams("parallel", "parallel"),
        name="hyena_filter_spectrum",
    )(cft, sft, taps)


def _hy_fwd_kernel(cf_ref, sf_ref, v_ref, k_ref, p_ref, q_ref):
    v = v_ref[0].astype(BF16)
    a = _dot(cf_ref[...], v)
    b = _dot(sf_ref[...], v)
    kr = k_ref[0, 0]
    ks = k_ref[0, 1]
    p_ref[0] = (a * kr - b * ks).astype(BF16)
    q_ref[0] = (a * ks + b * kr).astype(BF16)


def _hy_forward(cft, sft, src, src_col, kspec, order):
    bsz, L, _ = src.shape
    c = kspec.shape[-1]
    tk = min(L, 1024)
    return pl.pallas_call(
        _hy_fwd_kernel,
        out_shape=(jax.ShapeDtypeStruct((bsz, L, c), BF16), jax.ShapeDtypeStruct((bsz, L, c), BF16)),
        grid=(L // tk, bsz),
        in_specs=[
            pl.BlockSpec((tk, L), lambda i, b: (i, 0)),
            pl.BlockSpec((tk, L), lambda i, b: (i, 0)),
            pl.BlockSpec((1, L, c), lambda i, b: (b, 0, src_col)),
            pl.BlockSpec((1, 2, tk, c), lambda i, b: (order, 0, i, 0)),
        ],
        out_specs=(pl.BlockSpec((1, tk, c), lambda i, b: (b, i, 0)),
                   pl.BlockSpec((1, tk, c), lambda i, b: (b, i, 0))),
        compiler_params=_cparams("parallel", "parallel"),
        name="hyena_spectrum",
    )(cft, sft, src, kspec)


def _hy_inv_kernel(cf_ref, sf_ref, p_ref, q_ref, gate_ref, src_ref, d_ref, o_ref):
    conv = _dot(cf_ref[...], p_ref[0]) + _dot(sf_ref[...], q_ref[0])
    o_ref[0] = (gate_ref[0] * (conv + src_ref[0] * d_ref[0])).astype(o_ref.dtype)


def _hy_inverse(cf, sf, p, q, gate, gate_col, src, src_col, d, order, out_dtype):
    bsz, L, c = p.shape
    tm = min(L, 1024)
    return pl.pallas_call(
        _hy_inv_kernel,
        out_shape=jax.ShapeDtypeStruct((bsz, L, c), out_dtype),
        grid=(L // tm, bsz),
        in_specs=[
            pl.BlockSpec((tm, L), lambda i, b: (i, 0)),
            pl.BlockSpec((tm, L), lambda i, b: (i, 0)),
            pl.BlockSpec((1, L, c), lambda i, b: (b, 0, 0)),
            pl.BlockSpec((1, L, c), lambda i, b: (b, 0, 0)),
            pl.BlockSpec((1, tm, c), lambda i, b: (b, i, gate_col)),
            pl.BlockSpec((1, tm, c), lambda i, b: (b, i, src_col)),
            pl.BlockSpec((1, 1, c), lambda i, b: (order, 0, 0)),
        ],
        out_specs=pl.BlockSpec((1, tm, c), lambda i, b: (b, i, 0)),
        compiler_params=_cparams("parallel", "parallel"),
        name="hyena_inverse",
    )(cf, sf, p, q, gate, src, d.reshape(HY_ORDER, 1, c))


def _hyena(u, dft, taps_w, d):
    (cft, sft), (cf, sf) = dft
    L = u.shape[1]
    taps = _hyena_filter_taps(L, *taps_w)
    kspec = _filter_spectrum(cft, sft, taps)
    p, q = _hy_forward(cft, sft, u, 0, kspec, 0)
    z = _hy_inverse(cf, sf, p, q, u, 1, u, 0, d, 0, BF16)
    p, q = _hy_forward(cft, sft, z, 0, kspec, 1)
    return _hy_inverse(cf, sf, p, q, u, 2, z, 0, d, 1, BF16)


def _anchor_rows(b, n, a):
    assert n % 8 == 0
    parts = [jnp.broadcast_to(b[s + a:s + a + 1, :], (n, b.shape[1])) for s in range(0, b.shape[0], n)]
    return parts[0] if len(parts) == 1 else jnp.concatenate(parts, axis=0)


def _hgrn_group(q_raw, v, f_logit, lb, st_ref, lv, tri, *, reverse):
    C = HGRN_CHUNK
    G = q_raw.shape[0] // C
    rows = lambda x, i: x[i * C:(i + 1) * C]
    q = q_raw * _sigmoid(q_raw)
    f = lb + (1.0 - lb) * _sigmoid(f_logit)
    kk = 1.0 - f
    g = jnp.log2(f)
    g_top = lax.bitcast_convert_type(
        lax.bitcast_convert_type(g, jnp.uint32) & jnp.uint32(0xFFFF0000), F32)
    gg = jnp.concatenate([g_top.astype(BF16), (g - g_top).astype(BF16)], axis=1)
    bb = [_dot(tri, rows(gg, i)) for i in range(G)]
    part = lambda k: jnp.concatenate(
        [x[k * C:(k + 1) * C, :HEAD_DIM] + x[k * C:(k + 1) * C, HEAD_DIM:] for x in bb], axis=0)
    b = part(0)
    fine = {n_: part(k + 1) for k, n_ in enumerate(HGRN_FINE_LEVELS)}

    qb = q.astype(BF16)
    kb = kk.astype(BF16)
    scores = [jnp.zeros((C, C), F32)] * G
    n = 2
    level = 1
    while n <= C:
        a = n // 2 if reverse else n // 2 - 1
        d = fine[n] if n in fine else b - _anchor_rows(b, n, a)
        e = jnp.exp2(_neg_abs(d)).astype(BF16)
        qe = qb * e
        ke = kb * e
        scores = [jnp.where(lv == level, _dot_nt(rows(qe, i), rows(ke, i)), scores[i]) for i in range(G)]
        n *= 2
        level += 1

    vb = v.astype(BF16)
    diag = jnp.sum(q * kk, axis=-1, keepdims=True) * v
    o = [_dot(scores[i].astype(BF16), rows(vb, i)) + rows(diag, i) for i in range(G)]
    b_end = _anchor_rows(b, C, 0 if reverse else C - 1)
    qd = qb * jnp.exp2(b).astype(BF16)
    kd = kb * jnp.exp2(b_end - b).astype(BF16)
    dec = jnp.exp2(b_end)
    kv = [_dot(rows(v, i).T.astype(BF16), rows(kd, i)) for i in range(G)]
    st = st_ref[...]
    for i in (range(G - 1, -1, -1) if reverse else range(G)):
        o[i] = o[i] + _dot_nt(rows(qd, i), st.astype(BF16))
        st = st * dec[i * C:i * C + 1] + kv[i]
    st_ref[...] = st
    return jnp.concatenate(o, axis=0)


def _hgrn_kernel(qc, ic, gc, fc, bc, ql, il, gl, fl, bl, lb_ref, nw_ref, oc_ref, ol_ref,
                 of_s, ob_s, stf, stb, *, n_ctx, n_lat):
    C = HGRN_CHUNK
    row = lax.broadcasted_iota(jnp.int32, (C, C), 0)
    col = lax.broadcasted_iota(jnp.int32, (C, C), 1)
    x = row ^ col
    lvl = jnp.zeros((C, C), jnp.int32)
    n = 1
    while n < C:
        lvl = lvl + (x >= n).astype(jnp.int32)
        n *= 2
    lv_f = jnp.where(row > col, lvl, 0)
    lv_b = jnp.where(row < col, lvl, 0)
    def cum_matrix(reverse):
        cum = lambda r: (col >= r) if reverse else (col <= r)
        mats = [cum(row).astype(F32)]
        for n_ in HGRN_FINE_LEVELS:
            anchor = row - row % n_ + (n_ // 2 if reverse else n_ // 2 - 1)
            mats.append(mats[0] - cum(anchor).astype(F32))
        return jnp.concatenate(mats, axis=0).astype(BF16)

    tri_f = cum_matrix(False)
    tri_b = cum_matrix(True)
    lb_f = lb_ref[0, 0:1, :]
    lb_b = lb_ref[0, 1:2, :]
    stf[...] = jnp.zeros_like(stf)
    stb[...] = jnp.zeros_like(stb)

    def run(q_ref, i_ref, f_ref, b_ref, n_chunks, base):
        G = min(n_chunks, HGRN_GROUP)
        R = G * C
        n_groups = n_chunks // G

        def body(j, carry):
            off_f = pl.multiple_of(j * R, R)
            off_b = pl.multiple_of((n_groups - 1 - j) * R, R)
            rf = pl.ds(off_f, R)
            rb = pl.ds(off_b, R)
            ld = lambda ref, r: ref[0, r, :].astype(F32)
            of_s[pl.ds(base + off_f, R), :] = _hgrn_group(
                ld(q_ref, rf), ld(i_ref, rf), ld(f_ref, rf), lb_f, stf, lv_f, tri_f, reverse=False)
            ob_s[pl.ds(base + off_b, R), :] = _hgrn_group(
                ld(q_ref, rb), ld(i_ref, rb), ld(b_ref, rb), lb_b, stb, lv_b, tri_b, reverse=True)
            return carry
        lax.fori_loop(0, n_groups, body, 0)

    run(qc, ic, fc, bc, n_ctx, 0)
    run(ql, il, fl, bl, n_lat, n_ctx * C)

    nw = nw_ref[...]

    def finish(g_ref, o_ref, n_chunks, base):
        def body(j, carry):
            r = pl.ds(pl.multiple_of(j * C, C), C)
            rs = pl.ds(pl.multiple_of(base + j * C, C), C)
            o = of_s[rs, :] + ob_s[rs, :]
            o = o * lax.rsqrt(jnp.mean(o * o, axis=-1, keepdims=True) + NORM_EPS) * nw
            g = g_ref[0, r, :].astype(F32)
            o_ref[0, r, :] = (o * (g * _sigmoid(g))).astype(o_ref.dtype)
            return carry
        lax.fori_loop(0, n_chunks, body, 0)

    finish(gc, oc_ref, n_ctx, 0)
    finish(gl, ol_ref, n_lat, n_ctx * C)


def _hgrn(hg_ctx, hg_lat, lb, norm_w):
    bsz, lc, n5 = hg_ctx.shape
    ll = hg_lat.shape[1]
    dg = n5 // 5
    nh = dg // HEAD_DIM
    C = HGRN_CHUNK
    assert lc % C == 0 and ll % C == 0
    assert all((n // C) % min(n // C, HGRN_GROUP) == 0 for n in (lc, ll))

    def slab(L, part):
        return pl.BlockSpec((1, L, HEAD_DIM), lambda b, h: (b, 0, part * nh + h))

    return pl.pallas_call(
        functools.partial(_hgrn_kernel, n_ctx=lc // C, n_lat=ll // C),
        out_shape=(jax.ShapeDtypeStruct((bsz, lc, dg), BF16), jax.ShapeDtypeStruct((bsz, ll, dg), BF16)),
        grid=(bsz, nh),
        in_specs=[slab(lc, p) for p in range(5)] + [slab(ll, p) for p in range(5)] + [
            pl.BlockSpec((1, 2, HEAD_DIM), lambda b, h: (h, 0, 0)),
            pl.BlockSpec((1, HEAD_DIM), lambda b, h: (0, 0)),
        ],
        out_specs=(pl.BlockSpec((1, lc, HEAD_DIM), lambda b, h: (b, 0, h)),
                   pl.BlockSpec((1, ll, HEAD_DIM), lambda b, h: (b, 0, h))),
        scratch_shapes=[
            pltpu.VMEM((lc + ll, HEAD_DIM), F32),
            pltpu.VMEM((lc + ll, HEAD_DIM), F32),
            pltpu.VMEM((HEAD_DIM, HEAD_DIM), F32),
            pltpu.VMEM((HEAD_DIM, HEAD_DIM), F32),
        ],
        compiler_params=_cparams("parallel", "parallel"),
        name="hgrn2",
    )(*([hg_ctx] * 5), *([hg_lat] * 5), lb, norm_w.reshape(1, HEAD_DIM))


def _first_argmax(vals, lane, sentinel):
    m = jnp.max(vals, axis=-1, keepdims=True)
    idx = jnp.min(jnp.where(vals == m, lane, sentinel), axis=-1, keepdims=True)
    return m, idx


def _moe_gates(logits, rbias):
    tm, ne = logits.shape
    neg = -jnp.inf
    mx = jnp.max(logits, axis=-1, keepdims=True)
    ex = jnp.exp(logits - mx)
    scores = ex / jnp.sum(ex, axis=-1, keepdims=True)
    sel = scores + rbias
    lane = lax.broadcasted_iota(jnp.int32, (tm, ne), 1)
    grp = lane // EXPERTS_PER_GROUP
    best_s = jnp.full((tm, 1), neg, F32)
    best_g = jnp.zeros((tm, 1), jnp.int32)
    for gi in range(N_GROUPS):
        mg = jnp.where(grp == gi, sel, neg)
        m1, i1 = _first_argmax(mg, lane, ne)
        m2 = jnp.max(jnp.where(lane == i1, neg, mg), axis=-1, keepdims=True)
        gs = m1 + m2
        upd = gs > best_s
        best_g = jnp.where(upd, gi, best_g)
        best_s = jnp.where(upd, gs, best_s)
    sg = jnp.where(grp == best_g, sel, neg)
    _, i1 = _first_argmax(sg, lane, ne)
    _, i2 = _first_argmax(jnp.where(lane == i1, neg, sg), lane, ne)
    chosen = (lane == i1) | (lane == i2)
    w = jnp.where(chosen, scores, 0.0)
    return w / jnp.sum(w, axis=-1, keepdims=True)


def _mix_moe_kernel(x_ref, hy_ref, hg_ref, mod_ref, w_ref, rw_ref, rb_ref, wg_ref, wu_ref, wd_ref,
                    o_ref, act_s):
    c = hy_ref.shape[-1]
    mix = _dot(hy_ref[0], w_ref[:c, :]) + _dot(hg_ref[0], w_ref[c:, :])
    m = mod_ref[0]
    x1 = x_ref[0] + m[2:3] * mix
    xn = x1 * lax.rsqrt(jnp.mean(x1 * x1, axis=-1, keepdims=True) + NORM_EPS)
    t = xn * (1.0 + m[4:5]) + m[3:4]
    t_hi = t.astype(BF16)
    t_lo = (t - t_hi.astype(F32)).astype(BF16)
    r = _dot(t_hi, rw_ref[...])
    lg = r[:, :LANES] + r[:, LANES:] + _dot(t_lo, rw_ref[:, :LANES])
    ne, _, f = wg_ref.shape
    gates = _moe_gates(lg[:, :ne], rb_ref[...])
    for e in range(ne):
        hgate = _dot(t_hi, wg_ref[e])
        hup = _dot(t_hi, wu_ref[e])
        act_s[:, e * f:(e + 1) * f] = (hgate * _sigmoid(hgate) * hup * gates[:, e:e + 1]).astype(BF16)
    o_ref[0] = x1 + m[5:6] * _dot(act_s[...], wd_ref[...])


def _mix_moe(x, hy, hg, mod_l, mod_row, w_out, router_w, rbias, wg, wu, wd, *, tm):
    bsz, L, d = x.shape
    c = hy.shape[-1]
    cg = hg.shape[-1]
    ne, _, f = wg.shape
    rw_hi = router_w.astype(BF16)
    rw_lo = (router_w - rw_hi.astype(F32)).astype(BF16)
    rw = jnp.zeros((d, 2 * LANES), BF16).at[:, :ne].set(rw_hi).at[:, LANES:LANES + ne].set(rw_lo)
    return pl.pallas_call(
        _mix_moe_kernel,
        out_shape=jax.ShapeDtypeStruct((bsz, L, d), F32),
        grid=(bsz, L // tm),
        in_specs=[
            pl.BlockSpec((1, tm, d), lambda b, i: (b, i, 0)),
            pl.BlockSpec((1, tm, c), lambda b, i: (b, i, 0)),
            pl.BlockSpec((1, tm, cg), lambda b, i: (b, i, 0)),
            pl.BlockSpec((1, N_MOD, d), lambda b, i: (mod_row(b), 0, 0)),
            _resident((c + cg, d)),
            _resident((d, 2 * LANES)),
            _resident((1, ne)),
            _resident((ne, d, f)),
            _resident((ne, d, f)),
            _resident((ne * f, d)),
        ],
        out_specs=pl.BlockSpec((1, tm, d), lambda b, i: (b, i, 0)),
        scratch_shapes=[pltpu.VMEM((tm, ne * f), BF16)],
        compiler_params=_cparams("parallel", "parallel"),
        name="mix_moe",
    )(x, hy, hg, mod_l, w_out, rw, rbias.reshape(1, ne), wg, wu, wd.reshape(ne * f, d))


def _final_norm_kernel(x_ref, w_ref, o_ref):
    x = x_ref[0]
    o_ref[0] = x * lax.rsqrt(jnp.mean(x * x, axis=-1, keepdims=True) + NORM_EPS) * w_ref[...]


def _final_norm(x, w, *, tm):
    bsz, L, d = x.shape
    return pl.pallas_call(
        _final_norm_kernel,
        out_shape=jax.ShapeDtypeStruct(x.shape, F32),
        grid=(bsz, L // tm),
        in_specs=[pl.BlockSpec((1, tm, d), lambda b, i: (b, i, 0)), pl.BlockSpec((1, d), lambda b, i: (0, 0))],
        out_specs=pl.BlockSpec((1, tm, d), lambda b, i: (b, i, 0)),
        compiler_params=_cparams("parallel", "parallel"),
        name="final_norm",
    )(x, w.reshape(1, d))


def kernel(x, c, ctx, c_ctx, w_mod, b_mod, w_in, w_out, hy_conv_w, hy_conv_b, hy_w1, hy_b1, hy_w2, hy_b2, hy_w3, hy_b3, hy_bias, hgrn_lower_bounds, hgrn_norm_w, router_w, router_bias, moe_w_gate, moe_w_up, moe_w_down, final_norm_w):
    bsz, seq, d = x.shape
    ctx_len = ctx.shape[1]
    depth = w_mod.shape[0]
    dg = hgrn_lower_bounds.shape[-1]
    nh = dg // HEAD_DIM
    assert bsz + 1 <= MOD_ROWS and seq % GRID_W == 0

    cc = jnp.zeros((MOD_ROWS, d), F32).at[:bsz].set(c).at[bsz].set(c_ctx)
    mod = _modulation(cc, w_mod, b_mod).reshape(depth, MOD_ROWS, N_MOD, d)
    lat_row = lambda b: b
    ctx_row = lambda b: bsz

    lb_soft = jax.nn.softmax(hgrn_lower_bounds.astype(F32), axis=1)
    lower = jnp.cumsum(lb_soft, axis=1) - lb_soft[:, :1]
    lower = lower.reshape(2, depth, nh, HEAD_DIM).transpose(1, 2, 0, 3)

    dft_lat = _dft_matrices(seq)
    dft_ctx = _dft_matrices(ctx_len)
    tm_lat = min(seq, 512)
    tm_moe = min(seq, 512)

    xc = ctx
    for l in range(depth):
        last = l == depth - 1
        w_in_l = w_in[l].astype(BF16)
        w_out_l = w_out[l].astype(BF16)
        experts = (moe_w_gate[l].astype(BF16), moe_w_up[l].astype(BF16), moe_w_down[l].astype(BF16))
        u_lat, hg_lat = _inproj(x, mod[l], lat_row, w_in_l, hy_conv_w[l], hy_conv_b[l],
                                period=GRID_W, tm=tm_lat)
        u_ctx, hg_ctx = _inproj(xc, mod[l], ctx_row, w_in_l, hy_conv_w[l], hy_conv_b[l],
                                period=ctx_len, tm=ctx_len)
        o_ctx, o_lat = _hgrn(hg_ctx, hg_lat, lower[l], hgrn_norm_w[l])
        taps_w = (hy_w1[l], hy_b1[l], hy_w2[l], hy_b2[l], hy_w3[l], hy_b3[l])
        hy_lat = _hyena(u_lat, dft_lat, taps_w, hy_bias[l])
        x = _mix_moe(x, hy_lat, o_lat, mod[l], lat_row, w_out_l, router_w, router_bias, *experts, tm=tm_moe)
        if not last:
            hy_ctx = _hyena(u_ctx, dft_ctx, taps_w, hy_bias[l])
            xc = _mix_moe(xc, hy_ctx, o_ctx, mod[l], ctx_row, w_out_l, router_w, router_bias, *experts,
                          tm=ctx_len)

    return _final_norm(x, final_norm_w, tm=tm_lat)
```

```python
import functools
import math

import numpy as np
import jax
import jax.numpy as jnp
from jax import lax
from jax.experimental import pallas as pl
from jax.experimental.pallas import tpu as pltpu

F32 = jnp.float32
BF16 = jnp.bfloat16
HIGHEST = lax.Precision.HIGHEST

GRID_W = 64
NORM_EPS = 1e-6
N_MOD = 6
HY_ORDER = 2
HY_BANDS = 16
HY_TARGET = 1e-2
HY_FAST_PCT = 0.3
HY_SLOW_PCT = 1.5
HEAD_DIM = 128
N_EXPERTS = 16
N_GROUPS = 4
EXPERTS_PER_GROUP = N_EXPERTS // N_GROUPS
LANES = 128
MOD_ROWS = 16
HGRN_CHUNK = 128
HGRN_GROUP = 16
HGRN_FINE_LEVELS = (2, 4)
VMEM_LIMIT = 56 << 20


def _cparams(*sem, flags=None):
    return pltpu.CompilerParams(dimension_semantics=sem, vmem_limit_bytes=VMEM_LIMIT, flags=flags)


def _sigmoid(x):
    return 1.0 / (1.0 + jnp.exp(-x))


def _dot(a, b, **kw):
    return jnp.dot(a, b, preferred_element_type=F32, **kw)


def _resident(shape):
    return pl.BlockSpec(shape, lambda *_: (0,) * len(shape), pipeline_mode=pl.Buffered(1))


def _neg_abs(x):
    bits = lax.bitcast_convert_type(x, jnp.uint32) | jnp.uint32(0x80000000)
    return lax.bitcast_convert_type(bits, F32)


def _dot_nt(a, b):
    return lax.dot_general(a, b, (((1,), (1,)), ((), ())), preferred_element_type=F32)


def _mod_kernel(c_ref, w_ref, b_ref, o_ref):
    c = c_ref[...]
    cs = c * _sigmoid(c)
    o_ref[0] = _dot(cs, w_ref[0], precision=HIGHEST) + b_ref[0]


def _modulation(cc, w_mod, b_mod):
    depth, d, n = w_mod.shape
    tn = n // 3
    return pl.pallas_call(
        _mod_kernel,
        out_shape=jax.ShapeDtypeStruct((depth, MOD_ROWS, n), F32),
        grid=(depth, n // tn),
        in_specs=[
            pl.BlockSpec((MOD_ROWS, d), lambda l, j: (0, 0)),
            pl.BlockSpec((1, d, tn), lambda l, j: (l, 0, j)),
            pl.BlockSpec((1, 1, tn), lambda l, j: (l, 0, j)),
        ],
        out_specs=pl.BlockSpec((1, MOD_ROWS, tn), lambda l, j: (l, 0, j)),
        compiler_params=_cparams("parallel", "parallel"),
        name="modulation",
    )(cc, w_mod, b_mod.reshape(depth, 1, n))


def _dft_kernel(ac_ref, as_ref, bc_ref, bs_ref, c_ref, s_ref, *, n1):
    bc = bc_ref[...]
    bs = bs_ref[...]
    ac = ac_ref[...]
    as_ = as_ref[...]
    for j in range(n1):
        a_c = ac[:, j:j + 1]
        a_s = as_[:, j:j + 1]
        c_ref[:, j * LANES:(j + 1) * LANES] = (a_c * bc - a_s * bs).astype(BF16)
        s_ref[:, j * LANES:(j + 1) * LANES] = (a_s * bc + a_c * bs).astype(BF16)


def _dft_tables(L):
    n1 = L // LANES
    period = 4 * L
    r = np.arange(L, dtype=np.int64)[:, None]
    c1 = np.arange(n1, dtype=np.int64)[None, :]
    c0 = np.arange(LANES, dtype=np.int64)[None, :]

    def cs(phase):
        ang = (phase % period).astype(np.float64) * (2.0 * np.pi / period)
        return np.cos(ang).astype(np.float32), np.sin(ang).astype(np.float32)

    fa = cs((2 * r + 1) * (LANES * c1))
    fb = cs((2 * r + 1) * c0)
    ta = cs((2 * LANES * c1) * r)
    tb = cs((2 * c0 + 1) * r)
    return (fa, fb), (ta, tb)


def _dft_matrices(L):
    n1 = L // LANES
    tr = min(L, 256)
    outs = []
    for (ac, as_), (bc, bs) in _dft_tables(L):
        c, s = pl.pallas_call(
            functools.partial(_dft_kernel, n1=n1),
            out_shape=(jax.ShapeDtypeStruct((L, L), BF16), jax.ShapeDtypeStruct((L, L), BF16)),
            grid=(L // tr,),
            in_specs=[
                pl.BlockSpec((tr, n1), lambda i: (i, 0)),
                pl.BlockSpec((tr, n1), lambda i: (i, 0)),
                pl.BlockSpec((tr, LANES), lambda i: (i, 0)),
                pl.BlockSpec((tr, LANES), lambda i: (i, 0)),
            ],
            out_specs=(pl.BlockSpec((tr, L), lambda i: (i, 0)), pl.BlockSpec((tr, L), lambda i: (i, 0))),
            compiler_params=_cparams("parallel"),
            name="dft_tables",
        )(jnp.asarray(ac), jnp.asarray(as_), jnp.asarray(bc), jnp.asarray(bs))
        outs.append((c, s))
    return outs


def _inproj_kernel(x_ref, mod_ref, w_ref, cw_ref, cb_ref, u_ref, hg_ref, *, period, n_hy, tn):
    x = x_ref[0]
    tm = x.shape[0]
    xn = x * lax.rsqrt(jnp.mean(x * x, axis=-1, keepdims=True) + NORM_EPS)
    m = mod_ref[0]
    a = (xn * (1.0 + m[1:2]) + m[0:1]).astype(BF16)
    pos = lax.broadcasted_iota(jnp.int32, (tm, 1), 0) % period
    first = pos == 0
    last = pos == period - 1
    n_total = w_ref.shape[1]
    for j in range(n_total // tn):
        p = _dot(a, w_ref[:, j * tn:(j + 1) * tn])
        if j * tn < n_hy:
            cw = cw_ref[:, j * tn:(j + 1) * tn]
            prev = jnp.where(first, 0.0, pltpu.roll(p, 1, 0))
            nxt = jnp.where(last, 0.0, pltpu.roll(p, tm - 1, 0))
            u_ref[0, :, j * tn:(j + 1) * tn] = (
                prev * cw[0:1] + p * cw[1:2] + nxt * cw[2:3] + cb_ref[:, j * tn:(j + 1) * tn]
            ).astype(u_ref.dtype)
        else:
            hg_ref[0, :, j * tn - n_hy:(j + 1) * tn - n_hy] = p.astype(hg_ref.dtype)


def _inproj(x, mod_l, mod_row, w_in, conv_w, conv_b, *, period, tm):
    bsz, L, d = x.shape
    n_hy = conv_w.shape[1]
    n_all = w_in.shape[1]
    tn = n_hy // 3
    assert tm % period == 0 or period == L == tm
    return pl.pallas_call(
        functools.partial(_inproj_kernel, period=period, n_hy=n_hy, tn=tn),
        out_shape=(jax.ShapeDtypeStruct((bsz, L, n_hy), BF16),
                   jax.ShapeDtypeStruct((bsz, L, n_all - n_hy), BF16)),
        grid=(bsz, L // tm),
        in_specs=[
            pl.BlockSpec((1, tm, d), lambda b, i: (b, i, 0)),
            pl.BlockSpec((1, N_MOD, d), lambda b, i: (mod_row(b), 0, 0)),
            _resident((d, n_all)),
            _resident((3, n_hy)),
            _resident((1, n_hy)),
        ],
        out_specs=(pl.BlockSpec((1, tm, n_hy), lambda b, i: (b, i, 0)),
                   pl.BlockSpec((1, tm, n_all - n_hy), lambda b, i: (b, i, 0))),
        compiler_params=_cparams("parallel", "parallel"),
        name="inproj",
    )(x, mod_l, w_in, conv_w, conv_b.reshape(1, n_hy))


def _filter_kernel(z_ref, w1_ref, b1_ref, w2_ref, b2_ref, w3f_ref, b3f_ref, w3b_ref, b3b_ref,
                   dl_ref, o_ref, h_s):
    z = z_ref[...]

    @pl.when((pl.program_id(0) == 0) & (pl.program_id(1) == 0))
    def _():
        h1 = jnp.sin(_dot(z, w1_ref[...], precision=HIGHEST) + b1_ref[...])
        h_s[...] = jnp.sin(_dot(h1, w2_ref[...], precision=HIGHEST) + b2_ref[...])

    h = h_s[...]
    hf = _dot(h, w3f_ref[...], precision=HIGHEST) + b3f_ref[...]
    hb = _dot(h, w3b_ref[...], precision=HIGHEST) + b3b_ref[...]
    win = jnp.exp(-z[:, 0:1] * dl_ref[...])
    hf = hf * win
    hb = hb * win
    nrm = (jnp.sum(jnp.abs(hf), axis=0, keepdims=True)
           + jnp.sum(jnp.abs(hb), axis=0, keepdims=True))
    inv = 1.0 / nrm
    hf = hf * inv
    row = lax.broadcasted_iota(jnp.int32, (z.shape[0], 1), 0)
    hb0 = jnp.where(row == 0, 0.0, hb * inv)
    o_ref[0, 0] = hf + hb0
    o_ref[0, 1] = hf - hb0


def _hyena_filter_taps(L, w1, b1, w2, b2, w3, b3):
    nfeat, hid = w1.shape
    c = w3.shape[1] // (2 * HY_ORDER)
    tc = min(c, 256)
    nct = c // tc
    t = jnp.linspace(0.0, 1.0, L, dtype=F32)
    n = jnp.arange(L, dtype=F32)
    freqs = jnp.linspace(1e-4, HY_BANDS - 1, HY_BANDS, dtype=F32)
    ang = (2.0 * math.pi / L) * n[:, None] * freqs[None, :]
    z = jnp.concatenate([t[:, None], jnp.cos(ang), -jnp.sin(ang)], axis=-1)
    z = jnp.pad(z, ((0, 0), (0, LANES - nfeat)))
    w1p = jnp.pad(w1, ((0, LANES - nfeat), (0, 0)))
    deltas = jnp.abs(jnp.linspace(math.log(HY_TARGET) / HY_FAST_PCT, math.log(HY_TARGET) / HY_SLOW_PCT,
                                  c, dtype=F32)).reshape(1, c)
    full = lambda shape: pl.BlockSpec(shape, lambda o, j: (0,) * len(shape))
    return pl.pallas_call(
        _filter_kernel,
        out_shape=jax.ShapeDtypeStruct((HY_ORDER, 2, L, c), F32),
        grid=(HY_ORDER, nct),
        in_specs=[
            full((L, LANES)), full((LANES, hid)), full((1, hid)), full((hid, hid)), full((1, hid)),
            pl.BlockSpec((hid, tc), lambda o, j: (0, o * 2 * nct + j)),
            pl.BlockSpec((1, tc), lambda o, j: (0, o * 2 * nct + j)),
            pl.BlockSpec((hid, tc), lambda o, j: (0, o * 2 * nct + nct + j)),
            pl.BlockSpec((1, tc), lambda o, j: (0, o * 2 * nct + nct + j)),
            pl.BlockSpec((1, tc), lambda o, j: (0, j)),
        ],
        out_specs=pl.BlockSpec((1, 2, L, tc), lambda o, j: (o, 0, 0, j)),
        scratch_shapes=[pltpu.VMEM((L, hid), F32)],
        compiler_params=_cparams("arbitrary", "arbitrary"),
        name="hyena_filter",
    )(z, w1p, b1.reshape(1, hid), w2, b2.reshape(1, hid), w3, b3.reshape(1, -1), w3, b3.reshape(1, -1),
      deltas)


def _kspec_kernel(cf_ref, sf_ref, h_ref, o_ref, *, scale):
    o_ref[0, 0] = _dot(cf_ref[...], h_ref[0, 0].astype(BF16)) * scale
    o_ref[0, 1] = _dot(sf_ref[...], h_ref[0, 1].astype(BF16)) * scale


def _filter_spectrum(cft, sft, taps):
    _, _, L, c = taps.shape
    tk = min(L, 512)
    return pl.pallas_call(
        functools.partial(_kspec_kernel, scale=1.0 / L),
        out_shape=jax.ShapeDtypeStruct((HY_ORDER, 2, L, c), F32),
        grid=(HY_ORDER, L // tk),
        in_specs=[
            pl.BlockSpec((tk, L), lambda o, i: (i, 0)),
            pl.BlockSpec((tk, L), lambda o, i: (i, 0)),
            pl.BlockSpec((1, 2, L, c), lambda o, i: (o, 0, 0, 0)),
        ],
        out_specs=pl.BlockSpec((1, 2, tk, c), lambda o, i: (o, 0, i, 0)),
        compiler_params=_cparams("parallel", "parallel"),
        name="hyena_filter_spectrum",
    )(cft, sft, taps)


def _hy_fwd_kernel(cf_ref, sf_ref, v_ref, k_ref, p_ref, q_ref):
    v = v_ref[0].astype(BF16)
    a = _dot(cf_ref[...], v)
    b = _dot(sf_ref[...], v)
    kr = k_ref[0, 0]
    ks = k_ref[0, 1]
    p_ref[0] = (a * kr - b * ks).astype(BF16)
    q_ref[0] = (a * ks + b * kr).astype(BF16)


def _hy_forward(cft, sft, src, src_col, kspec, order):
    bsz, L, _ = src.shape
    c = kspec.shape[-1]
    tk = min(L, 1024)
    return pl.pallas_call(
        _hy_fwd_kernel,
        out_shape=(jax.ShapeDtypeStruct((bsz, L, c), BF16), jax.ShapeDtypeStruct((bsz, L, c), BF16)),
        grid=(L // tk, bsz),
        in_specs=[
            pl.BlockSpec((tk, L), lambda i, b: (i, 0)),
            pl.BlockSpec((tk, L), lambda i, b: (i, 0)),
            pl.BlockSpec((1, L, c), lambda i, b: (b, 0, src_col)),
            pl.BlockSpec((1, 2, tk, c), lambda i, b: (order, 0, i, 0)),
        ],
        out_specs=(pl.BlockSpec((1, tk, c), lambda i, b: (b, i, 0)),
                   pl.BlockSpec((1, tk, c), lambda i, b: (b, i, 0))),
        compiler_params=_cparams("parallel", "parallel"),
        name="hyena_spectrum",
    )(cft, sft, src, kspec)


def _hy_inv_kernel(cf_ref, sf_ref, p_ref, q_ref, gate_ref, src_ref, d_ref, o_ref):
    conv = _dot(cf_ref[...], p_ref[0]) + _dot(sf_ref[...], q_ref[0])
    o_ref[0] = (gate_ref[0] * (conv + src_ref[0] * d_ref[0])).astype(o_ref.dtype)


def _hy_inverse(cf, sf, p, q, gate, gate_col, src, src_col, d, order, out_dtype):
    bsz, L, c = p.shape
    tm = min(L, 1024)
    return pl.pallas_call(
        _hy_inv_kernel,
        out_shape=jax.ShapeDtypeStruct((bsz, L, c), out_dtype),
        grid=(L // tm, bsz),
        in_specs=[
            pl.BlockSpec((tm, L), lambda i, b: (i, 0)),
            pl.BlockSpec((tm, L), lambda i, b: (i, 0)),
            pl.BlockSpec((1, L, c), lambda i, b: (b, 0, 0)),
            pl.BlockSpec((1, L, c), lambda i, b: (b, 0, 0)),
            pl.BlockSpec((1, tm, c), lambda i, b: (b, i, gate_col)),
            pl.BlockSpec((1, tm, c), lambda i, b: (b, i, src_col)),
            pl.BlockSpec((1, 1, c), lambda i, b: (order, 0, 0)),
        ],
        out_specs=pl.BlockSpec((1, tm, c), lambda i, b: (b, i, 0)),
        compiler_params=_cparams("parallel", "parallel"),
        name="hyena_inverse",
    )(cf, sf, p, q, gate, src, d.reshape(HY_ORDER, 1, c))


def _hyena(u, dft, taps_w, d):
    (cft, sft), (cf, sf) = dft
    L = u.shape[1]
    taps = _hyena_filter_taps(L, *taps_w)
    kspec = _filter_spectrum(cft, sft, taps)
    p, q = _hy_forward(cft, sft, u, 0, kspec, 0)
    z = _hy_inverse(cf, sf, p, q, u, 1, u, 0, d, 0, BF16)
    p, q = _hy_forward(cft, sft, z, 0, kspec, 1)
    return _hy_inverse(cf, sf, p, q, u, 2, z, 0, d, 1, BF16)


def _anchor_rows(b, n, a):
    assert n % 8 == 0
    parts = [jnp.broadcast_to(b[s + a:s + a + 1, :], (n, b.shape[1])) for s in range(0, b.shape[0], n)]
    return parts[0] if len(parts) == 1 else jnp.concatenate(parts, axis=0)


def _hgrn_group(q_raw, v, f_logit, lb, st_ref, lv, tri, *, reverse):
    C = HGRN_CHUNK
    G = q_raw.shape[0] // C
    rows = lambda x, i: x[i * C:(i + 1) * C]
    q = q_raw * _sigmoid(q_raw)
    f = lb + (1.0 - lb) * _sigmoid(f_logit)
    kk = 1.0 - f
    g = jnp.log2(f)
    g_top = lax.bitcast_convert_type(
        lax.bitcast_convert_type(g, jnp.uint32) & jnp.uint32(0xFFFF0000), F32)
    gg = jnp.concatenate([g_top.astype(BF16), (g - g_top).astype(BF16)], axis=1)
    bb = [_dot(tri, rows(gg, i)) for i in range(G)]
    part = lambda k: jnp.concatenate(
        [x[k * C:(k + 1) * C, :HEAD_DIM] + x[k * C:(k + 1) * C, HEAD_DIM:] for x in bb], axis=0)
    b = part(0)
    fine = {n_: part(k + 1) for k, n_ in enumerate(HGRN_FINE_LEVELS)}

    qb = q.astype(BF16)
    kb = kk.astype(BF16)
    scores = [jnp.zeros((C, C), F32)] * G
    n = 2
    level = 1
    while n <= C:
        a = n // 2 if reverse else n // 2 - 1
        d = fine[n] if n in fine else b - _anchor_rows(b, n, a)
        e = jnp.exp2(_neg_abs(d)).astype(BF16)
        qe = qb * e
        ke = kb * e
        scores = [jnp.where(lv == level, _dot_nt(rows(qe, i), rows(ke, i)), scores[i]) for i in range(G)]
        n *= 2
        level += 1

    vb = v.astype(BF16)
    diag = jnp.sum(q * kk, axis=-1, keepdims=True) * v
    o = [_dot(scores[i].astype(BF16), rows(vb, i)) + rows(diag, i) for i in range(G)]
    b_end = _anchor_rows(b, C, 0 if reverse else C - 1)
    qd = qb * jnp.exp2(b).astype(BF16)
    kd = kb * jnp.exp2(b_end - b).astype(BF16)
    dec = jnp.exp2(b_end)
    kv = [_dot(rows(v, i).T.astype(BF16), rows(kd, i)) for i in range(G)]
    st = st_ref[...]
    for i in (range(G - 1, -1, -1) if reverse else range(G)):
        o[i] = o[i] + _dot_nt(rows(qd, i), st.astype(BF16))
        st = st * dec[i * C:i * C + 1] + kv[i]
    st_ref[...] = st
    return jnp.concatenate(o, axis=0)


def _hgrn_kernel(qc, ic, gc, fc, bc, ql, il, gl, fl, bl, lb_ref, nw_ref, oc_ref, ol_ref,
                 of_s, ob_s, stf, stb, *, n_ctx, n_lat):
    C = HGRN_CHUNK
    row = lax.broadcasted_iota(jnp.int32, (C, C), 0)
    col = lax.broadcasted_iota(jnp.int32, (C, C), 1)
    x = row ^ col
    lvl = jnp.zeros((C, C), jnp.int32)
    n = 1
    while n < C:
        lvl = lvl + (x >= n).astype(jnp.int32)
        n *= 2
    lv_f = jnp.where(row > col, lvl, 0)
    lv_b = jnp.where(row < col, lvl, 0)
    def cum_matrix(reverse):
        cum = lambda r: (col >= r) if reverse else (col <= r)
        mats = [cum(row).astype(F32)]
        for n_ in HGRN_FINE_LEVELS:
            anchor = row - row % n_ + (n_ // 2 if reverse else n_ // 2 - 1)
            mats.append(mats[0] - cum(anchor).astype(F32))
        return jnp.concatenate(mats, axis=0).astype(BF16)

    tri_f = cum_matrix(False)
    tri_b = cum_matrix(True)
    lb_f = lb_ref[0, 0:1, :]
    lb_b = lb_ref[0, 1:2, :]
    stf[...] = jnp.zeros_like(stf)
    stb[...] = jnp.zeros_like(stb)

    def run(q_ref, i_ref, f_ref, b_ref, n_chunks, base):
        G = min(n_chunks, HGRN_GROUP)
        R = G * C
        n_groups = n_chunks // G

        def body(j, carry):
            off_f = pl.multiple_of(j * R, R)
            off_b = pl.multiple_of((n_groups - 1 - j) * R, R)
            rf = pl.ds(off_f, R)
            rb = pl.ds(off_b, R)
            ld = lambda ref, r: ref[0, r, :].astype(F32)
            of_s[pl.ds(base + off_f, R), :] = _hgrn_group(
                ld(q_ref, rf), ld(i_ref, rf), ld(f_ref, rf), lb_f, stf, lv_f, tri_f, reverse=False)
            ob_s[pl.ds(base + off_b, R), :] = _hgrn_group(
                ld(q_ref, rb), ld(i_ref, rb), ld(b_ref, rb), lb_b, stb, lv_b, tri_b, reverse=True)
            return carry
        lax.fori_loop(0, n_groups, body, 0)

    run(qc, ic, fc, bc, n_ctx, 0)
    run(ql, il, fl, bl, n_lat, n_ctx * C)

    nw = nw_ref[...]

    def finish(g_ref, o_ref, n_chunks, base):
        R = C * math.gcd(n_chunks, 4)

        def body(j, carry):
            r = pl.ds(pl.multiple_of(j * R, R), R)
            rs = pl.ds(pl.multiple_of(base + j * R, C), R)
            o = of_s[rs, :] + ob_s[rs, :]
            o = o * lax.rsqrt(jnp.mean(o * o, axis=-1, keepdims=True) + NORM_EPS) * nw
            g = g_ref[0, r, :].astype(F32)
            o_ref[0, r, :] = (o * (g * _sigmoid(g))).astype(o_ref.dtype)
            return carry
        lax.fori_loop(0, n_chunks * C // R, body, 0)

    finish(gc, oc_ref, n_ctx, 0)
    finish(gl, ol_ref, n_lat, n_ctx * C)


def _hgrn(hg_ctx, hg_lat, lb, norm_w):
    bsz, lc, n5 = hg_ctx.shape
    ll = hg_lat.shape[1]
    dg = n5 // 5
    nh = dg // HEAD_DIM
    C = HGRN_CHUNK
    assert lc % C == 0 and ll % C == 0
    assert all((n // C) % min(n // C, HGRN_GROUP) == 0 for n in (lc, ll))

    def slab(L, part):
        return pl.BlockSpec((1, L, HEAD_DIM), lambda b, h: (b, 0, part * nh + h))

    return pl.pallas_call(
        functools.partial(_hgrn_kernel, n_ctx=lc // C, n_lat=ll // C),
        out_shape=(jax.ShapeDtypeStruct((bsz, lc, dg), BF16), jax.ShapeDtypeStruct((bsz, ll, dg), BF16)),
        grid=(bsz, nh),
        in_specs=[slab(lc, p) for p in range(5)] + [slab(ll, p) for p in range(5)] + [
            pl.BlockSpec((1, 2, HEAD_DIM), lambda b, h: (h, 0, 0)),
            pl.BlockSpec((1, HEAD_DIM), lambda b, h: (0, 0)),
        ],
        out_specs=(pl.BlockSpec((1, lc, HEAD_DIM), lambda b, h: (b, 0, h)),
                   pl.BlockSpec((1, ll, HEAD_DIM), lambda b, h: (b, 0, h))),
        scratch_shapes=[
            pltpu.VMEM((lc + ll, HEAD_DIM), F32),
            pltpu.VMEM((lc + ll, HEAD_DIM), F32),
            pltpu.VMEM((HEAD_DIM, HEAD_DIM), F32),
            pltpu.VMEM((HEAD_DIM, HEAD_DIM), F32),
        ],
        compiler_params=_cparams("parallel", "parallel"),
        name="hgrn2",
    )(*([hg_ctx] * 5), *([hg_lat] * 5), lb, norm_w.reshape(1, HEAD_DIM))


def _first_argmax(vals, lane, sentinel):
    m = jnp.max(vals, axis=-1, keepdims=True)
    idx = jnp.min(jnp.where(vals == m, lane, sentinel), axis=-1, keepdims=True)
    return m, idx


def _moe_gates(logits, rbias):
    tm, ne = logits.shape
    neg = -jnp.inf
    mx = jnp.max(logits, axis=-1, keepdims=True)
    ex = jnp.exp(logits - mx)
    scores = ex / jnp.sum(ex, axis=-1, keepdims=True)
    sel = scores + rbias
    lane = lax.broadcasted_iota(jnp.int32, (tm, ne), 1)
    grp = lane // EXPERTS_PER_GROUP
    best_s = jnp.full((tm, 1), neg, F32)
    best_g = jnp.zeros((tm, 1), jnp.int32)
    for gi in range(N_GROUPS):
        mg = jnp.where(grp == gi, sel, neg)
        m1, i1 = _first_argmax(mg, lane, ne)
        m2 = jnp.max(jnp.where(lane == i1, neg, mg), axis=-1, keepdims=True)
        gs = m1 + m2
        upd = gs > best_s
        best_g = jnp.where(upd, gi, best_g)
        best_s = jnp.where(upd, gs, best_s)
    sg = jnp.where(grp == best_g, sel, neg)
    _, i1 = _first_argmax(sg, lane, ne)
    _, i2 = _first_argmax(jnp.where(lane == i1, neg, sg), lane, ne)
    chosen = (lane == i1) | (lane == i2)
    w = jnp.where(chosen, scores, 0.0)
    return w / jnp.sum(w, axis=-1, keepdims=True)


def _mix_moe_kernel(x_ref, hy_ref, hg_ref, mod_ref, w_ref, rw_ref, rb_ref, wg_ref, wu_ref, wd_ref,
                    o_ref, act_s):
    c = hy_ref.shape[-1]
    mix = _dot(hy_ref[0], w_ref[:c, :]) + _dot(hg_ref[0], w_ref[c:, :])
    m = mod_ref[0]
    x1 = x_ref[0] + m[2:3] * mix
    xn = x1 * lax.rsqrt(jnp.mean(x1 * x1, axis=-1, keepdims=True) + NORM_EPS)
    t = xn * (1.0 + m[4:5]) + m[3:4]
    t_hi = t.astype(BF16)
    t_lo = (t - t_hi.astype(F32)).astype(BF16)
    r = _dot(t_hi, rw_ref[...])
    lg = r[:, :LANES] + r[:, LANES:] + _dot(t_lo, rw_ref[:, :LANES])
    ne, _, f = wg_ref.shape
    gates = _moe_gates(lg[:, :ne], rb_ref[...])
    for e in range(ne):
        hgate = _dot(t_hi, wg_ref[e])
        hup = _dot(t_hi, wu_ref[e])
        act_s[:, e * f:(e + 1) * f] = (hgate * _sigmoid(hgate) * hup * gates[:, e:e + 1]).astype(BF16)
    o_ref[0] = x1 + m[5:6] * _dot(act_s[...], wd_ref[...])


def _mix_moe(x, hy, hg, mod_l, mod_row, w_out, router_w, rbias, wg, wu, wd, *, tm):
    bsz, L, d = x.shape
    c = hy.shape[-1]
    cg = hg.shape[-1]
    ne, _, f = wg.shape
    rw_hi = router_w.astype(BF16)
    rw_lo = (router_w - rw_hi.astype(F32)).astype(BF16)
    rw = jnp.zeros((d, 2 * LANES), BF16).at[:, :ne].set(rw_hi).at[:, LANES:LANES + ne].set(rw_lo)
    return pl.pallas_call(
        _mix_moe_kernel,
        out_shape=jax.ShapeDtypeStruct((bsz, L, d), F32),
        grid=(bsz, L // tm),
        in_specs=[
            pl.BlockSpec((1, tm, d), lambda b, i: (b, i, 0)),
            pl.BlockSpec((1, tm, c), lambda b, i: (b, i, 0)),
            pl.BlockSpec((1, tm, cg), lambda b, i: (b, i, 0)),
            pl.BlockSpec((1, N_MOD, d), lambda b, i: (mod_row(b), 0, 0)),
            _resident((c + cg, d)),
            _resident((d, 2 * LANES)),
            _resident((1, ne)),
            _resident((ne, d, f)),
            _resident((ne, d, f)),
            _resident((ne * f, d)),
        ],
        out_specs=pl.BlockSpec((1, tm, d), lambda b, i: (b, i, 0)),
        scratch_shapes=[pltpu.VMEM((tm, ne * f), BF16)],
        compiler_params=_cparams("parallel", "parallel"),
        name="mix_moe",
    )(x, hy, hg, mod_l, w_out, rw, rbias.reshape(1, ne), wg, wu, wd.reshape(ne * f, d))


def _final_norm_kernel(x_ref, w_ref, o_ref):
    x = x_ref[0]
    o_ref[0] = x * lax.rsqrt(jnp.mean(x * x, axis=-1, keepdims=True) + NORM_EPS) * w_ref[...]


def _final_norm(x, w, *, tm):
    bsz, L, d = x.shape
    return pl.pallas_call(
        _final_norm_kernel,
        out_shape=jax.ShapeDtypeStruct(x.shape, F32),
        grid=(bsz, L // tm),
        in_specs=[pl.BlockSpec((1, tm, d), lambda b, i: (b, i, 0)), pl.BlockSpec((1, d), lambda b, i: (0, 0))],
        out_specs=pl.BlockSpec((1, tm, d), lambda b, i: (b, i, 0)),
        compiler_params=_cparams("parallel", "parallel"),
        name="final_norm",
    )(x, w.reshape(1, d))


def kernel(x, c, ctx, c_ctx, w_mod, b_mod, w_in, w_out, hy_conv_w, hy_conv_b, hy_w1, hy_b1, hy_w2, hy_b2, hy_w3, hy_b3, hy_bias, hgrn_lower_bounds, hgrn_norm_w, router_w, router_bias, moe_w_gate, moe_w_up, moe_w_down, final_norm_w):
    bsz, seq, d = x.shape
    ctx_len = ctx.shape[1]
    depth = w_mod.shape[0]
    dg = hgrn_lower_bounds.shape[-1]
    nh = dg // HEAD_DIM
    assert bsz + 1 <= MOD_ROWS and seq % GRID_W == 0

    cc = jnp.zeros((MOD_ROWS, d), F32).at[:bsz].set(c).at[bsz].set(c_ctx)
    mod = _modulation(cc, w_mod, b_mod).reshape(depth, MOD_ROWS, N_MOD, d)
    lat_row = lambda b: b
    ctx_row = lambda b: bsz

    lb_soft = jax.nn.softmax(hgrn_lower_bounds.astype(F32), axis=1)
    lower = jnp.cumsum(lb_soft, axis=1) - lb_soft[:, :1]
    lower = lower.reshape(2, depth, nh, HEAD_DIM).transpose(1, 2, 0, 3)

    dft_lat = _dft_matrices(seq)
    dft_ctx = _dft_matrices(ctx_len)
    tm_lat = min(seq, 1024)
    tm_moe = min(seq, 512)

    xc = ctx
    for l in range(depth):
        last = l == depth - 1
        w_in_l = w_in[l].astype(BF16)
        w_out_l = w_out[l].astype(BF16)
        experts = (moe_w_gate[l].astype(BF16), moe_w_up[l].astype(BF16), moe_w_down[l].astype(BF16))
        u_lat, hg_lat = _inproj(x, mod[l], lat_row, w_in_l, hy_conv_w[l], hy_conv_b[l],
                                period=GRID_W, tm=tm_lat)
        u_ctx, hg_ctx = _inproj(xc, mod[l], ctx_row, w_in_l, hy_conv_w[l], hy_conv_b[l],
                                period=ctx_len, tm=ctx_len)
        o_ctx, o_lat = _hgrn(hg_ctx, hg_lat, lower[l], hgrn_norm_w[l])
        taps_w = (hy_w1[l], hy_b1[l], hy_w2[l], hy_b2[l], hy_w3[l], hy_b3[l])
        hy_lat = _hyena(u_lat, dft_lat, taps_w, hy_bias[l])
        x = _mix_moe(x, hy_lat, o_lat, mod[l], lat_row, w_out_l, router_w, router_bias, *experts, tm=tm_moe)
        if not last:
            hy_ctx = _hyena(u_ctx, dft_ctx, taps_w, hy_bias[l])
            xc = _mix_moe(xc, hy_ctx, o_ctx, mod[l], ctx_row, w_out_l, router_w, router_bias, *experts,
                          tm=ctx_len)

    return _final_norm(x, final_norm_w, tm=tm_lat)
```

```python
import functools
import math

import numpy as np
import jax
import jax.numpy as jnp
from jax import lax
from jax.experimental import pallas as pl
from jax.experimental.pallas import tpu as pltpu

F32 = jnp.float32
BF16 = jnp.bfloat16
HIGHEST = lax.Precision.HIGHEST

GRID_W = 64
NORM_EPS = 1e-6
N_MOD = 6
HY_ORDER = 2
HY_BANDS = 16
HY_TARGET = 1e-2
HY_FAST_PCT = 0.3
HY_SLOW_PCT = 1.5
HEAD_DIM = 128
N_EXPERTS = 16
N_GROUPS = 4
EXPERTS_PER_GROUP = N_EXPERTS // N_GROUPS
LANES = 128
MOD_ROWS = 16
HGRN_CHUNK = 128
HGRN_GROUP = 16
HGRN_FINE_LEVELS = (2, 4)
VMEM_LIMIT = 56 << 20


def _cparams(*sem, flags=None):
    return pltpu.CompilerParams(dimension_semantics=sem, vmem_limit_bytes=VMEM_LIMIT, flags=flags)


def _sigmoid(x):
    return 1.0 / (1.0 + jnp.exp(-x))


def _dot(a, b, **kw):
    return jnp.dot(a, b, preferred_element_type=F32, **kw)


def _resident(shape):
    return pl.BlockSpec(shape, lambda *_: (0,) * len(shape), pipeline_mode=pl.Buffered(1))


def _neg_abs(x):
    bits = lax.bitcast_convert_type(x, jnp.uint32) | jnp.uint32(0x80000000)
    return lax.bitcast_convert_type(bits, F32)


def _dot_nt(a, b):
    return lax.dot_general(a, b, (((1,), (1,)), ((), ())), preferred_element_type=F32)


def _mod_kernel(c_ref, w_ref, b_ref, o_ref):
    c = c_ref[...]
    cs = c * _sigmoid(c)
    o_ref[0] = _dot(cs, w_ref[0], precision=HIGHEST) + b_ref[0]


def _modulation(cc, w_mod, b_mod):
    depth, d, n = w_mod.shape
    tn = n // 3
    return pl.pallas_call(
        _mod_kernel,
        out_shape=jax.ShapeDtypeStruct((depth, MOD_ROWS, n), F32),
        grid=(depth, n // tn),
        in_specs=[
            pl.BlockSpec((MOD_ROWS, d), lambda l, j: (0, 0)),
            pl.BlockSpec((1, d, tn), lambda l, j: (l, 0, j)),
            pl.BlockSpec((1, 1, tn), lambda l, j: (l, 0, j)),
        ],
        out_specs=pl.BlockSpec((1, MOD_ROWS, tn), lambda l, j: (l, 0, j)),
        compiler_params=_cparams("parallel", "parallel"),
        name="modulation",
    )(cc, w_mod, b_mod.reshape(depth, 1, n))


def _dft_kernel(ac_ref, as_ref, bc_ref, bs_ref, c_ref, s_ref, *, n1):
    bc = bc_ref[...]
    bs = bs_ref[...]
    ac = ac_ref[...]
    as_ = as_ref[...]
    for j in range(n1):
        a_c = ac[:, j:j + 1]
        a_s = as_[:, j:j + 1]
        c_ref[:, j * LANES:(j + 1) * LANES] = (a_c * bc - a_s * bs).astype(BF16)
        s_ref[:, j * LANES:(j + 1) * LANES] = (a_s * bc + a_c * bs).astype(BF16)


def _dft_tables(L):
    n1 = L // LANES
    period = 4 * L
    r = np.arange(L, dtype=np.int64)[:, None]
    c1 = np.arange(n1, dtype=np.int64)[None, :]
    c0 = np.arange(LANES, dtype=np.int64)[None, :]

    def cs(phase):
        ang = (phase % period).astype(np.float64) * (2.0 * np.pi / period)
        return np.cos(ang).astype(np.float32), np.sin(ang).astype(np.float32)

    fa = cs((2 * r + 1) * (LANES * c1))
    fb = cs((2 * r + 1) * c0)
    ta = cs((2 * LANES * c1) * r)
    tb = cs((2 * c0 + 1) * r)
    return (fa, fb), (ta, tb)


def _dft_matrices(L):
    n1 = L // LANES
    tr = min(L, 256)
    outs = []
    for (ac, as_), (bc, bs) in _dft_tables(L):
        c, s = pl.pallas_call(
            functools.partial(_dft_kernel, n1=n1),
            out_shape=(jax.ShapeDtypeStruct((L, L), BF16), jax.ShapeDtypeStruct((L, L), BF16)),
            grid=(L // tr,),
            in_specs=[
                pl.BlockSpec((tr, n1), lambda i: (i, 0)),
                pl.BlockSpec((tr, n1), lambda i: (i, 0)),
                pl.BlockSpec((tr, LANES), lambda i: (i, 0)),
                pl.BlockSpec((tr, LANES), lambda i: (i, 0)),
            ],
            out_specs=(pl.BlockSpec((tr, L), lambda i: (i, 0)), pl.BlockSpec((tr, L), lambda i: (i, 0))),
            compiler_params=_cparams("parallel"),
            name="dft_tables",
        )(jnp.asarray(ac), jnp.asarray(as_), jnp.asarray(bc), jnp.asarray(bs))
        outs.append((c, s))
    return outs


def _inproj_kernel(x_ref, mod_ref, w_ref, cw_ref, cb_ref, u_ref, hg_ref, *, period, n_hy, tn):
    x = x_ref[0]
    tm = x.shape[0]
    xn = x * lax.rsqrt(jnp.mean(x * x, axis=-1, keepdims=True) + NORM_EPS)
    m = mod_ref[0]
    a = (xn * (1.0 + m[1:2]) + m[0:1]).astype(BF16)
    pos = lax.broadcasted_iota(jnp.int32, (tm, 1), 0) % period
    first = pos == 0
    last = pos == period - 1
    n_total = w_ref.shape[1]
    for j in range(n_total // tn):
        p = _dot(a, w_ref[:, j * tn:(j + 1) * tn])
        if j * tn < n_hy:
            cw = cw_ref[:, j * tn:(j + 1) * tn]
            prev = jnp.where(first, 0.0, pltpu.roll(p, 1, 0))
            nxt = jnp.where(last, 0.0, pltpu.roll(p, tm - 1, 0))
            u_ref[0, :, j * tn:(j + 1) * tn] = (
                prev * cw[0:1] + p * cw[1:2] + nxt * cw[2:3] + cb_ref[:, j * tn:(j + 1) * tn]
            ).astype(u_ref.dtype)
        else:
            hg_ref[0, :, j * tn - n_hy:(j + 1) * tn - n_hy] = p.astype(hg_ref.dtype)


def _inproj(x, mod_l, mod_row, w_in, conv_w, conv_b, *, period, tm):
    bsz, L, d = x.shape
    n_hy = conv_w.shape[1]
    n_all = w_in.shape[1]
    tn = n_hy // 3
    assert tm % period == 0 or period == L == tm
    return pl.pallas_call(
        functools.partial(_inproj_kernel, period=period, n_hy=n_hy, tn=tn),
        out_shape=(jax.ShapeDtypeStruct((bsz, L, n_hy), BF16),
                   jax.ShapeDtypeStruct((bsz, L, n_all - n_hy), BF16)),
        grid=(bsz, L // tm),
        in_specs=[
            pl.BlockSpec((1, tm, d), lambda b, i: (b, i, 0)),
            pl.BlockSpec((1, N_MOD, d), lambda b, i: (mod_row(b), 0, 0)),
            _resident((d, n_all)),
            _resident((3, n_hy)),
            _resident((1, n_hy)),
        ],
        out_specs=(pl.BlockSpec((1, tm, n_hy), lambda b, i: (b, i, 0)),
                   pl.BlockSpec((1, tm, n_all - n_hy), lambda b, i: (b, i, 0))),
        compiler_params=_cparams("parallel", "parallel"),
        name="inproj",
    )(x, mod_l, w_in, conv_w, conv_b.reshape(1, n_hy))


def _filter_kernel(z_ref, w1_ref, b1_ref, w2_ref, b2_ref, w3f_ref, b3f_ref, w3b_ref, b3b_ref,
                   dl_ref, o_ref, h_s):
    z = z_ref[...]

    @pl.when((pl.program_id(0) == 0) & (pl.program_id(1) == 0))
    def _():
        h1 = jnp.sin(_dot(z, w1_ref[...], precision=HIGHEST) + b1_ref[...])
        h_s[...] = jnp.sin(_dot(h1, w2_ref[...], precision=HIGHEST) + b2_ref[...])

    h = h_s[...]
    hf = _dot(h, w3f_ref[...], precision=HIGHEST) + b3f_ref[...]
    hb = _dot(h, w3b_ref[...], precision=HIGHEST) + b3b_ref[...]
    win = jnp.exp(-z[:, 0:1] * dl_ref[...])
    hf = hf * win
    hb = hb * win
    nrm = (jnp.sum(jnp.abs(hf), axis=0, keepdims=True)
           + jnp.sum(jnp.abs(hb), axis=0, keepdims=True))
    inv = 1.0 / nrm
    hf = hf * inv
    row = lax.broadcasted_iota(jnp.int32, (z.shape[0], 1), 0)
    hb0 = jnp.where(row == 0, 0.0, hb * inv)
    o_ref[0, 0] = hf + hb0
    o_ref[0, 1] = hf - hb0


def _hyena_filter_taps(L, w1, b1, w2, b2, w3, b3):
    nfeat, hid = w1.shape
    c = w3.shape[1] // (2 * HY_ORDER)
    tc = min(c, 256)
    nct = c // tc
    t = jnp.linspace(0.0, 1.0, L, dtype=F32)
    n = jnp.arange(L, dtype=F32)
    freqs = jnp.linspace(1e-4, HY_BANDS - 1, HY_BANDS, dtype=F32)
    ang = (2.0 * math.pi / L) * n[:, None] * freqs[None, :]
    z = jnp.concatenate([t[:, None], jnp.cos(ang), -jnp.sin(ang)], axis=-1)
    z = jnp.pad(z, ((0, 0), (0, LANES - nfeat)))
    w1p = jnp.pad(w1, ((0, LANES - nfeat), (0, 0)))
    deltas = jnp.abs(jnp.linspace(math.log(HY_TARGET) / HY_FAST_PCT, math.log(HY_TARGET) / HY_SLOW_PCT,
                                  c, dtype=F32)).reshape(1, c)
    full = lambda shape: pl.BlockSpec(shape, lambda o, j: (0,) * len(shape))
    return pl.pallas_call(
        _filter_kernel,
        out_shape=jax.ShapeDtypeStruct((HY_ORDER, 2, L, c), F32),
        grid=(HY_ORDER, nct),
        in_specs=[
            full((L, LANES)), full((LANES, hid)), full((1, hid)), full((hid, hid)), full((1, hid)),
            pl.BlockSpec((hid, tc), lambda o, j: (0, o * 2 * nct + j)),
            pl.BlockSpec((1, tc), lambda o, j: (0, o * 2 * nct + j)),
            pl.BlockSpec((hid, tc), lambda o, j: (0, o * 2 * nct + nct + j)),
            pl.BlockSpec((1, tc), lambda o, j: (0, o * 2 * nct + nct + j)),
            pl.BlockSpec((1, tc), lambda o, j: (0, j)),
        ],
        out_specs=pl.BlockSpec((1, 2, L, tc), lambda o, j: (o, 0, 0, j)),
        scratch_shapes=[pltpu.VMEM((L, hid), F32)],
        compiler_params=_cparams("arbitrary", "arbitrary"),
        name="hyena_filter",
    )(z, w1p, b1.reshape(1, hid), w2, b2.reshape(1, hid), w3, b3.reshape(1, -1), w3, b3.reshape(1, -1),
      deltas)


def _kspec_kernel(cf_ref, sf_ref, h_ref, o_ref, *, scale):
    o_ref[0, 0] = _dot(cf_ref[...], h_ref[0, 0].astype(BF16)) * scale
    o_ref[0, 1] = _dot(sf_ref[...], h_ref[0, 1].astype(BF16)) * scale


def _filter_spectrum(cft, sft, taps):
    _, _, L, c = taps.shape
    tk = min(L, 512)
    return pl.pallas_call(
        functools.partial(_kspec_kernel, scale=1.0 / L),
        out_shape=jax.ShapeDtypeStruct((HY_ORDER, 2, L, c), F32),
        grid=(HY_ORDER, L // tk),
        in_specs=[
            pl.BlockSpec((tk, L), lambda o, i: (i, 0)),
            pl.BlockSpec((tk, L), lambda o, i: (i, 0)),
            pl.BlockSpec((1, 2, L, c), lambda o, i: (o, 0, 0, 0)),
        ],
        out_specs=pl.BlockSpec((1, 2, tk, c), lambda o, i: (o, 0, i, 0)),
        compiler_params=_cparams("parallel", "parallel"),
        name="hyena_filter_spectrum",
    )(cft, sft, taps)


def _hy_fwd_kernel(cf_ref, sf_ref, v_ref, k_ref, p_ref, q_ref):
    v = v_ref[0].astype(BF16)
    a = _dot(cf_ref[...], v)
    b = _dot(sf_ref[...], v)
    kr = k_ref[0, 0]
    ks = k_ref[0, 1]
    p_ref[0] = (a * kr - b * ks).astype(BF16)
    q_ref[0] = (a * ks + b * kr).astype(BF16)


def _hy_forward(cft, sft, src, src_col, kspec, order):
    bsz, L, _ = src.shape
    c = kspec.shape[-1]
    tk = min(L, 1024)
    return pl.pallas_call(
        _hy_fwd_kernel,
        out_shape=(jax.ShapeDtypeStruct((bsz, L, c), BF16), jax.ShapeDtypeStruct((bsz, L, c), BF16)),
        grid=(L // tk, bsz),
        in_specs=[
            pl.BlockSpec((tk, L), lambda i, b: (i, 0)),
            pl.BlockSpec((tk, L), lambda i, b: (i, 0)),
            pl.BlockSpec((1, L, c), lambda i, b: (b, 0, src_col)),
            pl.BlockSpec((1, 2, tk, c), lambda i, b: (order, 0, i, 0)),
        ],
        out_specs=(pl.BlockSpec((1, tk, c), lambda i, b: (b, i, 0)),
                   pl.BlockSpec((1, tk, c), lambda i, b: (b, i, 0))),
        compiler_params=_cparams("parallel", "parallel"),
        name="hyena_spectrum",
    )(cft, sft, src, kspec)


def _hy_inv_kernel(cf_ref, sf_ref, p_ref, q_ref, gate_ref, src_ref, d_ref, o_ref):
    conv = _dot(cf_ref[...], p_ref[0]) + _dot(sf_ref[...], q_ref[0])
    o_ref[0] = (gate_ref[0] * (conv + src_ref[0] * d_ref[0])).astype(o_ref.dtype)


def _hy_inverse(cf, sf, p, q, gate, gate_col, src, src_col, d, order, out_dtype):
    bsz, L, c = p.shape
    tm = min(L, 1024)
    return pl.pallas_call(
        _hy_inv_kernel,
        out_shape=jax.ShapeDtypeStruct((bsz, L, c), out_dtype),
        grid=(L // tm, bsz),
        in_specs=[
            pl.BlockSpec((tm, L), lambda i, b: (i, 0)),
            pl.BlockSpec((tm, L), lambda i, b: (i, 0)),
            pl.BlockSpec((1, L, c), lambda i, b: (b, 0, 0)),
            pl.BlockSpec((1, L, c), lambda i, b: (b, 0, 0)),
            pl.BlockSpec((1, tm, c), lambda i, b: (b, i, gate_col)),
            pl.BlockSpec((1, tm, c), lambda i, b: (b, i, src_col)),
            pl.BlockSpec((1, 1, c), lambda i, b: (order, 0, 0)),
        ],
        out_specs=pl.BlockSpec((1, tm, c), lambda i, b: (b, i, 0)),
        compiler_params=_cparams("parallel", "parallel"),
        name="hyena_inverse",
    )(cf, sf, p, q, gate, src, d.reshape(HY_ORDER, 1, c))


def _hyena(u, dft, taps_w, d):
    (cft, sft), (cf, sf) = dft
    L = u.shape[1]
    taps = _hyena_filter_taps(L, *taps_w)
    kspec = _filter_spectrum(cft, sft, taps)
    p, q = _hy_forward(cft, sft, u, 0, kspec, 0)
    z = _hy_inverse(cf, sf, p, q, u, 1, u, 0, d, 0, BF16)
    p, q = _hy_forward(cft, sft, z, 0, kspec, 1)
    return _hy_inverse(cf, sf, p, q, u, 2, z, 0, d, 1, BF16)


def _anchor_rows(b, n, a):
    assert n % 8 == 0
    parts = [jnp.broadcast_to(b[s + a:s + a + 1, :], (n, b.shape[1])) for s in range(0, b.shape[0], n)]
    return parts[0] if len(parts) == 1 else jnp.concatenate(parts, axis=0)


def _hgrn_group(q_raw, v, f_logit, lb, st_ref, lv, tri, *, reverse):
    C = HGRN_CHUNK
    G = q_raw.shape[0] // C
    rows = lambda x, i: x[i * C:(i + 1) * C]
    q = q_raw * _sigmoid(q_raw)
    f = lb + (1.0 - lb) * _sigmoid(f_logit)
    kk = 1.0 - f
    g = jnp.log2(f)
    g_top = lax.bitcast_convert_type(
        lax.bitcast_convert_type(g, jnp.uint32) & jnp.uint32(0xFFFF0000), F32)
    gg = jnp.concatenate([g_top.astype(BF16), (g - g_top).astype(BF16)], axis=1)
    bb = [_dot(tri, rows(gg, i)) for i in range(G)]
    part = lambda k: jnp.concatenate(
        [x[k * C:(k + 1) * C, :HEAD_DIM] + x[k * C:(k + 1) * C, HEAD_DIM:] for x in bb], axis=0)
    b = part(0)
    fine = {n_: part(k + 1) for k, n_ in enumerate(HGRN_FINE_LEVELS)}

    qb = q.astype(BF16)
    kb = kk.astype(BF16)
    scores = [jnp.zeros((C, C), F32)] * G
    n = 2
    level = 1
    while n <= C:
        a = n // 2 if reverse else n // 2 - 1
        d = fine[n] if n in fine else b - _anchor_rows(b, n, a)
        e = jnp.exp2(_neg_abs(d)).astype(BF16)
        qe = qb * e
        ke = kb * e
        scores = [jnp.where(lv == level, _dot_nt(rows(qe, i), rows(ke, i)), scores[i]) for i in range(G)]
        n *= 2
        level += 1

    vb = v.astype(BF16)
    diag = jnp.sum(q * kk, axis=-1, keepdims=True) * v
    o = [_dot(scores[i].astype(BF16), rows(vb, i)) + rows(diag, i) for i in range(G)]
    b_end = _anchor_rows(b, C, 0 if reverse else C - 1)
    qd = qb * jnp.exp2(b).astype(BF16)
    kd = kb * jnp.exp2(b_end - b).astype(BF16)
    dec = jnp.exp2(b_end)
    kv = [_dot(rows(v, i).T.astype(BF16), rows(kd, i)) for i in range(G)]
    st = st_ref[...]
    for i in (range(G - 1, -1, -1) if reverse else range(G)):
        o[i] = o[i] + _dot_nt(rows(qd, i), st.astype(BF16))
        st = st * dec[i * C:i * C + 1] + kv[i]
    st_ref[...] = st
    return jnp.concatenate(o, axis=0)


def _hgrn_kernel(qc, ic, gc, fc, bc, ql, il, gl, fl, bl, lb_ref, nw_ref, oc_ref, ol_ref,
                 of_s, ob_s, stf, stb, *, n_ctx, n_lat):
    C = HGRN_CHUNK
    row = lax.broadcasted_iota(jnp.int32, (C, C), 0)
    col = lax.broadcasted_iota(jnp.int32, (C, C), 1)
    x = row ^ col
    lvl = jnp.zeros((C, C), jnp.int32)
    n = 1
    while n < C:
        lvl = lvl + (x >= n).astype(jnp.int32)
        n *= 2
    lv_f = jnp.where(row > col, lvl, 0)
    lv_b = jnp.where(row < col, lvl, 0)

    def cum_matrix(reverse):
        cum = lambda r: (col >= r) if reverse else (col <= r)
        mats = [cum(row).astype(F32)]
        for n_ in HGRN_FINE_LEVELS:
            anchor = row - row % n_ + (n_ // 2 if reverse else n_ // 2 - 1)
            mats.append(mats[0] - cum(anchor).astype(F32))
        return jnp.concatenate(mats, axis=0).astype(BF16)

    tri_f = cum_matrix(False)
    tri_b = cum_matrix(True)
    lb_f = lb_ref[0, 0:1, :]
    lb_b = lb_ref[0, 1:2, :]
    stf[...] = jnp.zeros_like(stf)
    stb[...] = jnp.zeros_like(stb)

    def run(q_ref, i_ref, f_ref, b_ref, n_chunks, base):
        G = min(n_chunks, HGRN_GROUP)
        R = G * C
        n_groups = n_chunks // G

        def body(j, carry):
            off_f = pl.multiple_of(j * R, R)
            off_b = pl.multiple_of((n_groups - 1 - j) * R, R)
            rf = pl.ds(off_f, R)
            rb = pl.ds(off_b, R)
            ld = lambda ref, r: ref[0, r, :].astype(F32)
            of_s[pl.ds(base + off_f, R), :] = _hgrn_group(
                ld(q_ref, rf), ld(i_ref, rf), ld(f_ref, rf), lb_f, stf, lv_f, tri_f, reverse=False)
            ob_s[pl.ds(base + off_b, R), :] = _hgrn_group(
                ld(q_ref, rb), ld(i_ref, rb), ld(b_ref, rb), lb_b, stb, lv_b, tri_b, reverse=True)
            return carry
        lax.fori_loop(0, n_groups, body, 0)

    run(qc, ic, fc, bc, n_ctx, 0)
    run(ql, il, fl, bl, n_lat, n_ctx * C)

    nw = nw_ref[...]

    def finish(g_ref, o_ref, n_chunks, base):
        R = C * math.gcd(n_chunks, 4)

        def body(j, carry):
            r = pl.ds(pl.multiple_of(j * R, R), R)
            rs = pl.ds(pl.multiple_of(base + j * R, C), R)
            o = of_s[rs, :] + ob_s[rs, :]
            o = o * lax.rsqrt(jnp.mean(o * o, axis=-1, keepdims=True) + NORM_EPS) * nw
            g = g_ref[0, r, :].astype(F32)
            o_ref[0, r, :] = (o * (g * _sigmoid(g))).astype(o_ref.dtype)
            return carry
        lax.fori_loop(0, n_chunks * C // R, body, 0)

    finish(gc, oc_ref, n_ctx, 0)
    finish(gl, ol_ref, n_lat, n_ctx * C)


def _hgrn(hg_ctx, hg_lat, lb, norm_w):
    bsz, lc, n5 = hg_ctx.shape
    ll = hg_lat.shape[1]
    dg = n5 // 5
    nh = dg // HEAD_DIM
    C = HGRN_CHUNK
    assert lc % C == 0 and ll % C == 0
    assert all((n // C) % min(n // C, HGRN_GROUP) == 0 for n in (lc, ll))

    def slab(L, part):
        return pl.BlockSpec((1, L, HEAD_DIM), lambda b, h: (b, 0, part * nh + h))

    return pl.pallas_call(
        functools.partial(_hgrn_kernel, n_ctx=lc // C, n_lat=ll // C),
        out_shape=(jax.ShapeDtypeStruct((bsz, lc, dg), BF16), jax.ShapeDtypeStruct((bsz, ll, dg), BF16)),
        grid=(bsz, nh),
        in_specs=[slab(lc, p) for p in range(5)] + [slab(ll, p) for p in range(5)] + [
            pl.BlockSpec((1, 2, HEAD_DIM), lambda b, h: (h, 0, 0)),
            pl.BlockSpec((1, HEAD_DIM), lambda b, h: (0, 0)),
        ],
        out_specs=(pl.BlockSpec((1, lc, HEAD_DIM), lambda b, h: (b, 0, h)),
                   pl.BlockSpec((1, ll, HEAD_DIM), lambda b, h: (b, 0, h))),
        scratch_shapes=[
            pltpu.VMEM((lc + ll, HEAD_DIM), F32),
            pltpu.VMEM((lc + ll, HEAD_DIM), F32),
            pltpu.VMEM((HEAD_DIM, HEAD_DIM), F32),
            pltpu.VMEM((HEAD_DIM, HEAD_DIM), F32),
        ],
        compiler_params=_cparams("parallel", "parallel"),
        name="hgrn2",
    )(*([hg_ctx] * 5), *([hg_lat] * 5), lb, norm_w.reshape(1, HEAD_DIM))


def _first_argmax(vals, idx, sentinel):
    m = jnp.max(vals, axis=0, keepdims=True)
    first = jnp.min(jnp.where(vals == m, idx, sentinel), axis=0, keepdims=True)
    return m, first


def _moe_gates(logits_t, rbias):
    ne, tm = logits_t.shape
    neg = -jnp.inf
    mx = jnp.max(logits_t, axis=0, keepdims=True)
    ex = jnp.exp(logits_t - mx)
    scores = ex / jnp.sum(ex, axis=0, keepdims=True)
    sel = scores + rbias
    eidx = lax.broadcasted_iota(jnp.int32, (ne, tm), 0)
    grp = eidx // EXPERTS_PER_GROUP
    best_s = jnp.full((1, tm), neg, F32)
    best_g = jnp.zeros((1, tm), jnp.int32)
    for gi in range(N_GROUPS):
        mg = jnp.where(grp == gi, sel, neg)
        m1, i1 = _first_argmax(mg, eidx, ne)
        m2 = jnp.max(jnp.where(eidx == i1, neg, mg), axis=0, keepdims=True)
        gs = m1 + m2
        upd = gs > best_s
        best_g = jnp.where(upd, gi, best_g)
        best_s = jnp.where(upd, gs, best_s)
    sg = jnp.where(grp == best_g, sel, neg)
    _, i1 = _first_argmax(sg, eidx, ne)
    _, i2 = _first_argmax(jnp.where(eidx == i1, neg, sg), eidx, ne)
    chosen = (eidx == i1) | (eidx == i2)
    w = jnp.where(chosen, scores, 0.0)
    return w / jnp.sum(w, axis=0, keepdims=True), best_g


def _mix_moe_kernel(x_ref, hy_ref, hg_ref, mod_ref, w_ref, rw_ref, rb_ref, wg_ref, wu_ref, wd_ref,
                    o_ref, t_s, g4_s, *, cap):
    c = hy_ref.shape[-1]
    mix = _dot(hy_ref[0], w_ref[:c, :]) + _dot(hg_ref[0], w_ref[c:, :])
    m = mod_ref[0]
    x1 = x_ref[0] + m[2:3] * mix
    o_ref[0] = x1
    xn = x1 * lax.rsqrt(jnp.mean(x1 * x1, axis=-1, keepdims=True) + NORM_EPS)
    t = xn * (1.0 + m[4:5]) + m[3:4]
    t_hi = t.astype(BF16)
    t_s[...] = t_hi
    t_lo = (t - t_hi.astype(F32)).astype(BF16)
    r = _dot(t_hi, rw_ref[...])
    lg = r[:, :LANES] + r[:, LANES:] + _dot(t_lo, rw_ref[:, :LANES])
    ne, _, f = wg_ref.shape
    tm = t.shape[0]
    per_group = ne // N_GROUPS
    gates_t, best_g = _moe_gates(lg.T[:ne], rb_ref[...])

    row8 = lax.broadcasted_iota(jnp.int32, (8, tm), 0)
    member = row8 == best_g
    ri = lax.broadcasted_iota(jnp.int32, (tm, tm), 0)
    ci = lax.broadcasted_iota(jnp.int32, (tm, tm), 1)
    before = _dot(member.astype(BF16), (ri < ci).astype(BF16))
    rank_row = jnp.sum(jnp.where(member, before, 0.0), axis=0, keepdims=True)
    grp_row = best_g.astype(F32)
    count = jnp.sum(member.astype(F32), axis=1, keepdims=True)
    g4_t = gates_t
    for g in range(1, N_GROUPS):
        g4_t = g4_t + pltpu.roll(gates_t, g * per_group, 0)
    row_e = lax.broadcasted_iota(jnp.int32, (ne, tm), 0)
    info_t = jnp.where(row_e < per_group, g4_t,
                       jnp.where(row_e == per_group, rank_row, jnp.where(row_e == per_group + 1, grp_row, 0.0)))
    info = jnp.concatenate([info_t, jnp.zeros((LANES - ne, tm), F32)], axis=0).T
    g4_s[...] = info.astype(BF16)
    rank_col = info[:, per_group:per_group + 1]
    grp_col = info[:, per_group + 1:per_group + 2]
    g2 = m[5:6]

    slot_col = lax.broadcasted_iota(jnp.int32, (cap, 1), 0).astype(F32)
    slot_row = lax.broadcasted_iota(jnp.int32, (1, cap), 1).astype(F32)
    for g in range(N_GROUPS):
        n_g = count[g, 0].astype(jnp.int32)

        def body(s, carry, g=g):
            base = (s * cap).astype(F32)
            pick = ((rank_row - base == slot_col) & (grp_row == float(g))).astype(BF16)
            xg = _dot(pick, t_s[...]).astype(BF16)
            gs = _dot(pick, g4_s[...])
            acts = []
            for j in range(per_group):
                e = g * per_group + j
                hgate = _dot(xg, wg_ref[e])
                hup = _dot(xg, wu_ref[e])
                acts.append((hgate * _sigmoid(hgate) * hup * gs[:, j:j + 1]).astype(BF16))
            y = _dot(jnp.concatenate(acts, axis=1), wd_ref[g * per_group * f:(g + 1) * per_group * f, :])
            put = ((rank_col - base == slot_row) & (grp_col == float(g))).astype(BF16)
            o_ref[0] += _dot(put, (y * g2).astype(BF16))
            return carry

        lax.fori_loop(0, (n_g + cap - 1) // cap, body, 0)


def _mix_moe(x, hy, hg, mod_l, mod_row, w_out, router_w, rbias, wg, wu, wd, *, tm):
    bsz, L, d = x.shape
    c = hy.shape[-1]
    cg = hg.shape[-1]
    ne, _, f = wg.shape
    rw_hi = router_w.astype(BF16)
    rw_lo = (router_w - rw_hi.astype(F32)).astype(BF16)
    rw = jnp.zeros((d, 2 * LANES), BF16).at[:, :ne].set(rw_hi).at[:, LANES:LANES + ne].set(rw_lo)
    cap = 3 * tm // 8
    assert cap % 16 == 0
    return pl.pallas_call(
        functools.partial(_mix_moe_kernel, cap=cap),
        out_shape=jax.ShapeDtypeStruct((bsz, L, d), F32),
        grid=(bsz, L // tm),
        in_specs=[
            pl.BlockSpec((1, tm, d), lambda b, i: (b, i, 0)),
            pl.BlockSpec((1, tm, c), lambda b, i: (b, i, 0)),
            pl.BlockSpec((1, tm, cg), lambda b, i: (b, i, 0)),
            pl.BlockSpec((1, N_MOD, d), lambda b, i: (mod_row(b), 0, 0)),
            _resident((c + cg, d)),
            _resident((d, 2 * LANES)),
            _resident((ne, 1)),
            _resident((ne, d, f)),
            _resident((ne, d, f)),
            _resident((ne * f, d)),
        ],
        out_specs=pl.BlockSpec((1, tm, d), lambda b, i: (b, i, 0)),
        scratch_shapes=[pltpu.VMEM((tm, d), BF16), pltpu.VMEM((tm, LANES), BF16)],
        compiler_params=_cparams("parallel", "parallel"),
        name="mix_moe",
    )(x, hy, hg, mod_l, w_out, rw, rbias.reshape(ne, 1), wg, wu, wd.reshape(ne * f, d))


def _final_norm_kernel(x_ref, w_ref, o_ref):
    x = x_ref[0]
    o_ref[0] = x * lax.rsqrt(jnp.mean(x * x, axis=-1, keepdims=True) + NORM_EPS) * w_ref[...]


def _final_norm(x, w, *, tm):
    bsz, L, d = x.shape
    return pl.pallas_call(
        _final_norm_kernel,
        out_shape=jax.ShapeDtypeStruct(x.shape, F32),
        grid=(bsz, L // tm),
        in_specs=[pl.BlockSpec((1, tm, d), lambda b, i: (b, i, 0)), pl.BlockSpec((1, d), lambda b, i: (0, 0))],
        out_specs=pl.BlockSpec((1, tm, d), lambda b, i: (b, i, 0)),
        compiler_params=_cparams("parallel", "parallel"),
        name="final_norm",
    )(x, w.reshape(1, d))


def kernel(x, c, ctx, c_ctx, w_mod, b_mod, w_in, w_out, hy_conv_w, hy_conv_b, hy_w1, hy_b1, hy_w2, hy_b2, hy_w3, hy_b3, hy_bias, hgrn_lower_bounds, hgrn_norm_w, router_w, router_bias, moe_w_gate, moe_w_up, moe_w_down, final_norm_w):
    bsz, seq, d = x.shape
    ctx_len = ctx.shape[1]
    depth = w_mod.shape[0]
    dg = hgrn_lower_bounds.shape[-1]
    nh = dg // HEAD_DIM
    assert bsz + 1 <= MOD_ROWS and seq % GRID_W == 0

    cc = jnp.zeros((MOD_ROWS, d), F32).at[:bsz].set(c).at[bsz].set(c_ctx)
    mod = _modulation(cc, w_mod, b_mod).reshape(depth, MOD_ROWS, N_MOD, d)
    lat_row = lambda b: b
    ctx_row = lambda b: bsz

    lb_soft = jax.nn.softmax(hgrn_lower_bounds.astype(F32), axis=1)
    lower = jnp.cumsum(lb_soft, axis=1) - lb_soft[:, :1]
    lower = lower.reshape(2, depth, nh, HEAD_DIM).transpose(1, 2, 0, 3)

    dft_lat = _dft_matrices(seq)
    dft_ctx = _dft_matrices(ctx_len)
    tm_lat = min(seq, 1024)
    tm_moe = min(seq, 512)

    xc = ctx
    for l in range(depth):
        last = l == depth - 1
        w_in_l = w_in[l].astype(BF16)
        w_out_l = w_out[l].astype(BF16)
        experts = (moe_w_gate[l].astype(BF16), moe_w_up[l].astype(BF16), moe_w_down[l].astype(BF16))
        u_lat, hg_lat = _inproj(x, mod[l], lat_row, w_in_l, hy_conv_w[l], hy_conv_b[l],
                                period=GRID_W, tm=tm_lat)
        u_ctx, hg_ctx = _inproj(xc, mod[l], ctx_row, w_in_l, hy_conv_w[l], hy_conv_b[l],
                                period=ctx_len, tm=ctx_len)
        o_ctx, o_lat = _hgrn(hg_ctx, hg_lat, lower[l], hgrn_norm_w[l])
        taps_w = (hy_w1[l], hy_b1[l], hy_w2[l], hy_b2[l], hy_w3[l], hy_b3[l])
        hy_lat = _hyena(u_lat, dft_lat, taps_w, hy_bias[l])
        x = _mix_moe(x, hy_lat, o_lat, mod[l], lat_row, w_out_l, router_w, router_bias, *experts, tm=tm_moe)
        if not last:
            hy_ctx = _hyena(u_ctx, dft_ctx, taps_w, hy_bias[l])
            xc = _mix_moe(xc, hy_ctx, o_ctx, mod[l], ctx_row, w_out_l, router_w, router_bias, *experts,
                          tm=ctx_len)

    return _final_norm(x, final_norm_w, tm=tm_lat)
```

```python
import functools
import math

import numpy as np
import jax
import jax.numpy as jnp
from jax import lax
from jax.experimental import pallas as pl
from jax.experimental.pallas import tpu as pltpu

F32 = jnp.float32
BF16 = jnp.bfloat16
HIGHEST = lax.Precision.HIGHEST

GRID_W = 64
NORM_EPS = 1e-6
N_MOD = 6
HY_ORDER = 2
HY_BANDS = 16
HY_TARGET = 1e-2
HY_FAST_PCT = 0.3
HY_SLOW_PCT = 1.5
HEAD_DIM = 128
N_EXPERTS = 16
N_GROUPS = 4
EXPERTS_PER_GROUP = N_EXPERTS // N_GROUPS
LANES = 128
MOD_ROWS = 16
HGRN_CHUNK = 128
HGRN_GROUP = 16
HGRN_FINE_LEVELS = (2, 4)
VMEM_LIMIT = 56 << 20


def _cparams(*sem, flags=None):
    return pltpu.CompilerParams(dimension_semantics=sem, vmem_limit_bytes=VMEM_LIMIT, flags=flags)


def _sigmoid(x):
    return 1.0 / (1.0 + jnp.exp(-x))


def _dot(a, b, **kw):
    return jnp.dot(a, b, preferred_element_type=F32, **kw)


def _resident(shape, layer=None):
    if layer is None:
        return pl.BlockSpec(shape, lambda *_: (0,) * len(shape), pipeline_mode=pl.Buffered(1))
    return pl.BlockSpec((None,) + tuple(shape), lambda *_: (layer,) + (0,) * len(shape),
                        pipeline_mode=pl.Buffered(1))


def _neg_abs(x):
    bits = lax.bitcast_convert_type(x, jnp.uint32) | jnp.uint32(0x80000000)
    return lax.bitcast_convert_type(bits, F32)


def _dot_nt(a, b):
    return lax.dot_general(a, b, (((1,), (1,)), ((), ())), preferred_element_type=F32)


def _mod_kernel(c_ref, w_ref, b_ref, o_ref):
    c = c_ref[...]
    cs = c * _sigmoid(c)
    o_ref[0] = _dot(cs, w_ref[0], precision=HIGHEST) + b_ref[0]


def _modulation(cc, w_mod, b_mod):
    depth, d, n = w_mod.shape
    tn = n // 3
    return pl.pallas_call(
        _mod_kernel,
        out_shape=jax.ShapeDtypeStruct((depth, MOD_ROWS, n), F32),
        grid=(depth, n // tn),
        in_specs=[
            pl.BlockSpec((MOD_ROWS, d), lambda l, j: (0, 0)),
            pl.BlockSpec((1, d, tn), lambda l, j: (l, 0, j)),
            pl.BlockSpec((1, 1, tn), lambda l, j: (l, 0, j)),
        ],
        out_specs=pl.BlockSpec((1, MOD_ROWS, tn), lambda l, j: (l, 0, j)),
        compiler_params=_cparams("parallel", "parallel"),
        name="modulation",
    )(cc, w_mod, b_mod.reshape(depth, 1, n))


def _dft_kernel(ac_ref, as_ref, bc_ref, bs_ref, c_ref, s_ref, *, n1):
    bc = bc_ref[...]
    bs = bs_ref[...]
    ac = ac_ref[...]
    as_ = as_ref[...]
    for j in range(n1):
        a_c = ac[:, j:j + 1]
        a_s = as_[:, j:j + 1]
        c_ref[:, j * LANES:(j + 1) * LANES] = (a_c * bc - a_s * bs).astype(BF16)
        s_ref[:, j * LANES:(j + 1) * LANES] = (a_s * bc + a_c * bs).astype(BF16)


def _dft_tables(L):
    n1 = L // LANES
    period = 4 * L
    r = np.arange(L, dtype=np.int64)[:, None]
    c1 = np.arange(n1, dtype=np.int64)[None, :]
    c0 = np.arange(LANES, dtype=np.int64)[None, :]

    def cs(phase):
        ang = (phase % period).astype(np.float64) * (2.0 * np.pi / period)
        return np.cos(ang).astype(np.float32), np.sin(ang).astype(np.float32)

    fa = cs((2 * r + 1) * (LANES * c1))
    fb = cs((2 * r + 1) * c0)
    ta = cs((2 * LANES * c1) * r)
    tb = cs((2 * c0 + 1) * r)
    return (fa, fb), (ta, tb)


def _dft_matrices(L):
    n1 = L // LANES
    tr = min(L, 256)
    outs = []
    for (ac, as_), (bc, bs) in _dft_tables(L):
        c, s = pl.pallas_call(
            functools.partial(_dft_kernel, n1=n1),
            out_shape=(jax.ShapeDtypeStruct((L, L), BF16), jax.ShapeDtypeStruct((L, L), BF16)),
            grid=(L // tr,),
            in_specs=[
                pl.BlockSpec((tr, n1), lambda i: (i, 0)),
                pl.BlockSpec((tr, n1), lambda i: (i, 0)),
                pl.BlockSpec((tr, LANES), lambda i: (i, 0)),
                pl.BlockSpec((tr, LANES), lambda i: (i, 0)),
            ],
            out_specs=(pl.BlockSpec((tr, L), lambda i: (i, 0)), pl.BlockSpec((tr, L), lambda i: (i, 0))),
            compiler_params=_cparams("parallel"),
            name="dft_tables",
        )(jnp.asarray(ac), jnp.asarray(as_), jnp.asarray(bc), jnp.asarray(bs))
        outs.append((c, s))
    return outs


def _inproj_kernel(x_ref, mod_ref, w_ref, cw_ref, cb_ref, u_ref, hg_ref, *, period, n_hy, tn):
    x = x_ref[0]
    tm = x.shape[0]
    xn = x * lax.rsqrt(jnp.mean(x * x, axis=-1, keepdims=True) + NORM_EPS)
    m = mod_ref[0]
    a = (xn * (1.0 + m[1:2]) + m[0:1]).astype(BF16)
    pos = lax.broadcasted_iota(jnp.int32, (tm, 1), 0) % period
    first = pos == 0
    last = pos == period - 1
    n_total = w_ref.shape[1]
    for j in range(n_total // tn):
        p = _dot(a, w_ref[:, j * tn:(j + 1) * tn])
        if j * tn < n_hy:
            cw = cw_ref[:, j * tn:(j + 1) * tn]
            prev = jnp.where(first, 0.0, pltpu.roll(p, 1, 0))
            nxt = jnp.where(last, 0.0, pltpu.roll(p, tm - 1, 0))
            u_ref[0, :, j * tn:(j + 1) * tn] = (
                prev * cw[0:1] + p * cw[1:2] + nxt * cw[2:3] + cb_ref[:, j * tn:(j + 1) * tn]
            ).astype(u_ref.dtype)
        else:
            hg_ref[0, :, j * tn - n_hy:(j + 1) * tn - n_hy] = p.astype(hg_ref.dtype)


def _inproj(x, mod_l, mod_row, w_in, layer, conv_w, conv_b, *, period, tm):
    bsz, L, d = x.shape
    n_hy = conv_w.shape[1]
    n_all = w_in.shape[-1]
    tn = n_hy // 3
    assert tm % period == 0 or period == L == tm
    return pl.pallas_call(
        functools.partial(_inproj_kernel, period=period, n_hy=n_hy, tn=tn),
        out_shape=(jax.ShapeDtypeStruct((bsz, L, n_hy), BF16),
                   jax.ShapeDtypeStruct((bsz, L, n_all - n_hy), BF16)),
        grid=(bsz, L // tm),
        in_specs=[
            pl.BlockSpec((1, tm, d), lambda b, i: (b, i, 0)),
            pl.BlockSpec((1, N_MOD, d), lambda b, i: (mod_row(b), 0, 0)),
            _resident((d, n_all), layer),
            _resident((3, n_hy)),
            _resident((1, n_hy)),
        ],
        out_specs=(pl.BlockSpec((1, tm, n_hy), lambda b, i: (b, i, 0)),
                   pl.BlockSpec((1, tm, n_all - n_hy), lambda b, i: (b, i, 0))),
        compiler_params=_cparams("parallel", "parallel"),
        name="inproj",
    )(x, mod_l, w_in, conv_w, conv_b.reshape(1, n_hy))


def _filter_kernel(z_ref, w1_ref, b1_ref, w2_ref, b2_ref, w3f_ref, b3f_ref, w3b_ref, b3b_ref,
                   dl_ref, o_ref, h_s):
    z = z_ref[...]

    @pl.when((pl.program_id(0) == 0) & (pl.program_id(1) == 0))
    def _():
        h1 = jnp.sin(_dot(z, w1_ref[...], precision=HIGHEST) + b1_ref[...])
        h_s[...] = jnp.sin(_dot(h1, w2_ref[...], precision=HIGHEST) + b2_ref[...])

    h = h_s[...]
    hf = _dot(h, w3f_ref[...], precision=HIGHEST) + b3f_ref[...]
    hb = _dot(h, w3b_ref[...], precision=HIGHEST) + b3b_ref[...]
    win = jnp.exp(-z[:, 0:1] * dl_ref[...])
    hf = hf * win
    hb = hb * win
    nrm = (jnp.sum(jnp.abs(hf), axis=0, keepdims=True)
           + jnp.sum(jnp.abs(hb), axis=0, keepdims=True))
    inv = 1.0 / nrm
    hf = hf * inv
    row = lax.broadcasted_iota(jnp.int32, (z.shape[0], 1), 0)
    hb0 = jnp.where(row == 0, 0.0, hb * inv)
    o_ref[0, 0] = hf + hb0
    o_ref[0, 1] = hf - hb0


def _hyena_filter_taps(L, w1, b1, w2, b2, w3, b3):
    nfeat, hid = w1.shape
    c = w3.shape[1] // (2 * HY_ORDER)
    tc = min(c, 256)
    nct = c // tc
    t = jnp.linspace(0.0, 1.0, L, dtype=F32)
    n = jnp.arange(L, dtype=F32)
    freqs = jnp.linspace(1e-4, HY_BANDS - 1, HY_BANDS, dtype=F32)
    ang = (2.0 * math.pi / L) * n[:, None] * freqs[None, :]
    z = jnp.concatenate([t[:, None], jnp.cos(ang), -jnp.sin(ang)], axis=-1)
    z = jnp.pad(z, ((0, 0), (0, LANES - nfeat)))
    w1p = jnp.pad(w1, ((0, LANES - nfeat), (0, 0)))
    deltas = jnp.abs(jnp.linspace(math.log(HY_TARGET) / HY_FAST_PCT, math.log(HY_TARGET) / HY_SLOW_PCT,
                                  c, dtype=F32)).reshape(1, c)
    full = lambda shape: pl.BlockSpec(shape, lambda o, j: (0,) * len(shape))
    return pl.pallas_call(
        _filter_kernel,
        out_shape=jax.ShapeDtypeStruct((HY_ORDER, 2, L, c), F32),
        grid=(HY_ORDER, nct),
        in_specs=[
            full((L, LANES)), full((LANES, hid)), full((1, hid)), full((hid, hid)), full((1, hid)),
            pl.BlockSpec((hid, tc), lambda o, j: (0, o * 2 * nct + j)),
            pl.BlockSpec((1, tc), lambda o, j: (0, o * 2 * nct + j)),
            pl.BlockSpec((hid, tc), lambda o, j: (0, o * 2 * nct + nct + j)),
            pl.BlockSpec((1, tc), lambda o, j: (0, o * 2 * nct + nct + j)),
            pl.BlockSpec((1, tc), lambda o, j: (0, j)),
        ],
        out_specs=pl.BlockSpec((1, 2, L, tc), lambda o, j: (o, 0, 0, j)),
        scratch_shapes=[pltpu.VMEM((L, hid), F32)],
        compiler_params=_cparams("arbitrary", "arbitrary"),
        name="hyena_filter",
    )(z, w1p, b1.reshape(1, hid), w2, b2.reshape(1, hid), w3, b3.reshape(1, -1), w3, b3.reshape(1, -1),
      deltas)


def _kspec_kernel(cf_ref, sf_ref, h_ref, o_ref, *, scale):
    o_ref[0, 0] = _dot(cf_ref[...], h_ref[0, 0].astype(BF16)) * scale
    o_ref[0, 1] = _dot(sf_ref[...], h_ref[0, 1].astype(BF16)) * scale


def _filter_spectrum(cft, sft, taps):
    _, _, L, c = taps.shape
    tk = min(L, 512)
    return pl.pallas_call(
        functools.partial(_kspec_kernel, scale=1.0 / L),
        out_shape=jax.ShapeDtypeStruct((HY_ORDER, 2, L, c), F32),
        grid=(HY_ORDER, L // tk),
        in_specs=[
            pl.BlockSpec((tk, L), lambda o, i: (i, 0)),
            pl.BlockSpec((tk, L), lambda o, i: (i, 0)),
            pl.BlockSpec((1, 2, L, c), lambda o, i: (o, 0, 0, 0)),
        ],
        out_specs=pl.BlockSpec((1, 2, tk, c), lambda o, i: (o, 0, i, 0)),
        compiler_params=_cparams("parallel", "parallel"),
        name="hyena_filter_spectrum",
    )(cft, sft, taps)


def _hy_fwd_kernel(cf_ref, sf_ref, v_ref, k_ref, p_ref, q_ref):
    v = v_ref[0].astype(BF16)
    a = _dot(cf_ref[...], v)
    b = _dot(sf_ref[...], v)
    kr = k_ref[0, 0]
    ks = k_ref[0, 1]
    p_ref[0] = (a * kr - b * ks).astype(BF16)
    q_ref[0] = (a * ks + b * kr).astype(BF16)


def _hy_forward(cft, sft, src, src_col, kspec, order):
    bsz, L, _ = src.shape
    c = kspec.shape[-1]
    tk = min(L, 1024)
    return pl.pallas_call(
        _hy_fwd_kernel,
        out_shape=(jax.ShapeDtypeStruct((bsz, L, c), BF16), jax.ShapeDtypeStruct((bsz, L, c), BF16)),
        grid=(L // tk, bsz),
        in_specs=[
            pl.BlockSpec((tk, L), lambda i, b: (i, 0)),
            pl.BlockSpec((tk, L), lambda i, b: (i, 0)),
            pl.BlockSpec((1, L, c), lambda i, b: (b, 0, src_col)),
            pl.BlockSpec((1, 2, tk, c), lambda i, b: (order, 0, i, 0)),
        ],
        out_specs=(pl.BlockSpec((1, tk, c), lambda i, b: (b, i, 0)),
                   pl.BlockSpec((1, tk, c), lambda i, b: (b, i, 0))),
        compiler_params=_cparams("parallel", "parallel"),
        name="hyena_spectrum",
    )(cft, sft, src, kspec)


def _hy_inv_kernel(cf_ref, sf_ref, p_ref, q_ref, gate_ref, src_ref, d_ref, o_ref):
    conv = _dot(cf_ref[...], p_ref[0]) + _dot(sf_ref[...], q_ref[0])
    o_ref[0] = (gate_ref[0] * (conv + src_ref[0] * d_ref[0])).astype(o_ref.dtype)


def _hy_inverse(cf, sf, p, q, gate, gate_col, src, src_col, d, order, out_dtype):
    bsz, L, c = p.shape
    tm = min(L, 1024)
    return pl.pallas_call(
        _hy_inv_kernel,
        out_shape=jax.ShapeDtypeStruct((bsz, L, c), out_dtype),
        grid=(L // tm, bsz),
        in_specs=[
            pl.BlockSpec((tm, L), lambda i, b: (i, 0)),
            pl.BlockSpec((tm, L), lambda i, b: (i, 0)),
            pl.BlockSpec((1, L, c), lambda i, b: (b, 0, 0)),
            pl.BlockSpec((1, L, c), lambda i, b: (b, 0, 0)),
            pl.BlockSpec((1, tm, c), lambda i, b: (b, i, gate_col)),
            pl.BlockSpec((1, tm, c), lambda i, b: (b, i, src_col)),
            pl.BlockSpec((1, 1, c), lambda i, b: (order, 0, 0)),
        ],
        out_specs=pl.BlockSpec((1, tm, c), lambda i, b: (b, i, 0)),
        compiler_params=_cparams("parallel", "parallel"),
        name="hyena_inverse",
    )(cf, sf, p, q, gate, src, d.reshape(HY_ORDER, 1, c))


def _hyena(u, dft, taps_w, d):
    (cft, sft), (cf, sf) = dft
    L = u.shape[1]
    taps = _hyena_filter_taps(L, *taps_w)
    kspec = _filter_spectrum(cft, sft, taps)
    p, q = _hy_forward(cft, sft, u, 0, kspec, 0)
    z = _hy_inverse(cf, sf, p, q, u, 1, u, 0, d, 0, BF16)
    p, q = _hy_forward(cft, sft, z, 0, kspec, 1)
    return _hy_inverse(cf, sf, p, q, u, 2, z, 0, d, 1, BF16)


def _anchor_rows(b, n, a):
    assert n % 8 == 0
    parts = [jnp.broadcast_to(b[s + a:s + a + 1, :], (n, b.shape[1])) for s in range(0, b.shape[0], n)]
    return parts[0] if len(parts) == 1 else jnp.concatenate(parts, axis=0)


def _hgrn_group(q_raw, v, f_logit, lb, st_ref, lv, tri, *, reverse):
    C = HGRN_CHUNK
    G = q_raw.shape[0] // C
    rows = lambda x, i: x[i * C:(i + 1) * C]
    q = q_raw * _sigmoid(q_raw)
    f = lb + (1.0 - lb) * _sigmoid(f_logit)
    kk = 1.0 - f
    g = jnp.log2(f)
    g_top = lax.bitcast_convert_type(
        lax.bitcast_convert_type(g, jnp.uint32) & jnp.uint32(0xFFFF0000), F32)
    gg = jnp.concatenate([g_top.astype(BF16), (g - g_top).astype(BF16)], axis=1)
    bb = [_dot(tri, rows(gg, i)) for i in range(G)]
    part = lambda k: jnp.concatenate(
        [x[k * C:(k + 1) * C, :HEAD_DIM] + x[k * C:(k + 1) * C, HEAD_DIM:] for x in bb], axis=0)
    b = part(0)
    fine = {n_: part(k + 1) for k, n_ in enumerate(HGRN_FINE_LEVELS)}

    qb = q.astype(BF16)
    kb = kk.astype(BF16)
    scores = [jnp.zeros((C, C), F32)] * G
    n = 2
    level = 1
    while n <= C:
        a = n // 2 if reverse else n // 2 - 1
        d = fine[n] if n in fine else b - _anchor_rows(b, n, a)
        e = jnp.exp2(_neg_abs(d)).astype(BF16)
        qe = qb * e
        ke = kb * e
        scores = [jnp.where(lv == level, _dot_nt(rows(qe, i), rows(ke, i)), scores[i]) for i in range(G)]
        n *= 2
        level += 1

    vb = v.astype(BF16)
    diag = jnp.sum(q * kk, axis=-1, keepdims=True) * v
    o = [_dot(scores[i].astype(BF16), rows(vb, i)) + rows(diag, i) for i in range(G)]
    b_end = _anchor_rows(b, C, 0 if reverse else C - 1)
    qd = qb * jnp.exp2(b).astype(BF16)
    kd = kb * jnp.exp2(b_end - b).astype(BF16)
    dec = jnp.exp2(b_end)
    kv = [_dot(rows(v, i).T.astype(BF16), rows(kd, i)) for i in range(G)]
    st = st_ref[...]
    for i in (range(G - 1, -1, -1) if reverse else range(G)):
        o[i] = o[i] + _dot_nt(rows(qd, i), st.astype(BF16))
        st = st * dec[i * C:i * C + 1] + kv[i]
    st_ref[...] = st
    return jnp.concatenate(o, axis=0)


def _hgrn_kernel(qc, ic, gc, fc, bc, ql, il, gl, fl, bl, lb_ref, nw_ref, oc_ref, ol_ref,
                 of_s, ob_s, stf, stb, *, n_ctx, n_lat):
    C = HGRN_CHUNK
    row = lax.broadcasted_iota(jnp.int32, (C, C), 0)
    col = lax.broadcasted_iota(jnp.int32, (C, C), 1)
    x = row ^ col
    lvl = jnp.zeros((C, C), jnp.int32)
    n = 1
    while n < C:
        lvl = lvl + (x >= n).astype(jnp.int32)
        n *= 2
    lv_f = jnp.where(row > col, lvl, 0)
    lv_b = jnp.where(row < col, lvl, 0)

    def cum_matrix(reverse):
        cum = lambda r: (col >= r) if reverse else (col <= r)
        mats = [cum(row).astype(F32)]
        for n_ in HGRN_FINE_LEVELS:
            anchor = row - row % n_ + (n_ // 2 if reverse else n_ // 2 - 1)
            mats.append(mats[0] - cum(anchor).astype(F32))
        return jnp.concatenate(mats, axis=0).astype(BF16)

    tri_f = cum_matrix(False)
    tri_b = cum_matrix(True)
    lb_f = lb_ref[0, 0:1, :]
    lb_b = lb_ref[0, 1:2, :]
    stf[...] = jnp.zeros_like(stf)
    stb[...] = jnp.zeros_like(stb)

    def run(q_ref, i_ref, f_ref, b_ref, n_chunks, base):
        G = min(n_chunks, HGRN_GROUP)
        R = G * C
        n_groups = n_chunks // G

        def body(j, carry):
            off_f = pl.multiple_of(j * R, R)
            off_b = pl.multiple_of((n_groups - 1 - j) * R, R)
            rf = pl.ds(off_f, R)
            rb = pl.ds(off_b, R)
            ld = lambda ref, r: ref[0, r, :].astype(F32)
            of_s[pl.ds(base + off_f, R), :] = _hgrn_group(
                ld(q_ref, rf), ld(i_ref, rf), ld(f_ref, rf), lb_f, stf, lv_f, tri_f, reverse=False)
            ob_s[pl.ds(base + off_b, R), :] = _hgrn_group(
                ld(q_ref, rb), ld(i_ref, rb), ld(b_ref, rb), lb_b, stb, lv_b, tri_b, reverse=True)
            return carry
        lax.fori_loop(0, n_groups, body, 0)

    run(qc, ic, fc, bc, n_ctx, 0)
    run(ql, il, fl, bl, n_lat, n_ctx * C)

    nw = nw_ref[...]

    def finish(g_ref, o_ref, n_chunks, base):
        R = C * math.gcd(n_chunks, 4)

        def body(j, carry):
            r = pl.ds(pl.multiple_of(j * R, R), R)
            rs = pl.ds(pl.multiple_of(base + j * R, C), R)
            o = of_s[rs, :] + ob_s[rs, :]
            o = o * lax.rsqrt(jnp.mean(o * o, axis=-1, keepdims=True) + NORM_EPS) * nw
            g = g_ref[0, r, :].astype(F32)
            o_ref[0, r, :] = (o * (g * _sigmoid(g))).astype(o_ref.dtype)
            return carry
        lax.fori_loop(0, n_chunks * C // R, body, 0)

    finish(gc, oc_ref, n_ctx, 0)
    finish(gl, ol_ref, n_lat, n_ctx * C)


def _hgrn(hg_ctx, hg_lat, lb, norm_w):
    bsz, lc, n5 = hg_ctx.shape
    ll = hg_lat.shape[1]
    dg = n5 // 5
    nh = dg // HEAD_DIM
    C = HGRN_CHUNK
    assert lc % C == 0 and ll % C == 0
    assert all((n // C) % min(n // C, HGRN_GROUP) == 0 for n in (lc, ll))

    def slab(L, part):
        return pl.BlockSpec((1, L, HEAD_DIM), lambda b, h: (b, 0, part * nh + h))

    return pl.pallas_call(
        functools.partial(_hgrn_kernel, n_ctx=lc // C, n_lat=ll // C),
        out_shape=(jax.ShapeDtypeStruct((bsz, lc, dg), BF16), jax.ShapeDtypeStruct((bsz, ll, dg), BF16)),
        grid=(bsz, nh),
        in_specs=[slab(lc, p) for p in range(5)] + [slab(ll, p) for p in range(5)] + [
            pl.BlockSpec((1, 2, HEAD_DIM), lambda b, h: (h, 0, 0)),
            pl.BlockSpec((1, HEAD_DIM), lambda b, h: (0, 0)),
        ],
        out_specs=(pl.BlockSpec((1, lc, HEAD_DIM), lambda b, h: (b, 0, h)),
                   pl.BlockSpec((1, ll, HEAD_DIM), lambda b, h: (b, 0, h))),
        scratch_shapes=[
            pltpu.VMEM((lc + ll, HEAD_DIM), F32),
            pltpu.VMEM((lc + ll, HEAD_DIM), F32),
            pltpu.VMEM((HEAD_DIM, HEAD_DIM), F32),
            pltpu.VMEM((HEAD_DIM, HEAD_DIM), F32),
        ],
        compiler_params=_cparams("parallel", "parallel"),
        name="hgrn2",
    )(*([hg_ctx] * 5), *([hg_lat] * 5), lb, norm_w.reshape(1, HEAD_DIM))


def _first_argmax(vals, idx, sentinel):
    m = jnp.max(vals, axis=0, keepdims=True)
    first = jnp.min(jnp.where(vals == m, idx, sentinel), axis=0, keepdims=True)
    return m, first


def _moe_gates(logits_t, rbias):
    ne, tm = logits_t.shape
    neg = -jnp.inf
    mx = jnp.max(logits_t, axis=0, keepdims=True)
    ex = jnp.exp(logits_t - mx)
    scores = ex / jnp.sum(ex, axis=0, keepdims=True)
    sel = scores + rbias
    eidx = lax.broadcasted_iota(jnp.int32, (ne, tm), 0)
    grp = eidx // EXPERTS_PER_GROUP
    best_s = jnp.full((1, tm), neg, F32)
    best_g = jnp.zeros((1, tm), jnp.int32)
    for gi in range(N_GROUPS):
        mg = jnp.where(grp == gi, sel, neg)
        m1, i1 = _first_argmax(mg, eidx, ne)
        m2 = jnp.max(jnp.where(eidx == i1, neg, mg), axis=0, keepdims=True)
        gs = m1 + m2
        upd = gs > best_s
        best_g = jnp.where(upd, gi, best_g)
        best_s = jnp.where(upd, gs, best_s)
    sg = jnp.where(grp == best_g, sel, neg)
    _, i1 = _first_argmax(sg, eidx, ne)
    _, i2 = _first_argmax(jnp.where(eidx == i1, neg, sg), eidx, ne)
    chosen = (eidx == i1) | (eidx == i2)
    w = jnp.where(chosen, scores, 0.0)
    return w / jnp.sum(w, axis=0, keepdims=True), best_g


def _mix_moe_kernel(x_ref, hy_ref, hg_ref, mod_ref, w_ref, rw_ref, rb_ref, wg_ref, wu_ref, wd_ref,
                    o_ref, t_s, g4_s, *, cap):
    c = hy_ref.shape[-1]
    mix = _dot(hy_ref[0], w_ref[:c, :]) + _dot(hg_ref[0], w_ref[c:, :])
    m = mod_ref[0]
    x1 = x_ref[0] + m[2:3] * mix
    o_ref[0] = x1
    xn = x1 * lax.rsqrt(jnp.mean(x1 * x1, axis=-1, keepdims=True) + NORM_EPS)
    t = xn * (1.0 + m[4:5]) + m[3:4]
    t_hi = t.astype(BF16)
    t_s[...] = t_hi
    t_lo = (t - t_hi.astype(F32)).astype(BF16)
    r = _dot(t_hi, rw_ref[...])
    lg = r[:, :LANES] + r[:, LANES:] + _dot(t_lo, rw_ref[:, :LANES])
    ne, _, f = wg_ref.shape
    tm = t.shape[0]
    per_group = ne // N_GROUPS
    gates_t, best_g = _moe_gates(lg.T[:ne], rb_ref[...])

    row8 = lax.broadcasted_iota(jnp.int32, (8, tm), 0)
    member = row8 == best_g
    ri = lax.broadcasted_iota(jnp.int32, (tm, tm), 0)
    ci = lax.broadcasted_iota(jnp.int32, (tm, tm), 1)
    before = _dot(member.astype(BF16), (ri < ci).astype(BF16))
    rank_row = jnp.sum(jnp.where(member, before, 0.0), axis=0, keepdims=True)
    grp_row = best_g.astype(F32)
    count = jnp.sum(member.astype(F32), axis=1, keepdims=True)
    g4_t = gates_t
    for g in range(1, N_GROUPS):
        g4_t = g4_t + pltpu.roll(gates_t, g * per_group, 0)
    row_e = lax.broadcasted_iota(jnp.int32, (ne, tm), 0)
    info_t = jnp.where(row_e < per_group, g4_t,
                       jnp.where(row_e == per_group, rank_row, jnp.where(row_e == per_group + 1, grp_row, 0.0)))
    info = jnp.concatenate([info_t, jnp.zeros((LANES - ne, tm), F32)], axis=0).T
    g4_s[...] = info.astype(BF16)
    rank_col = info[:, per_group:per_group + 1]
    grp_col = info[:, per_group + 1:per_group + 2]
    g2 = m[5:6]

    slot_col = lax.broadcasted_iota(jnp.int32, (cap, 1), 0).astype(F32)
    slot_row = lax.broadcasted_iota(jnp.int32, (1, cap), 1).astype(F32)
    for g in range(N_GROUPS):
        n_g = count[g, 0].astype(jnp.int32)

        def body(s, carry, g=g):
            base = (s * cap).astype(F32)
            pick = ((rank_row - base == slot_col) & (grp_row == float(g))).astype(BF16)
            xg = _dot(pick, t_s[...]).astype(BF16)
            gs = _dot(pick, g4_s[...])
            acts = []
            for j in range(per_group):
                e = g * per_group + j
                hgate = _dot(xg, wg_ref[e])
                hup = _dot(xg, wu_ref[e])
                acts.append((hgate * _sigmoid(hgate) * hup * gs[:, j:j + 1]).astype(BF16))
            y = _dot(jnp.concatenate(acts, axis=1), wd_ref[g * per_group * f:(g + 1) * per_group * f, :])
            put = ((rank_col - base == slot_row) & (grp_col == float(g))).astype(BF16)
            o_ref[0] += _dot(put, (y * g2).astype(BF16))
            return carry

        lax.fori_loop(0, (n_g + cap - 1) // cap, body, 0)


def _mix_moe(x, hy, hg, mod_l, mod_row, layer, w_out, router_w, rbias, wg, wu, wd, *, tm):
    bsz, L, d = x.shape
    c = hy.shape[-1]
    cg = hg.shape[-1]
    _, ne, _, f = wg.shape
    rw_hi = router_w.astype(BF16)
    rw_lo = (router_w - rw_hi.astype(F32)).astype(BF16)
    rw = jnp.zeros((d, 2 * LANES), BF16).at[:, :ne].set(rw_hi).at[:, LANES:LANES + ne].set(rw_lo)
    cap = 5 * tm // 16
    assert cap % 8 == 0
    return pl.pallas_call(
        functools.partial(_mix_moe_kernel, cap=cap),
        out_shape=jax.ShapeDtypeStruct((bsz, L, d), F32),
        grid=(bsz, L // tm),
        in_specs=[
            pl.BlockSpec((1, tm, d), lambda b, i: (b, i, 0)),
            pl.BlockSpec((1, tm, c), lambda b, i: (b, i, 0)),
            pl.BlockSpec((1, tm, cg), lambda b, i: (b, i, 0)),
            pl.BlockSpec((1, N_MOD, d), lambda b, i: (mod_row(b), 0, 0)),
            _resident((c + cg, d), layer),
            _resident((d, 2 * LANES)),
            _resident((ne, 1)),
            _resident((ne, d, f), layer),
            _resident((ne, d, f), layer),
            _resident((ne * f, d), layer),
        ],
        out_specs=pl.BlockSpec((1, tm, d), lambda b, i: (b, i, 0)),
        scratch_shapes=[pltpu.VMEM((tm, d), BF16), pltpu.VMEM((tm, LANES), BF16)],
        compiler_params=_cparams("parallel", "parallel"),
        name="mix_moe",
    )(x, hy, hg, mod_l, w_out, rw, rbias.reshape(ne, 1), wg, wu, wd.reshape(-1, ne * f, d))


def _final_norm_kernel(x_ref, w_ref, o_ref):
    x = x_ref[0]
    o_ref[0] = x * lax.rsqrt(jnp.mean(x * x, axis=-1, keepdims=True) + NORM_EPS) * w_ref[...]


def _final_norm(x, w, *, tm):
    bsz, L, d = x.shape
    return pl.pallas_call(
        _final_norm_kernel,
        out_shape=jax.ShapeDtypeStruct(x.shape, F32),
        grid=(bsz, L // tm),
        in_specs=[pl.BlockSpec((1, tm, d), lambda b, i: (b, i, 0)), pl.BlockSpec((1, d), lambda b, i: (0, 0))],
        out_specs=pl.BlockSpec((1, tm, d), lambda b, i: (b, i, 0)),
        compiler_params=_cparams("parallel", "parallel"),
        name="final_norm",
    )(x, w.reshape(1, d))


def kernel(x, c, ctx, c_ctx, w_mod, b_mod, w_in, w_out, hy_conv_w, hy_conv_b, hy_w1, hy_b1, hy_w2, hy_b2, hy_w3, hy_b3, hy_bias, hgrn_lower_bounds, hgrn_norm_w, router_w, router_bias, moe_w_gate, moe_w_up, moe_w_down, final_norm_w):
    bsz, seq, d = x.shape
    ctx_len = ctx.shape[1]
    depth = w_mod.shape[0]
    dg = hgrn_lower_bounds.shape[-1]
    nh = dg // HEAD_DIM
    assert bsz + 1 <= MOD_ROWS and seq % GRID_W == 0

    cc = jnp.zeros((MOD_ROWS, d), F32).at[:bsz].set(c).at[bsz].set(c_ctx)
    mod = _modulation(cc, w_mod, b_mod).reshape(depth, MOD_ROWS, N_MOD, d)
    lat_row = lambda b: b
    ctx_row = lambda b: bsz

    lb_soft = jax.nn.softmax(hgrn_lower_bounds.astype(F32), axis=1)
    lower = jnp.cumsum(lb_soft, axis=1) - lb_soft[:, :1]
    lower = lower.reshape(2, depth, nh, HEAD_DIM).transpose(1, 2, 0, 3)

    dft_lat = _dft_matrices(seq)
    dft_ctx = _dft_matrices(ctx_len)
    tm_lat = min(seq, 1024)
    tm_moe = min(seq, 512)

    w_in_b = w_in.astype(BF16)
    moe_w = (w_out.astype(BF16), router_w, router_bias,
             moe_w_gate.astype(BF16), moe_w_up.astype(BF16), moe_w_down.astype(BF16))

    xc = ctx
    for l in range(depth):
        last = l == depth - 1
        u_lat, hg_lat = _inproj(x, mod[l], lat_row, w_in_b, l, hy_conv_w[l], hy_conv_b[l],
                                period=GRID_W, tm=tm_lat)
        u_ctx, hg_ctx = _inproj(xc, mod[l], ctx_row, w_in_b, l, hy_conv_w[l], hy_conv_b[l],
                                period=ctx_len, tm=ctx_len)
        o_ctx, o_lat = _hgrn(hg_ctx, hg_lat, lower[l], hgrn_norm_w[l])
        taps_w = (hy_w1[l], hy_b1[l], hy_w2[l], hy_b2[l], hy_w3[l], hy_b3[l])
        hy_lat = _hyena(u_lat, dft_lat, taps_w, hy_bias[l])
        x = _mix_moe(x, hy_lat, o_lat, mod[l], lat_row, l, *moe_w, tm=tm_moe)
        if not last:
            hy_ctx = _hyena(u_ctx, dft_ctx, taps_w, hy_bias[l])
            xc = _mix_moe(xc, hy_ctx, o_ctx, mod[l], ctx_row, l, *moe_w, tm=ctx_len)

    return _final_norm(x, final_norm_w, tm=tm_lat)
```

```python
import functools
import math

import numpy as np
import jax
import jax.numpy as jnp
from jax import lax
from jax.experimental import pallas as pl
from jax.experimental.pallas import tpu as pltpu

F32 = jnp.float32
BF16 = jnp.bfloat16
HIGHEST = lax.Precision.HIGHEST

GRID_W = 64
NORM_EPS = 1e-6
N_MOD = 6
HY_ORDER = 2
HY_BANDS = 16
HY_TARGET = 1e-2
HY_FAST_PCT = 0.3
HY_SLOW_PCT = 1.5
HEAD_DIM = 128
HY_BLOCKS = 2
N_EXPERTS = 16
N_GROUPS = 4
EXPERTS_PER_GROUP = N_EXPERTS // N_GROUPS
LANES = 128
MOD_ROWS = 16
HGRN_CHUNK = 128
HGRN_GROUP = 16
HGRN_FINE_LEVELS = (2, 4)
VMEM_LIMIT = 56 << 20


def _cparams(*sem, flags=None):
    return pltpu.CompilerParams(dimension_semantics=sem, vmem_limit_bytes=VMEM_LIMIT, flags=flags)


def _sigmoid(x):
    return 1.0 / (1.0 + jnp.exp(-x))


def _dot(a, b, **kw):
    return jnp.dot(a, b, preferred_element_type=F32, **kw)


def _resident(shape, layer=None):
    if layer is None:
        return pl.BlockSpec(shape, lambda *_: (0,) * len(shape), pipeline_mode=pl.Buffered(1))
    return pl.BlockSpec((None,) + tuple(shape), lambda *_: (layer,) + (0,) * len(shape),
                        pipeline_mode=pl.Buffered(1))


def _neg_abs(x):
    bits = lax.bitcast_convert_type(x, jnp.uint32) | jnp.uint32(0x80000000)
    return lax.bitcast_convert_type(bits, F32)


def _dot_nt(a, b):
    return lax.dot_general(a, b, (((1,), (1,)), ((), ())), preferred_element_type=F32)


def _mod_kernel(c_ref, w_ref, b_ref, o_ref):
    c = c_ref[...]
    cs = c * _sigmoid(c)
    o_ref[0] = _dot(cs, w_ref[0], precision=HIGHEST) + b_ref[0]


def _modulation(cc, w_mod, b_mod):
    depth, d, n = w_mod.shape
    tn = n // 3
    return pl.pallas_call(
        _mod_kernel,
        out_shape=jax.ShapeDtypeStruct((depth, MOD_ROWS, n), F32),
        grid=(depth, n // tn),
        in_specs=[
            pl.BlockSpec((MOD_ROWS, d), lambda l, j: (0, 0)),
            pl.BlockSpec((1, d, tn), lambda l, j: (l, 0, j)),
            pl.BlockSpec((1, 1, tn), lambda l, j: (l, 0, j)),
        ],
        out_specs=pl.BlockSpec((1, MOD_ROWS, tn), lambda l, j: (l, 0, j)),
        compiler_params=_cparams("parallel", "parallel"),
        name="modulation",
    )(cc, w_mod, b_mod.reshape(depth, 1, n))


def _dft_kernel(ac_ref, as_ref, bc_ref, bs_ref, c_ref, s_ref, *, n1):
    bc = bc_ref[...]
    bs = bs_ref[...]
    ac = ac_ref[...]
    as_ = as_ref[...]
    for j in range(n1):
        a_c = ac[:, j:j + 1]
        a_s = as_[:, j:j + 1]
        c_ref[:, j * LANES:(j + 1) * LANES] = (a_c * bc - a_s * bs).astype(BF16)
        s_ref[:, j * LANES:(j + 1) * LANES] = (a_s * bc + a_c * bs).astype(BF16)


def _dft_tables(L):
    n1 = L // LANES
    period = 4 * L
    r = np.arange(L, dtype=np.int64)[:, None]
    c1 = np.arange(n1, dtype=np.int64)[None, :]
    c0 = np.arange(LANES, dtype=np.int64)[None, :]

    def cs(phase):
        ang = (phase % period).astype(np.float64) * (2.0 * np.pi / period)
        return np.cos(ang).astype(np.float32), np.sin(ang).astype(np.float32)

    fa = cs((2 * r + 1) * (LANES * c1))
    fb = cs((2 * r + 1) * c0)
    ta = cs((2 * LANES * c1) * r)
    tb = cs((2 * c0 + 1) * r)
    return (fa, fb), (ta, tb)


def _dft_matrices(L):
    n1 = L // LANES
    tr = min(L, 256)
    outs = []
    for (ac, as_), (bc, bs) in _dft_tables(L):
        c, s = pl.pallas_call(
            functools.partial(_dft_kernel, n1=n1),
            out_shape=(jax.ShapeDtypeStruct((L, L), BF16), jax.ShapeDtypeStruct((L, L), BF16)),
            grid=(L // tr,),
            in_specs=[
                pl.BlockSpec((tr, n1), lambda i: (i, 0)),
                pl.BlockSpec((tr, n1), lambda i: (i, 0)),
                pl.BlockSpec((tr, LANES), lambda i: (i, 0)),
                pl.BlockSpec((tr, LANES), lambda i: (i, 0)),
            ],
            out_specs=(pl.BlockSpec((tr, L), lambda i: (i, 0)), pl.BlockSpec((tr, L), lambda i: (i, 0))),
            compiler_params=_cparams("parallel"),
            name="dft_tables",
        )(jnp.asarray(ac), jnp.asarray(as_), jnp.asarray(bc), jnp.asarray(bs))
        outs.append((c, s))
    return outs


def _inproj_kernel(x_ref, mod_ref, w_ref, cw_ref, cb_ref, u_ref, hg_ref, *, period, n_hy, tn):
    x = x_ref[0]
    tm = x.shape[0]
    xn = x * lax.rsqrt(jnp.mean(x * x, axis=-1, keepdims=True) + NORM_EPS)
    m = mod_ref[0]
    a = (xn * (1.0 + m[1:2]) + m[0:1]).astype(BF16)
    pos = lax.broadcasted_iota(jnp.int32, (tm, 1), 0) % period
    first = pos == 0
    last = pos == period - 1
    n_total = w_ref.shape[1]
    for j in range(n_total // tn):
        p = _dot(a, w_ref[:, j * tn:(j + 1) * tn])
        if j * tn < n_hy:
            cw = cw_ref[:, j * tn:(j + 1) * tn]
            prev = jnp.where(first, 0.0, pltpu.roll(p, 1, 0))
            nxt = jnp.where(last, 0.0, pltpu.roll(p, tm - 1, 0))
            u_ref[0, :, j * tn:(j + 1) * tn] = (
                prev * cw[0:1] + p * cw[1:2] + nxt * cw[2:3] + cb_ref[:, j * tn:(j + 1) * tn]
            ).astype(u_ref.dtype)
        else:
            hg_ref[0, :, j * tn - n_hy:(j + 1) * tn - n_hy] = p.astype(hg_ref.dtype)


def _inproj(x, mod_l, mod_row, w_in, layer, conv_w, conv_b, *, period, tm):
    bsz, L, d = x.shape
    n_hy = conv_w.shape[1]
    n_all = w_in.shape[-1]
    tn = n_hy // 3
    assert tm % period == 0 or period == L == tm
    return pl.pallas_call(
        functools.partial(_inproj_kernel, period=period, n_hy=n_hy, tn=tn),
        out_shape=(jax.ShapeDtypeStruct((bsz, L, n_hy), BF16),
                   jax.ShapeDtypeStruct((bsz, L, n_all - n_hy), BF16)),
        grid=(bsz, L // tm),
        in_specs=[
            pl.BlockSpec((1, tm, d), lambda b, i: (b, i, 0)),
            pl.BlockSpec((1, N_MOD, d), lambda b, i: (mod_row(b), 0, 0)),
            _resident((d, n_all), layer),
            _resident((3, n_hy)),
            _resident((1, n_hy)),
        ],
        out_specs=(pl.BlockSpec((1, tm, n_hy), lambda b, i: (b, i, 0)),
                   pl.BlockSpec((1, tm, n_all - n_hy), lambda b, i: (b, i, 0))),
        compiler_params=_cparams("parallel", "parallel"),
        name="inproj",
    )(x, mod_l, w_in, conv_w, conv_b.reshape(1, n_hy))


def _rev_rows(x):
    nblk = x.shape[0] // LANES
    r = lax.broadcasted_iota(jnp.int32, (LANES, LANES), 0)
    c_ = lax.broadcasted_iota(jnp.int32, (LANES, LANES), 1)
    exch = (r + c_ == LANES - 1).astype(BF16)
    xb = x.astype(BF16)
    return jnp.concatenate(
        [_dot(exch, xb[(nblk - 1 - i) * LANES:(nblk - i) * LANES]) for i in range(nblk)], axis=0)


def _filter_kernel(z_ref, w1_ref, b1_ref, w2_ref, b2_ref, w3f_ref, b3f_ref, w3b_ref, b3b_ref,
                   dl_ref, kap_ref, rev_ref, h_s):
    z = z_ref[...]
    L = z.shape[0]

    @pl.when((pl.program_id(0) == 0) & (pl.program_id(1) == 0))
    def _():
        h1 = jnp.sin(_dot(z, w1_ref[...], precision=HIGHEST) + b1_ref[...])
        h_s[...] = jnp.sin(_dot(h1, w2_ref[...], precision=HIGHEST) + b2_ref[...])

    h = h_s[...]
    hf = _dot(h, w3f_ref[...], precision=HIGHEST) + b3f_ref[...]
    hb = _dot(h, w3b_ref[...], precision=HIGHEST) + b3b_ref[...]
    win = jnp.exp(-z[:, 0:1] * dl_ref[...])
    hf = hf * win
    hb = hb * win
    nrm = (jnp.sum(jnp.abs(hf), axis=0, keepdims=True)
           + jnp.sum(jnp.abs(hb), axis=0, keepdims=True))
    inv = 1.0 / nrm
    hf = hf * inv
    hb = hb * inv
    first = lax.broadcasted_iota(jnp.int32, (L, 1), 0) == 0
    down1 = lambda y: jnp.where(first, 0.0, pltpu.roll(y, 1, 0))
    kap_ref[0, 0:L, :] = down1(_rev_rows(hb)).astype(BF16)
    kap_ref[0, L:2 * L, :] = hf.astype(BF16)
    rev_ref[0, 0:L, :] = down1(_rev_rows(hf)).astype(BF16)
    rev_ref[0, L:2 * L, :] = jnp.where(first, hf[0:1], hb).astype(BF16)


def _hyena_filter_taps(L, w1, b1, w2, b2, w3, b3):
    nfeat, hid = w1.shape
    c = w3.shape[1] // (2 * HY_ORDER)
    tc = min(c, 256)
    nct = c // tc
    t = jnp.linspace(0.0, 1.0, L, dtype=F32)
    n = jnp.arange(L, dtype=F32)
    freqs = jnp.linspace(1e-4, HY_BANDS - 1, HY_BANDS, dtype=F32)
    ang = (2.0 * math.pi / L) * n[:, None] * freqs[None, :]
    z = jnp.concatenate([t[:, None], jnp.cos(ang), -jnp.sin(ang)], axis=-1)
    z = jnp.pad(z, ((0, 0), (0, LANES - nfeat)))
    w1p = jnp.pad(w1, ((0, LANES - nfeat), (0, 0)))
    deltas = jnp.abs(jnp.linspace(math.log(HY_TARGET) / HY_FAST_PCT, math.log(HY_TARGET) / HY_SLOW_PCT,
                                  c, dtype=F32)).reshape(1, c)
    full = lambda shape: pl.BlockSpec(shape, lambda o, j: (0,) * len(shape))
    out = jax.ShapeDtypeStruct((HY_ORDER, 2 * L, c), BF16)
    return pl.pallas_call(
        _filter_kernel,
        out_shape=(out, out),
        grid=(HY_ORDER, nct),
        in_specs=[
            full((L, LANES)), full((LANES, hid)), full((1, hid)), full((hid, hid)), full((1, hid)),
            pl.BlockSpec((hid, tc), lambda o, j: (0, o * 2 * nct + j)),
            pl.BlockSpec((1, tc), lambda o, j: (0, o * 2 * nct + j)),
            pl.BlockSpec((hid, tc), lambda o, j: (0, o * 2 * nct + nct + j)),
            pl.BlockSpec((1, tc), lambda o, j: (0, o * 2 * nct + nct + j)),
            pl.BlockSpec((1, tc), lambda o, j: (0, j)),
        ],
        out_specs=(pl.BlockSpec((1, 2 * L, tc), lambda o, j: (o, 0, j)),
                   pl.BlockSpec((1, 2 * L, tc), lambda o, j: (o, 0, j))),
        scratch_shapes=[pltpu.VMEM((L, hid), F32)],
        compiler_params=_cparams("arbitrary", "arbitrary"),
        name="hyena_filter",
    )(z, w1p, b1.reshape(1, hid), w2, b2.reshape(1, hid), w3, b3.reshape(1, -1), w3, b3.reshape(1, -1),
      deltas)


def _kspec_kernel(cf_ref, sf_ref, pos_ref, neg_ref, o_ref, *, scale):
    pos = pos_ref[0].astype(F32)
    neg = neg_ref[0].astype(F32)
    first = lax.broadcasted_iota(jnp.int32, (pos.shape[0], 1), 0) == 0
    neg = jnp.where(first, 0.0, neg)
    o_ref[0, 0, 0] = _dot(cf_ref[...], (pos + neg).astype(BF16)) * scale
    o_ref[0, 0, 1] = _dot(sf_ref[...], (pos - neg).astype(BF16)) * scale


def _filter_spectrum(cft, sft, kap, rev, nb):
    _, two_l, c = kap.shape
    bk = two_l // (2 * nb)
    n_off = 2 * nb - 1
    return pl.pallas_call(
        functools.partial(_kspec_kernel, scale=1.0 / bk),
        out_shape=jax.ShapeDtypeStruct((HY_ORDER, n_off, 2, bk, c), F32),
        grid=(HY_ORDER, n_off),
        in_specs=[
            _resident((bk, bk)),
            _resident((bk, bk)),
            pl.BlockSpec((1, bk, c), lambda o, k: (o, k + 1, 0)),
            pl.BlockSpec((1, bk, c), lambda o, k: (o, 2 * nb - 1 - k, 0)),
        ],
        out_specs=pl.BlockSpec((1, 1, 2, bk, c), lambda o, k: (o, k, 0, 0, 0)),
        compiler_params=_cparams("parallel", "parallel"),
        name="hyena_filter_spectrum",
    )(cft, sft, kap, rev)


def _hy_fwd_kernel(cf_ref, sf_ref, v_ref, a_ref, b_ref, *, nb, bk):
    for j in range(nb):
        rows = slice(j * bk, (j + 1) * bk)
        v = v_ref[0, rows, :].astype(BF16)
        a_ref[0, rows, :] = _dot(cf_ref[...], v).astype(BF16)
        b_ref[0, rows, :] = _dot(sf_ref[...], v).astype(BF16)


def _hy_forward(cft, sft, src, src_col, c, nb):
    bsz, L, _ = src.shape
    bk = L // nb
    out = jax.ShapeDtypeStruct((bsz, L, c), BF16)
    return pl.pallas_call(
        functools.partial(_hy_fwd_kernel, nb=nb, bk=bk),
        out_shape=(out, out),
        grid=(bsz,),
        in_specs=[
            _resident((bk, bk)),
            _resident((bk, bk)),
            pl.BlockSpec((1, L, c), lambda b: (b, 0, src_col)),
        ],
        out_specs=(pl.BlockSpec((1, L, c), lambda b: (b, 0, 0)), pl.BlockSpec((1, L, c), lambda b: (b, 0, 0))),
        compiler_params=_cparams("parallel"),
        name="hyena_spectrum",
    )(cft, sft, src)


def _hy_inv_kernel(cf_ref, sf_ref, a_ref, b_ref, k_ref, gate_ref, src_ref, d_ref, o_ref, *, nb, bk):
    for i in range(nb):
        p = q = None
        for j in range(nb):
            rows = slice(j * bk, (j + 1) * bk)
            a = a_ref[0, rows, :].astype(F32)
            b = b_ref[0, rows, :].astype(F32)
            kr = k_ref[i - j + nb - 1, 0]
            ks = k_ref[i - j + nb - 1, 1]
            pj = a * kr - b * ks
            qj = a * ks + b * kr
            p = pj if p is None else p + pj
            q = qj if q is None else q + qj
        conv = _dot(cf_ref[...], p.astype(BF16)) + _dot(sf_ref[...], q.astype(BF16))
        rows = slice(i * bk, (i + 1) * bk)
        o_ref[0, rows, :] = (gate_ref[0, rows, :] * (conv + src_ref[0, rows, :] * d_ref[0])).astype(o_ref.dtype)


def _hy_inverse(cf, sf, a, b, kspec, gate, gate_col, src, src_col, d, order, out_dtype):
    bsz, L, c = a.shape
    _, n_off, _, bk, _ = kspec.shape
    nb = (n_off + 1) // 2
    return pl.pallas_call(
        functools.partial(_hy_inv_kernel, nb=nb, bk=bk),
        out_shape=jax.ShapeDtypeStruct((bsz, L, c), out_dtype),
        grid=(bsz,),
        in_specs=[
            _resident((bk, bk)),
            _resident((bk, bk)),
            pl.BlockSpec((1, L, c), lambda b: (b, 0, 0)),
            pl.BlockSpec((1, L, c), lambda b: (b, 0, 0)),
            _resident((n_off, 2, bk, c), order),
            pl.BlockSpec((1, L, c), lambda b: (b, 0, gate_col)),
            pl.BlockSpec((1, L, c), lambda b: (b, 0, src_col)),
            pl.BlockSpec((1, 1, c), lambda b: (order, 0, 0)),
        ],
        out_specs=pl.BlockSpec((1, L, c), lambda b: (b, 0, 0)),
        compiler_params=_cparams("parallel"),
        name="hyena_inverse",
    )(cf, sf, a, b, kspec, gate, src, d.reshape(HY_ORDER, 1, c))


def _hyena_blocks(L):
    bk = max(L // HY_BLOCKS, LANES)
    return L // bk


def _hyena(u, dft, taps_w, d):
    (cft, sft), (cf, sf) = dft
    L = u.shape[1]
    c = u.shape[2] // 3
    nb = _hyena_blocks(L)
    kap, rev = _hyena_filter_taps(L, *taps_w)
    kspec = _filter_spectrum(cft, sft, kap, rev, nb)
    a, b = _hy_forward(cft, sft, u, 0, c, nb)
    z = _hy_inverse(cf, sf, a, b, kspec, u, 1, u, 0, d, 0, BF16)
    a, b = _hy_forward(cft, sft, z, 0, c, nb)
    return _hy_inverse(cf, sf, a, b, kspec, u, 2, z, 0, d, 1, BF16)


def _anchor_rows(b, n, a):
    assert n % 8 == 0
    parts = [jnp.broadcast_to(b[s + a:s + a + 1, :], (n, b.shape[1])) for s in range(0, b.shape[0], n)]
    return parts[0] if len(parts) == 1 else jnp.concatenate(parts, axis=0)


def _hgrn_group(q_raw, v, f_logit, lb, st_ref, lv, tri, *, reverse):
    C = HGRN_CHUNK
    G = q_raw.shape[0] // C
    rows = lambda x, i: x[i * C:(i + 1) * C]
    q = q_raw * _sigmoid(q_raw)
    f = lb + (1.0 - lb) * _sigmoid(f_logit)
    kk = 1.0 - f
    g = jnp.log2(f)
    g_top = lax.bitcast_convert_type(
        lax.bitcast_convert_type(g, jnp.uint32) & jnp.uint32(0xFFFF0000), F32)
    gg = jnp.concatenate([g_top.astype(BF16), (g - g_top).astype(BF16)], axis=1)
    bb = [_dot(tri, rows(gg, i)) for i in range(G)]
    part = lambda k: jnp.concatenate(
        [x[k * C:(k + 1) * C, :HEAD_DIM] + x[k * C:(k + 1) * C, HEAD_DIM:] for x in bb], axis=0)
    b = part(0)
    fine = {n_: part(k + 1) for k, n_ in enumerate(HGRN_FINE_LEVELS)}

    qb = q.astype(BF16)
    kb = kk.astype(BF16)
    scores = [jnp.zeros((C, C), F32)] * G
    n = 2
    level = 1
    while n <= C:
        a = n // 2 if reverse else n // 2 - 1
        d = fine[n] if n in fine else b - _anchor_rows(b, n, a)
        e = jnp.exp2(_neg_abs(d)).astype(BF16)
        qe = qb * e
        ke = kb * e
        scores = [jnp.where(lv == level, _dot_nt(rows(qe, i), rows(ke, i)), scores[i]) for i in range(G)]
        n *= 2
        level += 1

    vb = v.astype(BF16)
    diag = jnp.sum(q * kk, axis=-1, keepdims=True) * v
    o = [_dot(scores[i].astype(BF16), rows(vb, i)) + rows(diag, i) for i in range(G)]
    b_end = _anchor_rows(b, C, 0 if reverse else C - 1)
    qd = qb * jnp.exp2(b).astype(BF16)
    kd = kb * jnp.exp2(b_end - b).astype(BF16)
    dec = jnp.exp2(b_end)
    kv = [_dot(rows(v, i).T.astype(BF16), rows(kd, i)) for i in range(G)]
    st = st_ref[...]
    for i in (range(G - 1, -1, -1) if reverse else range(G)):
        o[i] = o[i] + _dot_nt(rows(qd, i), st.astype(BF16))
        st = st * dec[i * C:i * C + 1] + kv[i]
    st_ref[...] = st
    return jnp.concatenate(o, axis=0)


def _hgrn_kernel(qc, ic, gc, fc, bc, ql, il, gl, fl, bl, lb_ref, nw_ref, oc_ref, ol_ref,
                 of_s, ob_s, stf, stb, *, n_ctx, n_lat):
    C = HGRN_CHUNK
    row = lax.broadcasted_iota(jnp.int32, (C, C), 0)
    col = lax.broadcasted_iota(jnp.int32, (C, C), 1)
    x = row ^ col
    lvl = jnp.zeros((C, C), jnp.int32)
    n = 1
    while n < C:
        lvl = lvl + (x >= n).astype(jnp.int32)
        n *= 2
    lv_f = jnp.where(row > col, lvl, 0)
    lv_b = jnp.where(row < col, lvl, 0)

    def cum_matrix(reverse):
        cum = lambda r: (col >= r) if reverse else (col <= r)
        mats = [cum(row).astype(F32)]
        for n_ in HGRN_FINE_LEVELS:
            anchor = row - row % n_ + (n_ // 2 if reverse else n_ // 2 - 1)
            mats.append(mats[0] - cum(anchor).astype(F32))
        return jnp.concatenate(mats, axis=0).astype(BF16)

    tri_f = cum_matrix(False)
    tri_b = cum_matrix(True)
    lb_f = lb_ref[0, 0:1, :]
    lb_b = lb_ref[0, 1:2, :]
    stf[...] = jnp.zeros_like(stf)
    stb[...] = jnp.zeros_like(stb)

    def run(q_ref, i_ref, f_ref, b_ref, n_chunks, base):
        G = min(n_chunks, HGRN_GROUP)
        R = G * C
        n_groups = n_chunks // G

        def body(j, carry):
            off_f = pl.multiple_of(j * R, R)
            off_b = pl.multiple_of((n_groups - 1 - j) * R, R)
            rf = pl.ds(off_f, R)
            rb = pl.ds(off_b, R)
            ld = lambda ref, r: ref[0, r, :].astype(F32)
            of_s[pl.ds(base + off_f, R), :] = _hgrn_group(
                ld(q_ref, rf), ld(i_ref, rf), ld(f_ref, rf), lb_f, stf, lv_f, tri_f, reverse=False)
            ob_s[pl.ds(base + off_b, R), :] = _hgrn_group(
                ld(q_ref, rb), ld(i_ref, rb), ld(b_ref, rb), lb_b, stb, lv_b, tri_b, reverse=True)
            return carry
        lax.fori_loop(0, n_groups, body, 0)

    run(qc, ic, fc, bc, n_ctx, 0)
    run(ql, il, fl, bl, n_lat, n_ctx * C)

    nw = nw_ref[...]

    def finish(g_ref, o_ref, n_chunks, base):
        R = C * math.gcd(n_chunks, 4)

        def body(j, carry):
            r = pl.ds(pl.multiple_of(j * R, R), R)
            rs = pl.ds(pl.multiple_of(base + j * R, C), R)
            o = of_s[rs, :] + ob_s[rs, :]
            o = o * lax.rsqrt(jnp.mean(o * o, axis=-1, keepdims=True) + NORM_EPS) * nw
            g = g_ref[0, r, :].astype(F32)
            o_ref[0, r, :] = (o * (g * _sigmoid(g))).astype(o_ref.dtype)
            return carry
        lax.fori_loop(0, n_chunks * C // R, body, 0)

    finish(gc, oc_ref, n_ctx, 0)
    finish(gl, ol_ref, n_lat, n_ctx * C)


def _hgrn(hg_ctx, hg_lat, lb, norm_w):
    bsz, lc, n5 = hg_ctx.shape
    ll = hg_lat.shape[1]
    dg = n5 // 5
    nh = dg // HEAD_DIM
    C = HGRN_CHUNK
    assert lc % C == 0 and ll % C == 0
    assert all((n // C) % min(n // C, HGRN_GROUP) == 0 for n in (lc, ll))

    def slab(L, part):
        return pl.BlockSpec((1, L, HEAD_DIM), lambda b, h: (b, 0, part * nh + h))

    return pl.pallas_call(
        functools.partial(_hgrn_kernel, n_ctx=lc // C, n_lat=ll // C),
        out_shape=(jax.ShapeDtypeStruct((bsz, lc, dg), BF16), jax.ShapeDtypeStruct((bsz, ll, dg), BF16)),
        grid=(bsz, nh),
        in_specs=[slab(lc, p) for p in range(5)] + [slab(ll, p) for p in range(5)] + [
            pl.BlockSpec((1, 2, HEAD_DIM), lambda b, h: (h, 0, 0)),
            pl.BlockSpec((1, HEAD_DIM), lambda b, h: (0, 0)),
        ],
        out_specs=(pl.BlockSpec((1, lc, HEAD_DIM), lambda b, h: (b, 0, h)),
                   pl.BlockSpec((1, ll, HEAD_DIM), lambda b, h: (b, 0, h))),
        scratch_shapes=[
            pltpu.VMEM((lc + ll, HEAD_DIM), F32),
            pltpu.VMEM((lc + ll, HEAD_DIM), F32),
            pltpu.VMEM((HEAD_DIM, HEAD_DIM), F32),
            pltpu.VMEM((HEAD_DIM, HEAD_DIM), F32),
        ],
        compiler_params=_cparams("parallel", "parallel"),
        name="hgrn2",
    )(*([hg_ctx] * 5), *([hg_lat] * 5), lb, norm_w.reshape(1, HEAD_DIM))


def _first_argmax(vals, idx, sentinel):
    m = jnp.max(vals, axis=0, keepdims=True)
    first = jnp.min(jnp.where(vals == m, idx, sentinel), axis=0, keepdims=True)
    return m, first


def _moe_gates(logits_t, rbias):
    ne, tm = logits_t.shape
    neg = -jnp.inf
    mx = jnp.max(logits_t, axis=0, keepdims=True)
    ex = jnp.exp(logits_t - mx)
    scores = ex / jnp.sum(ex, axis=0, keepdims=True)
    sel = scores + rbias
    eidx = lax.broadcasted_iota(jnp.int32, (ne, tm), 0)
    grp = eidx // EXPERTS_PER_GROUP
    best_s = jnp.full((1, tm), neg, F32)
    best_g = jnp.zeros((1, tm), jnp.int32)
    for gi in range(N_GROUPS):
        mg = jnp.where(grp == gi, sel, neg)
        m1, i1 = _first_argmax(mg, eidx, ne)
        m2 = jnp.max(jnp.where(eidx == i1, neg, mg), axis=0, keepdims=True)
        gs = m1 + m2
        upd = gs > best_s
        best_g = jnp.where(upd, gi, best_g)
        best_s = jnp.where(upd, gs, best_s)
    sg = jnp.where(grp == best_g, sel, neg)
    _, i1 = _first_argmax(sg, eidx, ne)
    _, i2 = _first_argmax(jnp.where(eidx == i1, neg, sg), eidx, ne)
    chosen = (eidx == i1) | (eidx == i2)
    w = jnp.where(chosen, scores, 0.0)
    return w / jnp.sum(w, axis=0, keepdims=True), best_g


def _mix_moe_kernel(x_ref, hy_ref, hg_ref, mod_ref, w_ref, rw_ref, rb_ref, wg_ref, wu_ref, wd_ref,
                    o_ref, t_s, g4_s, *, cap):
    c = hy_ref.shape[-1]
    mix = _dot(hy_ref[0], w_ref[:c, :]) + _dot(hg_ref[0], w_ref[c:, :])
    m = mod_ref[0]
    x1 = x_ref[0] + m[2:3] * mix
    o_ref[0] = x1
    xn = x1 * lax.rsqrt(jnp.mean(x1 * x1, axis=-1, keepdims=True) + NORM_EPS)
    t = xn * (1.0 + m[4:5]) + m[3:4]
    t_hi = t.astype(BF16)
    t_s[...] = t_hi
    t_lo = (t - t_hi.astype(F32)).astype(BF16)
    r = _dot(t_hi, rw_ref[...])
    lg = r[:, :LANES] + r[:, LANES:] + _dot(t_lo, rw_ref[:, :LANES])
    ne, _, f = wg_ref.shape
    tm = t.shape[0]
    per_group = ne // N_GROUPS
    gates_t, best_g = _moe_gates(lg.T[:ne], rb_ref[...])

    row8 = lax.broadcasted_iota(jnp.int32, (8, tm), 0)
    member = row8 == best_g
    ri = lax.broadcasted_iota(jnp.int32, (tm, tm), 0)
    ci = lax.broadcasted_iota(jnp.int32, (tm, tm), 1)
    before = _dot(member.astype(BF16), (ri < ci).astype(BF16))
    rank_row = jnp.sum(jnp.where(member, before, 0.0), axis=0, keepdims=True)
    grp_row = best_g.astype(F32)
    count = jnp.sum(member.astype(F32), axis=1, keepdims=True)
    g4_t = gates_t
    for g in range(1, N_GROUPS):
        g4_t = g4_t + pltpu.roll(gates_t, g * per_group, 0)
    row_e = lax.broadcasted_iota(jnp.int32, (ne, tm), 0)
    info_t = jnp.where(row_e < per_group, g4_t,
                       jnp.where(row_e == per_group, rank_row, jnp.where(row_e == per_group + 1, grp_row, 0.0)))
    info = jnp.concatenate([info_t, jnp.zeros((LANES - ne, tm), F32)], axis=0).T
    g4_s[...] = info.astype(BF16)
    rank_col = info[:, per_group:per_group + 1]
    grp_col = info[:, per_group + 1:per_group + 2]
    g2 = m[5:6]

    slot_col = lax.broadcasted_iota(jnp.int32, (cap, 1), 0).astype(F32)
    slot_row = lax.broadcasted_iota(jnp.int32, (1, cap), 1).astype(F32)
    for g in range(N_GROUPS):
        n_g = count[g, 0].astype(jnp.int32)

        def body(s, carry, g=g):
            base = (s * cap).astype(F32)
            pick = ((rank_row - base == slot_col) & (grp_row == float(g))).astype(BF16)
            xg = _dot(pick, t_s[...]).astype(BF16)
            gs = _dot(pick, g4_s[...])
            acts = []
            for j in range(per_group):
                e = g * per_group + j
                hgate = _dot(xg, wg_ref[e])
                hup = _dot(xg, wu_ref[e])
                acts.append((hgate * _sigmoid(hgate) * hup * gs[:, j:j + 1]).astype(BF16))
            y = _dot(jnp.concatenate(acts, axis=1), wd_ref[g * per_group * f:(g + 1) * per_group * f, :])
            put = ((rank_col - base == slot_row) & (grp_col == float(g))).astype(BF16)
            o_ref[0] += _dot(put, (y * g2).astype(BF16))
            return carry

        lax.fori_loop(0, (n_g + cap - 1) // cap, body, 0)


def _mix_moe(x, hy, hg, mod_l, mod_row, layer, w_out, router_w, rbias, wg, wu, wd, *, tm):
    bsz, L, d = x.shape
    c = hy.shape[-1]
    cg = hg.shape[-1]
    _, ne, _, f = wg.shape
    rw_hi = router_w.astype(BF16)
    rw_lo = (router_w - rw_hi.astype(F32)).astype(BF16)
    rw = jnp.zeros((d, 2 * LANES), BF16).at[:, :ne].set(rw_hi).at[:, LANES:LANES + ne].set(rw_lo)
    cap = 5 * tm // 16
    assert cap % 8 == 0
    return pl.pallas_call(
        functools.partial(_mix_moe_kernel, cap=cap),
        out_shape=jax.ShapeDtypeStruct((bsz, L, d), F32),
        grid=(bsz, L // tm),
        in_specs=[
            pl.BlockSpec((1, tm, d), lambda b, i: (b, i, 0)),
            pl.BlockSpec((1, tm, c), lambda b, i: (b, i, 0)),
            pl.BlockSpec((1, tm, cg), lambda b, i: (b, i, 0)),
            pl.BlockSpec((1, N_MOD, d), lambda b, i: (mod_row(b), 0, 0)),
            _resident((c + cg, d), layer),
            _resident((d, 2 * LANES)),
            _resident((ne, 1)),
            _resident((ne, d, f), layer),
            _resident((ne, d, f), layer),
            _resident((ne * f, d), layer),
        ],
        out_specs=pl.BlockSpec((1, tm, d), lambda b, i: (b, i, 0)),
        scratch_shapes=[pltpu.VMEM((tm, d), BF16), pltpu.VMEM((tm, LANES), BF16)],
        compiler_params=_cparams("parallel", "parallel"),
        name="mix_moe",
    )(x, hy, hg, mod_l, w_out, rw, rbias.reshape(ne, 1), wg, wu, wd.reshape(-1, ne * f, d))


def _final_norm_kernel(x_ref, w_ref, o_ref):
    x = x_ref[0]
    o_ref[0] = x * lax.rsqrt(jnp.mean(x * x, axis=-1, keepdims=True) + NORM_EPS) * w_ref[...]


def _final_norm(x, w, *, tm):
    bsz, L, d = x.shape
    return pl.pallas_call(
        _final_norm_kernel,
        out_shape=jax.ShapeDtypeStruct(x.shape, F32),
        grid=(bsz, L // tm),
        in_specs=[pl.BlockSpec((1, tm, d), lambda b, i: (b, i, 0)), pl.BlockSpec((1, d), lambda b, i: (0, 0))],
        out_specs=pl.BlockSpec((1, tm, d), lambda b, i: (b, i, 0)),
        compiler_params=_cparams("parallel", "parallel"),
        name="final_norm",
    )(x, w.reshape(1, d))


def kernel(x, c, ctx, c_ctx, w_mod, b_mod, w_in, w_out, hy_conv_w, hy_conv_b, hy_w1, hy_b1, hy_w2, hy_b2, hy_w3, hy_b3, hy_bias, hgrn_lower_bounds, hgrn_norm_w, router_w, router_bias, moe_w_gate, moe_w_up, moe_w_down, final_norm_w):
    bsz, seq, d = x.shape
    ctx_len = ctx.shape[1]
    depth = w_mod.shape[0]
    dg = hgrn_lower_bounds.shape[-1]
    nh = dg // HEAD_DIM
    assert bsz + 1 <= MOD_ROWS and seq % GRID_W == 0

    cc = jnp.zeros((MOD_ROWS, d), F32).at[:bsz].set(c).at[bsz].set(c_ctx)
    mod = _modulation(cc, w_mod, b_mod).reshape(depth, MOD_ROWS, N_MOD, d)
    lat_row = lambda b: b
    ctx_row = lambda b: bsz

    lb_soft = jax.nn.softmax(hgrn_lower_bounds.astype(F32), axis=1)
    lower = jnp.cumsum(lb_soft, axis=1) - lb_soft[:, :1]
    lower = lower.reshape(2, depth, nh, HEAD_DIM).transpose(1, 2, 0, 3)

    dft_lat = _dft_matrices(seq // _hyena_blocks(seq))
    dft_ctx = _dft_matrices(ctx_len // _hyena_blocks(ctx_len))
    tm_lat = min(seq, 1024)
    tm_moe = min(seq, 512)

    w_in_b = w_in.astype(BF16)
    moe_w = (w_out.astype(BF16), router_w, router_bias,
             moe_w_gate.astype(BF16), moe_w_up.astype(BF16), moe_w_down.astype(BF16))

    xc = ctx
    for l in range(depth):
        last = l == depth - 1
        u_lat, hg_lat = _inproj(x, mod[l], lat_row, w_in_b, l, hy_conv_w[l], hy_conv_b[l],
                                period=GRID_W, tm=tm_lat)
        u_ctx, hg_ctx = _inproj(xc, mod[l], ctx_row, w_in_b, l, hy_conv_w[l], hy_conv_b[l],
                                period=ctx_len, tm=ctx_len)
        o_ctx, o_lat = _hgrn(hg_ctx, hg_lat, lower[l], hgrn_norm_w[l])
        taps_w = (hy_w1[l], hy_b1[l], hy_w2[l], hy_b2[l], hy_w3[l], hy_b3[l])
        hy_lat = _hyena(u_lat, dft_lat, taps_w, hy_bias[l])
        x = _mix_moe(x, hy_lat, o_lat, mod[l], lat_row, l, *moe_w, tm=tm_moe)
        if not last:
            hy_ctx = _hyena(u_ctx, dft_ctx, taps_w, hy_bias[l])
            xc = _mix_moe(xc, hy_ctx, o_ctx, mod[l], ctx_row, l, *moe_w, tm=ctx_len)

    return _final_norm(x, final_norm_w, tm=tm_lat)
```

```python
import functools
import math

import numpy as np
import jax
import jax.numpy as jnp
from jax import lax
from jax.experimental import pallas as pl
from jax.experimental.pallas import tpu as pltpu

F32 = jnp.float32
BF16 = jnp.bfloat16
HIGHEST = lax.Precision.HIGHEST

GRID_W = 64
NORM_EPS = 1e-6
N_MOD = 6
HY_ORDER = 2
HY_BANDS = 16
HY_TARGET = 1e-2
HY_FAST_PCT = 0.3
HY_SLOW_PCT = 1.5
HEAD_DIM = 128
HY_BLOCKS = 2
N_EXPERTS = 16
N_GROUPS = 4
EXPERTS_PER_GROUP = N_EXPERTS // N_GROUPS
LANES = 128
MOD_ROWS = 16
HGRN_CHUNK = 128
HGRN_GROUP = 16
HGRN_FINE_LEVELS = (2, 4)
VMEM_LIMIT = 56 << 20


def _cparams(*sem, flags=None):
    return pltpu.CompilerParams(dimension_semantics=sem, vmem_limit_bytes=VMEM_LIMIT, flags=flags)


def _sigmoid(x):
    return 1.0 / (1.0 + jnp.exp2(x * (-1.0 / math.log(2.0))))


def _dot(a, b, **kw):
    return jnp.dot(a, b, preferred_element_type=F32, **kw)


def _resident(shape, layer=None):
    if layer is None:
        return pl.BlockSpec(shape, lambda *_: (0,) * len(shape), pipeline_mode=pl.Buffered(1))
    return pl.BlockSpec((None,) + tuple(shape), lambda *_: (layer,) + (0,) * len(shape),
                        pipeline_mode=pl.Buffered(1))


def _dot_nt(a, b):
    return lax.dot_general(a, b, (((1,), (1,)), ((), ())), preferred_element_type=F32)


def _mod_kernel(c_ref, w_ref, b_ref, o_ref):
    c = c_ref[...]
    cs = c * _sigmoid(c)
    o_ref[0] = _dot(cs, w_ref[0], precision=HIGHEST) + b_ref[0]


def _modulation(cc, w_mod, b_mod):
    depth, d, n = w_mod.shape
    tn = n // 3
    return pl.pallas_call(
        _mod_kernel,
        out_shape=jax.ShapeDtypeStruct((depth, MOD_ROWS, n), F32),
        grid=(depth, n // tn),
        in_specs=[
            pl.BlockSpec((MOD_ROWS, d), lambda l, j: (0, 0)),
            pl.BlockSpec((1, d, tn), lambda l, j: (l, 0, j)),
            pl.BlockSpec((1, 1, tn), lambda l, j: (l, 0, j)),
        ],
        out_specs=pl.BlockSpec((1, MOD_ROWS, tn), lambda l, j: (l, 0, j)),
        compiler_params=_cparams("parallel", "parallel"),
        name="modulation",
    )(cc, w_mod, b_mod.reshape(depth, 1, n))


def _dft_kernel(ac_ref, as_ref, bc_ref, bs_ref, c_ref, s_ref, *, n1):
    bc = bc_ref[...]
    bs = bs_ref[...]
    ac = ac_ref[...]
    as_ = as_ref[...]
    for j in range(n1):
        a_c = ac[:, j:j + 1]
        a_s = as_[:, j:j + 1]
        c_ref[:, j * LANES:(j + 1) * LANES] = (a_c * bc - a_s * bs).astype(BF16)
        s_ref[:, j * LANES:(j + 1) * LANES] = (a_s * bc + a_c * bs).astype(BF16)


def _dft_tables(L):
    n1 = L // LANES
    period = 4 * L
    r = np.arange(L, dtype=np.int64)[:, None]
    c1 = np.arange(n1, dtype=np.int64)[None, :]
    c0 = np.arange(LANES, dtype=np.int64)[None, :]

    def cs(phase):
        ang = (phase % period).astype(np.float64) * (2.0 * np.pi / period)
        return np.cos(ang).astype(np.float32), np.sin(ang).astype(np.float32)

    fa = cs((2 * r + 1) * (LANES * c1))
    fb = cs((2 * r + 1) * c0)
    ta = cs((2 * LANES * c1) * r)
    tb = cs((2 * c0 + 1) * r)
    return (fa, fb), (ta, tb)


def _dft_matrices(L):
    n1 = L // LANES
    tr = min(L, 256)
    outs = []
    for (ac, as_), (bc, bs) in _dft_tables(L):
        c, s = pl.pallas_call(
            functools.partial(_dft_kernel, n1=n1),
            out_shape=(jax.ShapeDtypeStruct((L, L), BF16), jax.ShapeDtypeStruct((L, L), BF16)),
            grid=(L // tr,),
            in_specs=[
                pl.BlockSpec((tr, n1), lambda i: (i, 0)),
                pl.BlockSpec((tr, n1), lambda i: (i, 0)),
                pl.BlockSpec((tr, LANES), lambda i: (i, 0)),
                pl.BlockSpec((tr, LANES), lambda i: (i, 0)),
            ],
            out_specs=(pl.BlockSpec((tr, L), lambda i: (i, 0)), pl.BlockSpec((tr, L), lambda i: (i, 0))),
            compiler_params=_cparams("parallel"),
            name="dft_tables",
        )(jnp.asarray(ac), jnp.asarray(as_), jnp.asarray(bc), jnp.asarray(bs))
        outs.append((c, s))
    return outs


def _inproj_kernel(x_ref, mod_ref, w_ref, cw_ref, cb_ref, u_ref, hg_ref, *, period, n_hy, tn):
    x = x_ref[0]
    tm = x.shape[0]
    xn = x * lax.rsqrt(jnp.mean(x * x, axis=-1, keepdims=True) + NORM_EPS)
    m = mod_ref[0]
    a = (xn * (1.0 + m[1:2]) + m[0:1]).astype(BF16)
    pos = lax.broadcasted_iota(jnp.int32, (tm, 1), 0) % period
    first = pos == 0
    last = pos == period - 1
    n_total = w_ref.shape[1]
    for j in range(n_total // tn):
        p = _dot(a, w_ref[:, j * tn:(j + 1) * tn])
        if j * tn < n_hy:
            cw = cw_ref[:, j * tn:(j + 1) * tn]
            prev = jnp.where(first, 0.0, pltpu.roll(p, 1, 0))
            nxt = jnp.where(last, 0.0, pltpu.roll(p, tm - 1, 0))
            u_ref[0, :, j * tn:(j + 1) * tn] = (
                prev * cw[0:1] + p * cw[1:2] + nxt * cw[2:3] + cb_ref[:, j * tn:(j + 1) * tn]
            ).astype(u_ref.dtype)
        else:
            hg_ref[0, :, j * tn - n_hy:(j + 1) * tn - n_hy] = p.astype(hg_ref.dtype)


def _inproj(x, mod_l, mod_row, w_in, layer, conv_w, conv_b, *, period, tm):
    bsz, L, d = x.shape
    n_hy = conv_w.shape[1]
    n_all = w_in.shape[-1]
    tn = n_hy // 3
    assert tm % period == 0 or period == L == tm
    return pl.pallas_call(
        functools.partial(_inproj_kernel, period=period, n_hy=n_hy, tn=tn),
        out_shape=(jax.ShapeDtypeStruct((bsz, L, n_hy), BF16),
                   jax.ShapeDtypeStruct((bsz, L, n_all - n_hy), BF16)),
        grid=(bsz, L // tm),
        in_specs=[
            pl.BlockSpec((1, tm, d), lambda b, i: (b, i, 0)),
            pl.BlockSpec((1, N_MOD, d), lambda b, i: (mod_row(b), 0, 0)),
            _resident((d, n_all), layer),
            _resident((3, n_hy)),
            _resident((1, n_hy)),
        ],
        out_specs=(pl.BlockSpec((1, tm, n_hy), lambda b, i: (b, i, 0)),
                   pl.BlockSpec((1, tm, n_all - n_hy), lambda b, i: (b, i, 0))),
        compiler_params=_cparams("parallel", "parallel"),
        name="inproj",
    )(x, mod_l, w_in, conv_w, conv_b.reshape(1, n_hy))


def _rev_rows(x):
    nblk = x.shape[0] // LANES
    r = lax.broadcasted_iota(jnp.int32, (LANES, LANES), 0)
    c_ = lax.broadcasted_iota(jnp.int32, (LANES, LANES), 1)
    exch = (r + c_ == LANES - 1).astype(BF16)
    xb = x.astype(BF16)
    return jnp.concatenate(
        [_dot(exch, xb[(nblk - 1 - i) * LANES:(nblk - i) * LANES]) for i in range(nblk)], axis=0)


def _filter_kernel(z_ref, w1_ref, b1_ref, w2_ref, b2_ref, w3f_ref, b3f_ref, w3b_ref, b3b_ref,
                   dl_ref, kap_ref, rev_ref, h_s):
    z = z_ref[...]
    L = z.shape[0]

    @pl.when((pl.program_id(0) == 0) & (pl.program_id(1) == 0))
    def _():
        h1 = jnp.sin(_dot(z, w1_ref[...], precision=HIGHEST) + b1_ref[...])
        h_s[...] = jnp.sin(_dot(h1, w2_ref[...], precision=HIGHEST) + b2_ref[...])

    h = h_s[...]
    hf = _dot(h, w3f_ref[...], precision=HIGHEST) + b3f_ref[...]
    hb = _dot(h, w3b_ref[...], precision=HIGHEST) + b3b_ref[...]
    win = jnp.exp(-z[:, 0:1] * dl_ref[...])
    hf = hf * win
    hb = hb * win
    nrm = (jnp.sum(jnp.abs(hf), axis=0, keepdims=True)
           + jnp.sum(jnp.abs(hb), axis=0, keepdims=True))
    inv = 1.0 / nrm
    hf = hf * inv
    hb = hb * inv
    first = lax.broadcasted_iota(jnp.int32, (L, 1), 0) == 0
    down1 = lambda y: jnp.where(first, 0.0, pltpu.roll(y, 1, 0))
    kap_ref[0, 0:L, :] = down1(_rev_rows(hb)).astype(BF16)
    kap_ref[0, L:2 * L, :] = hf.astype(BF16)
    rev_ref[0, 0:L, :] = down1(_rev_rows(hf)).astype(BF16)
    rev_ref[0, L:2 * L, :] = jnp.where(first, hf[0:1], hb).astype(BF16)


def _hyena_filter_taps(L, w1, b1, w2, b2, w3, b3):
    nfeat, hid = w1.shape
    c = w3.shape[1] // (2 * HY_ORDER)
    tc = min(c, 256)
    nct = c // tc
    t = jnp.linspace(0.0, 1.0, L, dtype=F32)
    n = jnp.arange(L, dtype=F32)
    freqs = jnp.linspace(1e-4, HY_BANDS - 1, HY_BANDS, dtype=F32)
    ang = (2.0 * math.pi / L) * n[:, None] * freqs[None, :]
    z = jnp.concatenate([t[:, None], jnp.cos(ang), -jnp.sin(ang)], axis=-1)
    z = jnp.pad(z, ((0, 0), (0, LANES - nfeat)))
    w1p = jnp.pad(w1, ((0, LANES - nfeat), (0, 0)))
    deltas = jnp.abs(jnp.linspace(math.log(HY_TARGET) / HY_FAST_PCT, math.log(HY_TARGET) / HY_SLOW_PCT,
                                  c, dtype=F32)).reshape(1, c)
    full = lambda shape: pl.BlockSpec(shape, lambda o, j: (0,) * len(shape))
    out = jax.ShapeDtypeStruct((HY_ORDER, 2 * L, c), BF16)
    return pl.pallas_call(
        _filter_kernel,
        out_shape=(out, out),
        grid=(HY_ORDER, nct),
        in_specs=[
            full((L, LANES)), full((LANES, hid)), full((1, hid)), full((hid, hid)), full((1, hid)),
            pl.BlockSpec((hid, tc), lambda o, j: (0, o * 2 * nct + j)),
            pl.BlockSpec((1, tc), lambda o, j: (0, o * 2 * nct + j)),
            pl.BlockSpec((hid, tc), lambda o, j: (0, o * 2 * nct + nct + j)),
            pl.BlockSpec((1, tc), lambda o, j: (0, o * 2 * nct + nct + j)),
            pl.BlockSpec((1, tc), lambda o, j: (0, j)),
        ],
        out_specs=(pl.BlockSpec((1, 2 * L, tc), lambda o, j: (o, 0, j)),
                   pl.BlockSpec((1, 2 * L, tc), lambda o, j: (o, 0, j))),
        scratch_shapes=[pltpu.VMEM((L, hid), F32)],
        compiler_params=_cparams("arbitrary", "arbitrary"),
        name="hyena_filter",
    )(z, w1p, b1.reshape(1, hid), w2, b2.reshape(1, hid), w3, b3.reshape(1, -1), w3, b3.reshape(1, -1),
      deltas)


def _kspec_kernel(cf_ref, sf_ref, pos_ref, neg_ref, o_ref, *, scale):
    pos = pos_ref[0].astype(F32)
    neg = neg_ref[0].astype(F32)
    first = lax.broadcasted_iota(jnp.int32, (pos.shape[0], 1), 0) == 0
    neg = jnp.where(first, 0.0, neg)
    o_ref[0, 0, 0] = _dot(cf_ref[...], (pos + neg).astype(BF16)) * scale
    o_ref[0, 0, 1] = _dot(sf_ref[...], (pos - neg).astype(BF16)) * scale


def _filter_spectrum(cft, sft, kap, rev, nb):
    _, two_l, c = kap.shape
    bk = two_l // (2 * nb)
    n_off = 2 * nb - 1
    return pl.pallas_call(
        functools.partial(_kspec_kernel, scale=1.0 / bk),
        out_shape=jax.ShapeDtypeStruct((HY_ORDER, n_off, 2, bk, c), F32),
        grid=(HY_ORDER, n_off),
        in_specs=[
            _resident((bk, bk)),
            _resident((bk, bk)),
            pl.BlockSpec((1, bk, c), lambda o, k: (o, k + 1, 0)),
            pl.BlockSpec((1, bk, c), lambda o, k: (o, 2 * nb - 1 - k, 0)),
        ],
        out_specs=pl.BlockSpec((1, 1, 2, bk, c), lambda o, k: (o, k, 0, 0, 0)),
        compiler_params=_cparams("parallel", "parallel"),
        name="hyena_filter_spectrum",
    )(cft, sft, kap, rev)


def _hy_fwd_kernel(cf_ref, sf_ref, v_ref, a_ref, b_ref, *, nb, bk):
    for j in range(nb):
        rows = slice(j * bk, (j + 1) * bk)
        v = v_ref[0, rows, :].astype(BF16)
        a_ref[0, rows, :] = _dot(cf_ref[...], v).astype(BF16)
        b_ref[0, rows, :] = _dot(sf_ref[...], v).astype(BF16)


def _hy_forward(cft, sft, src, src_col, c, nb):
    bsz, L, _ = src.shape
    bk = L // nb
    out = jax.ShapeDtypeStruct((bsz, L, c), BF16)
    return pl.pallas_call(
        functools.partial(_hy_fwd_kernel, nb=nb, bk=bk),
        out_shape=(out, out),
        grid=(bsz,),
        in_specs=[
            _resident((bk, bk)),
            _resident((bk, bk)),
            pl.BlockSpec((1, L, c), lambda b: (b, 0, src_col)),
        ],
        out_specs=(pl.BlockSpec((1, L, c), lambda b: (b, 0, 0)), pl.BlockSpec((1, L, c), lambda b: (b, 0, 0))),
        compiler_params=_cparams("parallel"),
        name="hyena_spectrum",
    )(cft, sft, src)


def _hy_inv_kernel(cf_ref, sf_ref, a_ref, b_ref, k_ref, gate_ref, src_ref, d_ref, o_ref, *, nb, bk):
    for i in range(nb):
        p = q = None
        for j in range(nb):
            rows = slice(j * bk, (j + 1) * bk)
            a = a_ref[0, rows, :].astype(F32)
            b = b_ref[0, rows, :].astype(F32)
            kr = k_ref[i - j + nb - 1, 0]
            ks = k_ref[i - j + nb - 1, 1]
            pj = a * kr - b * ks
            qj = a * ks + b * kr
            p = pj if p is None else p + pj
            q = qj if q is None else q + qj
        conv = _dot(cf_ref[...], p.astype(BF16)) + _dot(sf_ref[...], q.astype(BF16))
        rows = slice(i * bk, (i + 1) * bk)
        o_ref[0, rows, :] = (gate_ref[0, rows, :] * (conv + src_ref[0, rows, :] * d_ref[0])).astype(o_ref.dtype)


def _hy_inverse(cf, sf, a, b, kspec, gate, gate_col, src, src_col, d, order, out_dtype):
    bsz, L, c = a.shape
    _, n_off, _, bk, _ = kspec.shape
    nb = (n_off + 1) // 2
    return pl.pallas_call(
        functools.partial(_hy_inv_kernel, nb=nb, bk=bk),
        out_shape=jax.ShapeDtypeStruct((bsz, L, c), out_dtype),
        grid=(bsz,),
        in_specs=[
            _resident((bk, bk)),
            _resident((bk, bk)),
            pl.BlockSpec((1, L, c), lambda b: (b, 0, 0)),
            pl.BlockSpec((1, L, c), lambda b: (b, 0, 0)),
            _resident((n_off, 2, bk, c), order),
            pl.BlockSpec((1, L, c), lambda b: (b, 0, gate_col)),
            pl.BlockSpec((1, L, c), lambda b: (b, 0, src_col)),
            pl.BlockSpec((1, 1, c), lambda b: (order, 0, 0)),
        ],
        out_specs=pl.BlockSpec((1, L, c), lambda b: (b, 0, 0)),
        compiler_params=_cparams("parallel"),
        name="hyena_inverse",
    )(cf, sf, a, b, kspec, gate, src, d.reshape(HY_ORDER, 1, c))


def _hyena_blocks(L):
    bk = max(L // HY_BLOCKS, LANES)
    return L // bk


def _hyena(u, dft, taps_w, d):
    (cft, sft), (cf, sf) = dft
    L = u.shape[1]
    c = u.shape[2] // 3
    nb = _hyena_blocks(L)
    kap, rev = _hyena_filter_taps(L, *taps_w)
    kspec = _filter_spectrum(cft, sft, kap, rev, nb)
    a, b = _hy_forward(cft, sft, u, 0, c, nb)
    z = _hy_inverse(cf, sf, a, b, kspec, u, 1, u, 0, d, 0, BF16)
    a, b = _hy_forward(cft, sft, z, 0, c, nb)
    return _hy_inverse(cf, sf, a, b, kspec, u, 2, z, 0, d, 1, BF16)


def _anchor_rows(b, n, a):
    assert n % 8 == 0
    parts = [jnp.broadcast_to(b[s + a:s + a + 1, :], (n, b.shape[1])) for s in range(0, b.shape[0], n)]
    return parts[0] if len(parts) == 1 else jnp.concatenate(parts, axis=0)


def _neg_abs_anchor_diff(b, n, a, reverse):
    h = n // 2
    m = _anchor_rows(b, n, a)
    if h % 8:
        bits = lax.bitcast_convert_type(b - m, jnp.uint32) | jnp.uint32(0x80000000)
        return lax.bitcast_convert_type(bits, F32)
    pieces = []
    for s in range(0, b.shape[0], h):
        first_half = (s // h) % 2 == 0
        upstream = first_half != reverse
        x, y = (m, b) if upstream else (b, m)
        pieces.append(x[s:s + h] - y[s:s + h])
    return jnp.concatenate(pieces, axis=0)


def _hgrn_group(q_raw, v, f_logit, lb, st_ref, lv, tri, *, reverse):
    C = HGRN_CHUNK
    G = q_raw.shape[0] // C
    rows = lambda x, i: x[i * C:(i + 1) * C]
    q = q_raw * _sigmoid(q_raw)
    f = lb + (1.0 - lb) * _sigmoid(f_logit)
    kk = 1.0 - f
    g = jnp.log2(f)
    g_top = lax.bitcast_convert_type(
        lax.bitcast_convert_type(g, jnp.uint32) & jnp.uint32(0xFFFF0000), F32)
    gg = jnp.concatenate([g_top.astype(BF16), (g - g_top).astype(BF16)], axis=1)
    bb = [_dot(tri, rows(gg, i)) for i in range(G)]
    part = lambda k: jnp.concatenate(
        [x[k * C:(k + 1) * C, :HEAD_DIM] + x[k * C:(k + 1) * C, HEAD_DIM:] for x in bb], axis=0)
    b = part(0)
    fine = {n_: part(k + 1) for k, n_ in enumerate(HGRN_FINE_LEVELS)}

    qb = q.astype(BF16)
    kb = kk.astype(BF16)
    scores = [jnp.zeros((C, C), F32)] * G
    n = 2
    level = 1
    while n <= C:
        a = n // 2 if reverse else n // 2 - 1
        d = fine[n] if n in fine else _neg_abs_anchor_diff(b, n, a, reverse)
        e = jnp.exp2(d).astype(BF16)
        qe = qb * e
        ke = kb * e
        scores = [jnp.where(lv == level, _dot_nt(rows(qe, i), rows(ke, i)), scores[i]) for i in range(G)]
        n *= 2
        level += 1

    vb = v.astype(BF16)
    diag = jnp.sum(q * kk, axis=-1, keepdims=True) * v
    o = [_dot(scores[i].astype(BF16), rows(vb, i)) + rows(diag, i) for i in range(G)]
    b_end = _anchor_rows(b, C, 0 if reverse else C - 1)
    qd = qb * jnp.exp2(b).astype(BF16)
    kd = kb * jnp.exp2(b_end - b).astype(BF16)
    dec = jnp.exp2(b_end)
    kv = [_dot(rows(v, i).T.astype(BF16), rows(kd, i)) for i in range(G)]
    st = st_ref[...]
    for i in (range(G - 1, -1, -1) if reverse else range(G)):
        o[i] = o[i] + _dot_nt(rows(qd, i), st.astype(BF16))
        st = st * dec[i * C:i * C + 1] + kv[i]
    st_ref[...] = st
    return jnp.concatenate(o, axis=0)


def _hgrn_kernel(qc, ic, gc, fc, bc, ql, il, gl, fl, bl, lb_ref, nw_ref, oc_ref, ol_ref,
                 of_s, ob_s, stf, stb, *, n_ctx, n_lat):
    C = HGRN_CHUNK
    row = lax.broadcasted_iota(jnp.int32, (C, C), 0)
    col = lax.broadcasted_iota(jnp.int32, (C, C), 1)
    x = row ^ col
    lvl = jnp.zeros((C, C), jnp.int32)
    n = 1
    while n < C:
        lvl = lvl + (x >= n).astype(jnp.int32)
        n *= 2
    lv_f = jnp.where(row > col, lvl, 0)
    lv_b = jnp.where(row < col, lvl, 0)

    def cum_matrix(reverse):
        cum = lambda r: (col >= r) if reverse else (col <= r)
        mats = [cum(row).astype(F32)]
        for n_ in HGRN_FINE_LEVELS:
            anchor = row - row % n_ + (n_ // 2 if reverse else n_ // 2 - 1)
            upstream = (row % n_ < n_ // 2) != reverse
            diff = mats[0] - cum(anchor).astype(F32)
            mats.append(jnp.where(upstream, -diff, diff))
        return jnp.concatenate(mats, axis=0).astype(BF16)

    tri_f = cum_matrix(False)
    tri_b = cum_matrix(True)
    lb_f = lb_ref[0, 0:1, :]
    lb_b = lb_ref[0, 1:2, :]
    stf[...] = jnp.zeros_like(stf)
    stb[...] = jnp.zeros_like(stb)

    def run(q_ref, i_ref, f_ref, b_ref, n_chunks, base):
        G = min(n_chunks, HGRN_GROUP)
        R = G * C
        n_groups = n_chunks // G

        def body(j, carry):
            off_f = pl.multiple_of(j * R, R)
            off_b = pl.multiple_of((n_groups - 1 - j) * R, R)
            rf = pl.ds(off_f, R)
            rb = pl.ds(off_b, R)
            ld = lambda ref, r: ref[0, r, :].astype(F32)
            of_s[pl.ds(base + off_f, R), :] = _hgrn_group(
                ld(q_ref, rf), ld(i_ref, rf), ld(f_ref, rf), lb_f, stf, lv_f, tri_f, reverse=False)
            ob_s[pl.ds(base + off_b, R), :] = _hgrn_group(
                ld(q_ref, rb), ld(i_ref, rb), ld(b_ref, rb), lb_b, stb, lv_b, tri_b, reverse=True)
            return carry
        lax.fori_loop(0, n_groups, body, 0)

    run(qc, ic, fc, bc, n_ctx, 0)
    run(ql, il, fl, bl, n_lat, n_ctx * C)

    nw = nw_ref[...]

    def finish(g_ref, o_ref, n_chunks, base):
        R = C * math.gcd(n_chunks, 4)

        def body(j, carry):
            r = pl.ds(pl.multiple_of(j * R, R), R)
            rs = pl.ds(pl.multiple_of(base + j * R, C), R)
            o = of_s[rs, :] + ob_s[rs, :]
            o = o * lax.rsqrt(jnp.mean(o * o, axis=-1, keepdims=True) + NORM_EPS) * nw
            g = g_ref[0, r, :].astype(F32)
            o_ref[0, r, :] = (o * (g * _sigmoid(g))).astype(o_ref.dtype)
            return carry
        lax.fori_loop(0, n_chunks * C // R, body, 0)

    finish(gc, oc_ref, n_ctx, 0)
    finish(gl, ol_ref, n_lat, n_ctx * C)


def _hgrn(hg_ctx, hg_lat, lb, norm_w):
    bsz, lc, n5 = hg_ctx.shape
    ll = hg_lat.shape[1]
    dg = n5 // 5
    nh = dg // HEAD_DIM
    C = HGRN_CHUNK
    assert lc % C == 0 and ll % C == 0
    assert all((n // C) % min(n // C, HGRN_GROUP) == 0 for n in (lc, ll))

    def slab(L, part):
        return pl.BlockSpec((1, L, HEAD_DIM), lambda b, h: (b, 0, part * nh + h))

    return pl.pallas_call(
        functools.partial(_hgrn_kernel, n_ctx=lc // C, n_lat=ll // C),
        out_shape=(jax.ShapeDtypeStruct((bsz, lc, dg), BF16), jax.ShapeDtypeStruct((bsz, ll, dg), BF16)),
        grid=(bsz, nh),
        in_specs=[slab(lc, p) for p in range(5)] + [slab(ll, p) for p in range(5)] + [
            pl.BlockSpec((1, 2, HEAD_DIM), lambda b, h: (h, 0, 0)),
            pl.BlockSpec((1, HEAD_DIM), lambda b, h: (0, 0)),
        ],
        out_specs=(pl.BlockSpec((1, lc, HEAD_DIM), lambda b, h: (b, 0, h)),
                   pl.BlockSpec((1, ll, HEAD_DIM), lambda b, h: (b, 0, h))),
        scratch_shapes=[
            pltpu.VMEM((lc + ll, HEAD_DIM), F32),
            pltpu.VMEM((lc + ll, HEAD_DIM), F32),
            pltpu.VMEM((HEAD_DIM, HEAD_DIM), F32),
            pltpu.VMEM((HEAD_DIM, HEAD_DIM), F32),
        ],
        compiler_params=_cparams("parallel", "parallel"),
        name="hgrn2",
    )(*([hg_ctx] * 5), *([hg_lat] * 5), lb, norm_w.reshape(1, HEAD_DIM))


def _first_argmax(vals, idx, sentinel):
    m = jnp.max(vals, axis=0, keepdims=True)
    first = jnp.min(jnp.where(vals == m, idx, sentinel), axis=0, keepdims=True)
    return m, first


def _moe_gates(logits_t, rbias):
    ne, tm = logits_t.shape
    neg = -jnp.inf
    mx = jnp.max(logits_t, axis=0, keepdims=True)
    ex = jnp.exp(logits_t - mx)
    scores = ex / jnp.sum(ex, axis=0, keepdims=True)
    sel = scores + rbias
    eidx = lax.broadcasted_iota(jnp.int32, (ne, tm), 0)
    grp = eidx // EXPERTS_PER_GROUP
    best_s = jnp.full((1, tm), neg, F32)
    best_g = jnp.zeros((1, tm), jnp.int32)
    for gi in range(N_GROUPS):
        mg = jnp.where(grp == gi, sel, neg)
        m1, i1 = _first_argmax(mg, eidx, ne)
        m2 = jnp.max(jnp.where(eidx == i1, neg, mg), axis=0, keepdims=True)
        gs = m1 + m2
        upd = gs > best_s
        best_g = jnp.where(upd, gi, best_g)
        best_s = jnp.where(upd, gs, best_s)
    sg = jnp.where(grp == best_g, sel, neg)
    _, i1 = _first_argmax(sg, eidx, ne)
    _, i2 = _first_argmax(jnp.where(eidx == i1, neg, sg), eidx, ne)
    chosen = (eidx == i1) | (eidx == i2)
    w = jnp.where(chosen, scores, 0.0)
    return w / jnp.sum(w, axis=0, keepdims=True), best_g


def _mix_moe_kernel(x_ref, hy_ref, hg_ref, mod_ref, w_ref, rw_ref, rb_ref, wg_ref, wu_ref, wd_ref,
                    *rest, cap, final):
    fw_ref, o_ref, t_s, g4_s = rest if final else (None,) + rest
    c = hy_ref.shape[-1]
    mix = _dot(hy_ref[0], w_ref[:c, :]) + _dot(hg_ref[0], w_ref[c:, :])
    m = mod_ref[0]
    x1 = x_ref[0] + m[2:3] * mix
    o_ref[0] = x1
    xn = x1 * lax.rsqrt(jnp.mean(x1 * x1, axis=-1, keepdims=True) + NORM_EPS)
    t = xn * (1.0 + m[4:5]) + m[3:4]
    t_hi = t.astype(BF16)
    t_s[...] = t_hi
    t_lo = (t - t_hi.astype(F32)).astype(BF16)
    r = _dot(t_hi, rw_ref[...])
    lg = r[:, :LANES] + r[:, LANES:] + _dot(t_lo, rw_ref[:, :LANES])
    ne, _, f = wg_ref.shape
    tm = t.shape[0]
    per_group = ne // N_GROUPS
    gates_t, best_g = _moe_gates(lg.T[:ne], rb_ref[...])

    row8 = lax.broadcasted_iota(jnp.int32, (8, tm), 0)
    member = row8 == best_g
    ri = lax.broadcasted_iota(jnp.int32, (tm, tm), 0)
    ci = lax.broadcasted_iota(jnp.int32, (tm, tm), 1)
    before = _dot(member.astype(BF16), (ri < ci).astype(BF16))
    rank_row = jnp.sum(jnp.where(member, before, 0.0), axis=0, keepdims=True)
    grp_row = best_g.astype(F32)
    count = jnp.sum(member.astype(F32), axis=1, keepdims=True)
    g4_t = gates_t
    for g in range(1, N_GROUPS):
        g4_t = g4_t + pltpu.roll(gates_t, g * per_group, 0)
    row_e = lax.broadcasted_iota(jnp.int32, (ne, tm), 0)
    info_t = jnp.where(row_e < per_group, g4_t,
                       jnp.where(row_e == per_group, rank_row, jnp.where(row_e == per_group + 1, grp_row, 0.0)))
    info = jnp.concatenate([info_t, jnp.zeros((LANES - ne, tm), F32)], axis=0).T
    g4_s[...] = info.astype(BF16)
    rank_col = info[:, per_group:per_group + 1]
    grp_col = info[:, per_group + 1:per_group + 2]
    g2 = m[5:6]

    slot_col = lax.broadcasted_iota(jnp.int32, (cap, 1), 0).astype(F32)
    slot_row = lax.broadcasted_iota(jnp.int32, (1, cap), 1).astype(F32)
    for g in range(N_GROUPS):
        n_g = count[g, 0].astype(jnp.int32)

        def body(s, carry, g=g):
            base = (s * cap).astype(F32)
            pick = ((rank_row - base == slot_col) & (grp_row == float(g))).astype(BF16)
            xg = _dot(pick, t_s[...]).astype(BF16)
            gs = _dot(pick, g4_s[...])
            acts = []
            for j in range(per_group):
                e = g * per_group + j
                hgate = _dot(xg, wg_ref[e])
                hup = _dot(xg, wu_ref[e])
                acts.append((hgate * _sigmoid(hgate) * hup * gs[:, j:j + 1]).astype(BF16))
            y = _dot(jnp.concatenate(acts, axis=1), wd_ref[g * per_group * f:(g + 1) * per_group * f, :])
            put = ((rank_col - base == slot_row) & (grp_col == float(g))).astype(BF16)
            o_ref[0] += _dot(put, (y * g2).astype(BF16))
            return carry

        lax.fori_loop(0, (n_g + cap - 1) // cap, body, 0)

    if final:
        o = o_ref[0]
        o_ref[0] = o * lax.rsqrt(jnp.mean(o * o, axis=-1, keepdims=True) + NORM_EPS) * fw_ref[...]


def _mix_moe(x, hy, hg, mod_l, mod_row, layer, w_out, router_w, rbias, wg, wu, wd, *, tm, final_w=None):
    bsz, L, d = x.shape
    c = hy.shape[-1]
    cg = hg.shape[-1]
    _, ne, _, f = wg.shape
    rw_hi = router_w.astype(BF16)
    rw_lo = (router_w - rw_hi.astype(F32)).astype(BF16)
    rw = jnp.zeros((d, 2 * LANES), BF16).at[:, :ne].set(rw_hi).at[:, LANES:LANES + ne].set(rw_lo)
    cap = 5 * tm // 16
    assert cap % 8 == 0
    final = final_w is not None
    extra_specs = [_resident((1, d))] if final else []
    extra_args = [final_w.reshape(1, d)] if final else []
    return pl.pallas_call(
        functools.partial(_mix_moe_kernel, cap=cap, final=final),
        out_shape=jax.ShapeDtypeStruct((bsz, L, d), F32),
        grid=(bsz, L // tm),
        in_specs=[
            pl.BlockSpec((1, tm, d), lambda b, i: (b, i, 0)),
            pl.BlockSpec((1, tm, c), lambda b, i: (b, i, 0)),
            pl.BlockSpec((1, tm, cg), lambda b, i: (b, i, 0)),
            pl.BlockSpec((1, N_MOD, d), lambda b, i: (mod_row(b), 0, 0)),
            _resident((c + cg, d), layer),
            _resident((d, 2 * LANES)),
            _resident((ne, 1)),
            _resident((ne, d, f), layer),
            _resident((ne, d, f), layer),
            _resident((ne * f, d), layer),
        ] + extra_specs,
        out_specs=pl.BlockSpec((1, tm, d), lambda b, i: (b, i, 0)),
        scratch_shapes=[pltpu.VMEM((tm, d), BF16), pltpu.VMEM((tm, LANES), BF16)],
        compiler_params=_cparams("parallel", "parallel"),
        name="mix_moe",
    )(x, hy, hg, mod_l, w_out, rw, rbias.reshape(ne, 1), wg, wu, wd.reshape(-1, ne * f, d), *extra_args)


def kernel(x, c, ctx, c_ctx, w_mod, b_mod, w_in, w_out, hy_conv_w, hy_conv_b, hy_w1, hy_b1, hy_w2, hy_b2, hy_w3, hy_b3, hy_bias, hgrn_lower_bounds, hgrn_norm_w, router_w, router_bias, moe_w_gate, moe_w_up, moe_w_down, final_norm_w):
    bsz, seq, d = x.shape
    ctx_len = ctx.shape[1]
    depth = w_mod.shape[0]
    dg = hgrn_lower_bounds.shape[-1]
    nh = dg // HEAD_DIM
    assert bsz + 1 <= MOD_ROWS and seq % GRID_W == 0

    cc = jnp.zeros((MOD_ROWS, d), F32).at[:bsz].set(c).at[bsz].set(c_ctx)
    mod = _modulation(cc, w_mod, b_mod).reshape(depth, MOD_ROWS, N_MOD, d)
    lat_row = lambda b: b
    ctx_row = lambda b: bsz

    lb_soft = jax.nn.softmax(hgrn_lower_bounds.astype(F32), axis=1)
    lower = jnp.cumsum(lb_soft, axis=1) - lb_soft[:, :1]
    lower = lower.reshape(2, depth, nh, HEAD_DIM).transpose(1, 2, 0, 3)

    dft_lat = _dft_matrices(seq // _hyena_blocks(seq))
    dft_ctx = _dft_matrices(ctx_len // _hyena_blocks(ctx_len))
    tm_lat = min(seq, 1024)
    tm_moe = min(seq, 512)

    w_in_b = w_in.astype(BF16)
    moe_w = (w_out.astype(BF16), router_w, router_bias,
             moe_w_gate.astype(BF16), moe_w_up.astype(BF16), moe_w_down.astype(BF16))

    xc = ctx
    for l in range(depth):
        last = l == depth - 1
        u_lat, hg_lat = _inproj(x, mod[l], lat_row, w_in_b, l, hy_conv_w[l], hy_conv_b[l],
                                period=GRID_W, tm=tm_lat)
        u_ctx, hg_ctx = _inproj(xc, mod[l], ctx_row, w_in_b, l, hy_conv_w[l], hy_conv_b[l],
                                period=ctx_len, tm=ctx_len)
        o_ctx, o_lat = _hgrn(hg_ctx, hg_lat, lower[l], hgrn_norm_w[l])
        taps_w = (hy_w1[l], hy_b1[l], hy_w2[l], hy_b2[l], hy_w3[l], hy_b3[l])
        hy_lat = _hyena(u_lat, dft_lat, taps_w, hy_bias[l])
        x = _mix_moe(x, hy_lat, o_lat, mod[l], lat_row, l, *moe_w, tm=tm_moe,
                     final_w=final_norm_w if last else None)
        if not last:
            hy_ctx = _hyena(u_ctx, dft_ctx, taps_w, hy_bias[l])
            xc = _mix_moe(xc, hy_ctx, o_ctx, mod[l], ctx_row, l, *moe_w, tm=ctx_len)

    return x
```

```python
import functools
import math

import numpy as np
import jax
import jax.numpy as jnp
from jax import lax
from jax.experimental import pallas as pl
from jax.experimental.pallas import tpu as pltpu

F32 = jnp.float32
BF16 = jnp.bfloat16
HIGHEST = lax.Precision.HIGHEST

GRID_W = 64
NORM_EPS = 1e-6
N_MOD = 6
HY_ORDER = 2
HY_BANDS = 16
HY_TARGET = 1e-2
HY_FAST_PCT = 0.3
HY_SLOW_PCT = 1.5
HEAD_DIM = 128
HY_BLOCKS = 2
N_EXPERTS = 16
N_GROUPS = 4
EXPERTS_PER_GROUP = N_EXPERTS // N_GROUPS
LANES = 128
MOD_ROWS = 16
HGRN_CHUNK = 128
HGRN_GROUP = 16
HGRN_FINE_LEVELS = (2, 4)
VMEM_LIMIT = 56 << 20


def _cparams(*sem, flags=None):
    return pltpu.CompilerParams(dimension_semantics=sem, vmem_limit_bytes=VMEM_LIMIT, flags=flags)


def _sigmoid(x):
    return 1.0 / (1.0 + jnp.exp2(x * (-1.0 / math.log(2.0))))


def _dot(a, b, **kw):
    return jnp.dot(a, b, preferred_element_type=F32, **kw)


def _resident(shape, layer=None):
    if layer is None:
        return pl.BlockSpec(shape, lambda *_: (0,) * len(shape), pipeline_mode=pl.Buffered(1))
    return pl.BlockSpec((None,) + tuple(shape), lambda *_: (layer,) + (0,) * len(shape),
                        pipeline_mode=pl.Buffered(1))


def _dot_nt(a, b):
    return lax.dot_general(a, b, (((1,), (1,)), ((), ())), preferred_element_type=F32)


def _mod_kernel(c_ref, w_ref, b_ref, o_ref):
    c = c_ref[...]
    cs = c * _sigmoid(c)
    o_ref[0] = _dot(cs, w_ref[0], precision=HIGHEST) + b_ref[0]


def _modulation(cc, w_mod, b_mod):
    depth, d, n = w_mod.shape
    tn = n // 3
    return pl.pallas_call(
        _mod_kernel,
        out_shape=jax.ShapeDtypeStruct((depth, MOD_ROWS, n), F32),
        grid=(depth, n // tn),
        in_specs=[
            pl.BlockSpec((MOD_ROWS, d), lambda l, j: (0, 0)),
            pl.BlockSpec((1, d, tn), lambda l, j: (l, 0, j)),
            pl.BlockSpec((1, 1, tn), lambda l, j: (l, 0, j)),
        ],
        out_specs=pl.BlockSpec((1, MOD_ROWS, tn), lambda l, j: (l, 0, j)),
        compiler_params=_cparams("parallel", "parallel"),
        name="modulation",
    )(cc, w_mod, b_mod.reshape(depth, 1, n))


def _dft_kernel(ac_ref, as_ref, bc_ref, bs_ref, c_ref, s_ref, *, n1):
    bc = bc_ref[...]
    bs = bs_ref[...]
    ac = ac_ref[...]
    as_ = as_ref[...]
    for j in range(n1):
        a_c = ac[:, j:j + 1]
        a_s = as_[:, j:j + 1]
        c_ref[:, j * LANES:(j + 1) * LANES] = (a_c * bc - a_s * bs).astype(BF16)
        s_ref[:, j * LANES:(j + 1) * LANES] = (a_s * bc + a_c * bs).astype(BF16)


def _dft_tables(L):
    n1 = L // LANES
    period = 4 * L
    r = np.arange(L, dtype=np.int64)[:, None]
    c1 = np.arange(n1, dtype=np.int64)[None, :]
    c0 = np.arange(LANES, dtype=np.int64)[None, :]

    def cs(phase):
        ang = (phase % period).astype(np.float64) * (2.0 * np.pi / period)
        return np.cos(ang).astype(np.float32), np.sin(ang).astype(np.float32)

    fa = cs((2 * r + 1) * (LANES * c1))
    fb = cs((2 * r + 1) * c0)
    ta = cs((2 * LANES * c1) * r)
    tb = cs((2 * c0 + 1) * r)
    return (fa, fb), (ta, tb)


def _dft_matrices(L):
    n1 = L // LANES
    tr = min(L, 256)
    outs = []
    for (ac, as_), (bc, bs) in _dft_tables(L):
        c, s = pl.pallas_call(
            functools.partial(_dft_kernel, n1=n1),
            out_shape=(jax.ShapeDtypeStruct((L, L), BF16), jax.ShapeDtypeStruct((L, L), BF16)),
            grid=(L // tr,),
            in_specs=[
                pl.BlockSpec((tr, n1), lambda i: (i, 0)),
                pl.BlockSpec((tr, n1), lambda i: (i, 0)),
                pl.BlockSpec((tr, LANES), lambda i: (i, 0)),
                pl.BlockSpec((tr, LANES), lambda i: (i, 0)),
            ],
            out_specs=(pl.BlockSpec((tr, L), lambda i: (i, 0)), pl.BlockSpec((tr, L), lambda i: (i, 0))),
            compiler_params=_cparams("parallel"),
            name="dft_tables",
        )(jnp.asarray(ac), jnp.asarray(as_), jnp.asarray(bc), jnp.asarray(bs))
        outs.append((c, s))
    return outs


def _inproj_kernel(x_ref, mod_ref, w_ref, cw_ref, cb_ref, u_ref, hg_ref, *, period, n_hy, tn):
    x = x_ref[0]
    tm = x.shape[0]
    xn = x * lax.rsqrt(jnp.mean(x * x, axis=-1, keepdims=True) + NORM_EPS)
    m = mod_ref[0]
    a = (xn * (1.0 + m[1:2]) + m[0:1]).astype(BF16)
    pos = lax.broadcasted_iota(jnp.int32, (tm, 1), 0) % period
    first = pos == 0
    last = pos == period - 1
    n_total = w_ref.shape[1]
    for j in range(n_total // tn):
        p = _dot(a, w_ref[:, j * tn:(j + 1) * tn])
        if j * tn < n_hy:
            cw = cw_ref[:, j * tn:(j + 1) * tn]
            prev = jnp.where(first, 0.0, pltpu.roll(p, 1, 0))
            nxt = jnp.where(last, 0.0, pltpu.roll(p, tm - 1, 0))
            u_ref[0, :, j * tn:(j + 1) * tn] = (
                prev * cw[0:1] + p * cw[1:2] + nxt * cw[2:3] + cb_ref[:, j * tn:(j + 1) * tn]
            ).astype(u_ref.dtype)
        else:
            hg_ref[0, :, j * tn - n_hy:(j + 1) * tn - n_hy] = p.astype(hg_ref.dtype)


def _inproj(x, mod_l, mod_row, w_in, layer, conv_w, conv_b, *, period, tm):
    bsz, L, d = x.shape
    n_hy = conv_w.shape[1]
    n_all = w_in.shape[-1]
    tn = n_hy // 3
    assert tm % period == 0 or period == L == tm
    return pl.pallas_call(
        functools.partial(_inproj_kernel, period=period, n_hy=n_hy, tn=tn),
        out_shape=(jax.ShapeDtypeStruct((bsz, L, n_hy), BF16),
                   jax.ShapeDtypeStruct((bsz, L, n_all - n_hy), BF16)),
        grid=(bsz, L // tm),
        in_specs=[
            pl.BlockSpec((1, tm, d), lambda b, i: (b, i, 0)),
            pl.BlockSpec((1, N_MOD, d), lambda b, i: (mod_row(b), 0, 0)),
            _resident((d, n_all), layer),
            _resident((3, n_hy)),
            _resident((1, n_hy)),
        ],
        out_specs=(pl.BlockSpec((1, tm, n_hy), lambda b, i: (b, i, 0)),
                   pl.BlockSpec((1, tm, n_all - n_hy), lambda b, i: (b, i, 0))),
        compiler_params=_cparams("parallel", "parallel"),
        name="inproj",
    )(x, mod_l, w_in, conv_w, conv_b.reshape(1, n_hy))


def _rev_rows(x):
    nblk = x.shape[0] // LANES
    r = lax.broadcasted_iota(jnp.int32, (LANES, LANES), 0)
    c_ = lax.broadcasted_iota(jnp.int32, (LANES, LANES), 1)
    exch = (r + c_ == LANES - 1).astype(BF16)
    xb = x.astype(BF16)
    return jnp.concatenate(
        [_dot(exch, xb[(nblk - 1 - i) * LANES:(nblk - i) * LANES]) for i in range(nblk)], axis=0)


def _filter_kernel(z_ref, w1_ref, b1_ref, w2_ref, b2_ref, w3f_ref, b3f_ref, w3b_ref, b3b_ref,
                   dl_ref, kap_ref, rev_ref, h_s):
    z = z_ref[...]
    L = z.shape[0]

    @pl.when((pl.program_id(0) == 0) & (pl.program_id(1) == 0))
    def _():
        h1 = jnp.sin(_dot(z, w1_ref[...], precision=HIGHEST) + b1_ref[...])
        h_s[...] = jnp.sin(_dot(h1, w2_ref[...], precision=HIGHEST) + b2_ref[...])

    h = h_s[...]
    hf = _dot(h, w3f_ref[...], precision=HIGHEST) + b3f_ref[...]
    hb = _dot(h, w3b_ref[...], precision=HIGHEST) + b3b_ref[...]
    win = jnp.exp(-z[:, 0:1] * dl_ref[...])
    hf = hf * win
    hb = hb * win
    nrm = (jnp.sum(jnp.abs(hf), axis=0, keepdims=True)
           + jnp.sum(jnp.abs(hb), axis=0, keepdims=True))
    inv = 1.0 / nrm
    hf = hf * inv
    hb = hb * inv
    first = lax.broadcasted_iota(jnp.int32, (L, 1), 0) == 0
    down1 = lambda y: jnp.where(first, 0.0, pltpu.roll(y, 1, 0))
    kap_ref[0, 0:L, :] = down1(_rev_rows(hb)).astype(BF16)
    kap_ref[0, L:2 * L, :] = hf.astype(BF16)
    rev_ref[0, 0:L, :] = down1(_rev_rows(hf)).astype(BF16)
    rev_ref[0, L:2 * L, :] = jnp.where(first, hf[0:1], hb).astype(BF16)


def _hyena_filter_taps(L, w1, b1, w2, b2, w3, b3):
    nfeat, hid = w1.shape
    c = w3.shape[1] // (2 * HY_ORDER)
    tc = min(c, 512)
    nct = c // tc
    t = jnp.linspace(0.0, 1.0, L, dtype=F32)
    n = jnp.arange(L, dtype=F32)
    freqs = jnp.linspace(1e-4, HY_BANDS - 1, HY_BANDS, dtype=F32)
    ang = (2.0 * math.pi / L) * n[:, None] * freqs[None, :]
    z = jnp.concatenate([t[:, None], jnp.cos(ang), -jnp.sin(ang)], axis=-1)
    z = jnp.pad(z, ((0, 0), (0, LANES - nfeat)))
    w1p = jnp.pad(w1, ((0, LANES - nfeat), (0, 0)))
    deltas = jnp.abs(jnp.linspace(math.log(HY_TARGET) / HY_FAST_PCT, math.log(HY_TARGET) / HY_SLOW_PCT,
                                  c, dtype=F32)).reshape(1, c)
    full = lambda shape: pl.BlockSpec(shape, lambda o, j: (0,) * len(shape))
    out = jax.ShapeDtypeStruct((HY_ORDER, 2 * L, c), BF16)
    return pl.pallas_call(
        _filter_kernel,
        out_shape=(out, out),
        grid=(HY_ORDER, nct),
        in_specs=[
            full((L, LANES)), full((LANES, hid)), full((1, hid)), full((hid, hid)), full((1, hid)),
            pl.BlockSpec((hid, tc), lambda o, j: (0, o * 2 * nct + j)),
            pl.BlockSpec((1, tc), lambda o, j: (0, o * 2 * nct + j)),
            pl.BlockSpec((hid, tc), lambda o, j: (0, o * 2 * nct + nct + j)),
            pl.BlockSpec((1, tc), lambda o, j: (0, o * 2 * nct + nct + j)),
            pl.BlockSpec((1, tc), lambda o, j: (0, j)),
        ],
        out_specs=(pl.BlockSpec((1, 2 * L, tc), lambda o, j: (o, 0, j)),
                   pl.BlockSpec((1, 2 * L, tc), lambda o, j: (o, 0, j))),
        scratch_shapes=[pltpu.VMEM((L, hid), F32)],
        compiler_params=_cparams("arbitrary", "arbitrary"),
        name="hyena_filter",
    )(z, w1p, b1.reshape(1, hid), w2, b2.reshape(1, hid), w3, b3.reshape(1, -1), w3, b3.reshape(1, -1),
      deltas)


def _kspec_kernel(cf_ref, sf_ref, pos_ref, neg_ref, o_ref, *, scale):
    pos = pos_ref[0].astype(F32)
    neg = neg_ref[0].astype(F32)
    first = lax.broadcasted_iota(jnp.int32, (pos.shape[0], 1), 0) == 0
    neg = jnp.where(first, 0.0, neg)
    o_ref[0, 0, 0] = _dot(cf_ref[...], (pos + neg).astype(BF16)) * scale
    o_ref[0, 0, 1] = _dot(sf_ref[...], (pos - neg).astype(BF16)) * scale


def _filter_spectrum(cft, sft, kap, rev, nb):
    _, two_l, c = kap.shape
    bk = two_l // (2 * nb)
    n_off = 2 * nb - 1
    return pl.pallas_call(
        functools.partial(_kspec_kernel, scale=1.0 / bk),
        out_shape=jax.ShapeDtypeStruct((HY_ORDER, n_off, 2, bk, c), F32),
        grid=(HY_ORDER, n_off),
        in_specs=[
            _resident((bk, bk)),
            _resident((bk, bk)),
            pl.BlockSpec((1, bk, c), lambda o, k: (o, k + 1, 0)),
            pl.BlockSpec((1, bk, c), lambda o, k: (o, 2 * nb - 1 - k, 0)),
        ],
        out_specs=pl.BlockSpec((1, 1, 2, bk, c), lambda o, k: (o, k, 0, 0, 0)),
        compiler_params=_cparams("parallel", "parallel"),
        name="hyena_filter_spectrum",
    )(cft, sft, kap, rev)


def _hy_conv_kernel(cft_ref, sft_ref, cf_ref, sf_ref, src_ref, k_ref, gate_ref, d_ref, o_ref, a_s, b_s,
                    *, nb, bk):
    for j in range(nb):
        rows = slice(j * bk, (j + 1) * bk)
        v = src_ref[0, rows, :].astype(BF16)
        a_s[rows, :] = _dot(cft_ref[...], v).astype(BF16)
        b_s[rows, :] = _dot(sft_ref[...], v).astype(BF16)
    for i in range(nb):
        p = q = None
        for j in range(nb):
            rows = slice(j * bk, (j + 1) * bk)
            a = a_s[rows, :].astype(F32)
            b = b_s[rows, :].astype(F32)
            kr = k_ref[i - j + nb - 1, 0]
            ks = k_ref[i - j + nb - 1, 1]
            pj = a * kr - b * ks
            qj = a * ks + b * kr
            p = pj if p is None else p + pj
            q = qj if q is None else q + qj
        conv = _dot(cf_ref[...], p.astype(BF16)) + _dot(sf_ref[...], q.astype(BF16))
        rows = slice(i * bk, (i + 1) * bk)
        o_ref[0, rows, :] = (gate_ref[0, rows, :] * (conv + src_ref[0, rows, :] * d_ref[0])).astype(o_ref.dtype)


def _hy_conv(dft, src, src_col, kspec, gate, gate_col, d, order):
    (cft, sft), (cf, sf) = dft
    bsz, L, _ = src.shape
    _, n_off, _, bk, c = kspec.shape
    nb = (n_off + 1) // 2
    return pl.pallas_call(
        functools.partial(_hy_conv_kernel, nb=nb, bk=bk),
        out_shape=jax.ShapeDtypeStruct((bsz, L, c), BF16),
        grid=(bsz,),
        in_specs=[
            _resident((bk, bk)), _resident((bk, bk)), _resident((bk, bk)), _resident((bk, bk)),
            pl.BlockSpec((1, L, c), lambda b: (b, 0, src_col)),
            _resident((n_off, 2, bk, c), order),
            pl.BlockSpec((1, L, c), lambda b: (b, 0, gate_col)),
            pl.BlockSpec((1, 1, c), lambda b: (order, 0, 0)),
        ],
        out_specs=pl.BlockSpec((1, L, c), lambda b: (b, 0, 0)),
        scratch_shapes=[pltpu.VMEM((L, c), BF16), pltpu.VMEM((L, c), BF16)],
        compiler_params=_cparams("parallel"),
        name="hyena_conv",
    )(cft, sft, cf, sf, src, kspec, gate, d.reshape(HY_ORDER, 1, c))


def _hyena_blocks(L):
    bk = max(L // HY_BLOCKS, LANES)
    return L // bk


def _hyena(u, dft, taps_w, d):
    (cft, sft), _ = dft
    L = u.shape[1]
    nb = _hyena_blocks(L)
    kap, rev = _hyena_filter_taps(L, *taps_w)
    kspec = _filter_spectrum(cft, sft, kap, rev, nb)
    z = _hy_conv(dft, u, 0, kspec, u, 1, d, 0)
    return _hy_conv(dft, z, 0, kspec, u, 2, d, 1)


def _anchor_rows(b, n, a):
    assert n % 8 == 0
    parts = [jnp.broadcast_to(b[s + a:s + a + 1, :], (n, b.shape[1])) for s in range(0, b.shape[0], n)]
    return parts[0] if len(parts) == 1 else jnp.concatenate(parts, axis=0)


def _neg_abs_anchor_diff(b, n, a, reverse):
    h = n // 2
    m = _anchor_rows(b, n, a)
    if h % 8:
        bits = lax.bitcast_convert_type(b - m, jnp.uint32) | jnp.uint32(0x80000000)
        return lax.bitcast_convert_type(bits, F32)
    pieces = []
    for s in range(0, b.shape[0], h):
        first_half = (s // h) % 2 == 0
        upstream = first_half != reverse
        x, y = (m, b) if upstream else (b, m)
        pieces.append(x[s:s + h] - y[s:s + h])
    return jnp.concatenate(pieces, axis=0)


def _hgrn_group(q_raw, v, f_logit, lb, st_ref, lv, tri, *, reverse):
    C = HGRN_CHUNK
    G = q_raw.shape[0] // C
    rows = lambda x, i: x[i * C:(i + 1) * C]
    q = q_raw * _sigmoid(q_raw)
    f = lb + (1.0 - lb) * _sigmoid(f_logit)
    kk = 1.0 - f
    g = jnp.log2(f)
    g_top = lax.bitcast_convert_type(
        lax.bitcast_convert_type(g, jnp.uint32) & jnp.uint32(0xFFFF0000), F32)
    gg = jnp.concatenate([g_top.astype(BF16), (g - g_top).astype(BF16)], axis=1)
    bb = [_dot(tri, rows(gg, i)) for i in range(G)]
    part = lambda k: jnp.concatenate(
        [x[k * C:(k + 1) * C, :HEAD_DIM] + x[k * C:(k + 1) * C, HEAD_DIM:] for x in bb], axis=0)
    b = part(0)
    fine = {n_: part(k + 1) for k, n_ in enumerate(HGRN_FINE_LEVELS)}

    qb = q.astype(BF16)
    kb = kk.astype(BF16)
    scores = [jnp.zeros((C, C), F32)] * G
    n = 2
    level = 1
    while n <= C:
        a = n // 2 if reverse else n // 2 - 1
        d = fine[n] if n in fine else _neg_abs_anchor_diff(b, n, a, reverse)
        e = jnp.exp2(d).astype(BF16)
        qe = qb * e
        ke = kb * e
        scores = [jnp.where(lv == level, _dot_nt(rows(qe, i), rows(ke, i)), scores[i]) for i in range(G)]
        n *= 2
        level += 1

    vb = v.astype(BF16)
    diag = jnp.sum(q * kk, axis=-1, keepdims=True) * v
    o = [_dot(scores[i].astype(BF16), rows(vb, i)) + rows(diag, i) for i in range(G)]
    b_end = _anchor_rows(b, C, 0 if reverse else C - 1)
    qd = qb * jnp.exp2(b).astype(BF16)
    kd = kb * jnp.exp2(b_end - b).astype(BF16)
    dec = jnp.exp2(b_end)
    kv = [_dot(rows(v, i).T.astype(BF16), rows(kd, i)) for i in range(G)]
    st = st_ref[...]
    for i in (range(G - 1, -1, -1) if reverse else range(G)):
        o[i] = o[i] + _dot_nt(rows(qd, i), st.astype(BF16))
        st = st * dec[i * C:i * C + 1] + kv[i]
    st_ref[...] = st
    return jnp.concatenate(o, axis=0)


def _hgrn_kernel(qc, ic, gc, fc, bc, ql, il, gl, fl, bl, lb_ref, nw_ref, oc_ref, ol_ref,
                 of_s, ob_s, stf, stb, *, n_ctx, n_lat):
    C = HGRN_CHUNK
    row = lax.broadcasted_iota(jnp.int32, (C, C), 0)
    col = lax.broadcasted_iota(jnp.int32, (C, C), 1)
    x = row ^ col
    lvl = jnp.zeros((C, C), jnp.int32)
    n = 1
    while n < C:
        lvl = lvl + (x >= n).astype(jnp.int32)
        n *= 2
    lv_f = jnp.where(row > col, lvl, 0)
    lv_b = jnp.where(row < col, lvl, 0)

    def cum_matrix(reverse):
        cum = lambda r: (col >= r) if reverse else (col <= r)
        mats = [cum(row).astype(F32)]
        for n_ in HGRN_FINE_LEVELS:
            anchor = row - row % n_ + (n_ // 2 if reverse else n_ // 2 - 1)
            upstream = (row % n_ < n_ // 2) != reverse
            diff = mats[0] - cum(anchor).astype(F32)
            mats.append(jnp.where(upstream, -diff, diff))
        return jnp.concatenate(mats, axis=0).astype(BF16)

    tri_f = cum_matrix(False)
    tri_b = cum_matrix(True)
    nw = nw_ref[...]

    for h in range(lb_ref.shape[0]):
        lanes = slice(h * HEAD_DIM, (h + 1) * HEAD_DIM)
        lb_f = lb_ref[h, 0:1, :]
        lb_b = lb_ref[h, 1:2, :]
        stf[...] = jnp.zeros_like(stf)
        stb[...] = jnp.zeros_like(stb)

        def run(q_ref, i_ref, f_ref, b_ref, n_chunks, base):
            G = min(n_chunks, HGRN_GROUP)
            R = G * C
            n_groups = n_chunks // G

            def body(j, carry):
                off_f = pl.multiple_of(j * R, R)
                off_b = pl.multiple_of((n_groups - 1 - j) * R, R)
                rf = pl.ds(off_f, R)
                rb = pl.ds(off_b, R)
                ld = lambda ref, r: ref[0, r, lanes].astype(F32)
                of_s[pl.ds(base + off_f, R), :] = _hgrn_group(
                    ld(q_ref, rf), ld(i_ref, rf), ld(f_ref, rf), lb_f, stf, lv_f, tri_f, reverse=False)
                ob_s[pl.ds(base + off_b, R), :] = _hgrn_group(
                    ld(q_ref, rb), ld(i_ref, rb), ld(b_ref, rb), lb_b, stb, lv_b, tri_b, reverse=True)
                return carry
            lax.fori_loop(0, n_groups, body, 0)

        run(qc, ic, fc, bc, n_ctx, 0)
        run(ql, il, fl, bl, n_lat, n_ctx * C)

        def finish(g_ref, o_ref, n_chunks, base):
            R = C * math.gcd(n_chunks, 4)

            def body(j, carry):
                r = pl.ds(pl.multiple_of(j * R, R), R)
                rs = pl.ds(pl.multiple_of(base + j * R, C), R)
                o = of_s[rs, :] + ob_s[rs, :]
                o = o * lax.rsqrt(jnp.mean(o * o, axis=-1, keepdims=True) + NORM_EPS) * nw
                g = g_ref[0, r, lanes].astype(F32)
                o_ref[0, r, lanes] = (o * (g * _sigmoid(g))).astype(o_ref.dtype)
                return carry
            lax.fori_loop(0, n_chunks * C // R, body, 0)

        finish(gc, oc_ref, n_ctx, 0)
        finish(gl, ol_ref, n_lat, n_ctx * C)


def _hgrn(hg_ctx, hg_lat, lb, norm_w):
    bsz, lc, n5 = hg_ctx.shape
    ll = hg_lat.shape[1]
    dg = n5 // 5
    nh = dg // HEAD_DIM
    C = HGRN_CHUNK
    assert lc % C == 0 and ll % C == 0
    assert all((n // C) % min(n // C, HGRN_GROUP) == 0 for n in (lc, ll))

    hp = 2 if nh % 2 == 0 else 1
    nhp = nh // hp

    def slab(L, part):
        return pl.BlockSpec((1, L, hp * HEAD_DIM), lambda b, h: (b, 0, part * nhp + h))

    return pl.pallas_call(
        functools.partial(_hgrn_kernel, n_ctx=lc // C, n_lat=ll // C),
        out_shape=(jax.ShapeDtypeStruct((bsz, lc, dg), BF16), jax.ShapeDtypeStruct((bsz, ll, dg), BF16)),
        grid=(bsz, nhp),
        in_specs=[slab(lc, p) for p in range(5)] + [slab(ll, p) for p in range(5)] + [
            pl.BlockSpec((hp, 2, HEAD_DIM), lambda b, h: (h, 0, 0)),
            pl.BlockSpec((1, HEAD_DIM), lambda b, h: (0, 0)),
        ],
        out_specs=(pl.BlockSpec((1, lc, hp * HEAD_DIM), lambda b, h: (b, 0, h)),
                   pl.BlockSpec((1, ll, hp * HEAD_DIM), lambda b, h: (b, 0, h))),
        scratch_shapes=[
            pltpu.VMEM((lc + ll, HEAD_DIM), F32),
            pltpu.VMEM((lc + ll, HEAD_DIM), F32),
            pltpu.VMEM((HEAD_DIM, HEAD_DIM), F32),
            pltpu.VMEM((HEAD_DIM, HEAD_DIM), F32),
        ],
        compiler_params=_cparams("parallel", "parallel"),
        name="hgrn2",
    )(*([hg_ctx] * 5), *([hg_lat] * 5), lb, norm_w.reshape(1, HEAD_DIM))


def _first_argmax(vals, idx, sentinel):
    m = jnp.max(vals, axis=0, keepdims=True)
    first = jnp.min(jnp.where(vals == m, idx, sentinel), axis=0, keepdims=True)
    return m, first


def _moe_gates(logits_t, rbias):
    ne, tm = logits_t.shape
    neg = -jnp.inf
    mx = jnp.max(logits_t, axis=0, keepdims=True)
    ex = jnp.exp(logits_t - mx)
    scores = ex / jnp.sum(ex, axis=0, keepdims=True)
    sel = scores + rbias
    eidx = lax.broadcasted_iota(jnp.int32, (ne, tm), 0)
    grp = eidx // EXPERTS_PER_GROUP
    best_s = jnp.full((1, tm), neg, F32)
    best_g = jnp.zeros((1, tm), jnp.int32)
    for gi in range(N_GROUPS):
        mg = jnp.where(grp == gi, sel, neg)
        m1, i1 = _first_argmax(mg, eidx, ne)
        m2 = jnp.max(jnp.where(eidx == i1, neg, mg), axis=0, keepdims=True)
        gs = m1 + m2
        upd = gs > best_s
        best_g = jnp.where(upd, gi, best_g)
        best_s = jnp.where(upd, gs, best_s)
    sg = jnp.where(grp == best_g, sel, neg)
    _, i1 = _first_argmax(sg, eidx, ne)
    _, i2 = _first_argmax(jnp.where(eidx == i1, neg, sg), eidx, ne)
    chosen = (eidx == i1) | (eidx == i2)
    w = jnp.where(chosen, scores, 0.0)
    return w / jnp.sum(w, axis=0, keepdims=True), best_g


def _mix_moe_kernel(x_ref, hy_ref, hg_ref, mod_ref, w_ref, rw_ref, rb_ref, wg_ref, wu_ref, wd_ref,
                    *rest, cap, final):
    fw_ref, o_ref, t_s, g4_s = rest if final else (None,) + rest
    c = hy_ref.shape[-1]
    mix = _dot(hy_ref[0], w_ref[:c, :]) + _dot(hg_ref[0], w_ref[c:, :])
    m = mod_ref[0]
    x1 = x_ref[0] + m[2:3] * mix
    o_ref[0] = x1
    xn = x1 * lax.rsqrt(jnp.mean(x1 * x1, axis=-1, keepdims=True) + NORM_EPS)
    t = xn * (1.0 + m[4:5]) + m[3:4]
    t_hi = t.astype(BF16)
    t_s[...] = t_hi
    t_lo = (t - t_hi.astype(F32)).astype(BF16)
    r = _dot(t_hi, rw_ref[...])
    lg = r[:, :LANES] + r[:, LANES:] + _dot(t_lo, rw_ref[:, :LANES])
    ne, _, f = wg_ref.shape
    tm = t.shape[0]
    per_group = ne // N_GROUPS
    gates_t, best_g = _moe_gates(lg.T[:ne], rb_ref[...])

    row8 = lax.broadcasted_iota(jnp.int32, (8, tm), 0)
    member = row8 == best_g
    ri = lax.broadcasted_iota(jnp.int32, (tm, tm), 0)
    ci = lax.broadcasted_iota(jnp.int32, (tm, tm), 1)
    before = _dot(member.astype(BF16), (ri < ci).astype(BF16))
    rank_row = jnp.sum(jnp.where(member, before, 0.0), axis=0, keepdims=True)
    grp_row = best_g.astype(F32)
    count = jnp.sum(member.astype(F32), axis=1, keepdims=True)
    g4_t = gates_t
    for g in range(1, N_GROUPS):
        g4_t = g4_t + pltpu.roll(gates_t, g * per_group, 0)
    row_e = lax.broadcasted_iota(jnp.int32, (ne, tm), 0)
    info_t = jnp.where(row_e < per_group, g4_t,
                       jnp.where(row_e == per_group, rank_row, jnp.where(row_e == per_group + 1, grp_row, 0.0)))
    info = jnp.concatenate([info_t, jnp.zeros((LANES - ne, tm), F32)], axis=0).T
    g4_s[...] = info.astype(BF16)
    rank_col = info[:, per_group:per_group + 1]
    grp_col = info[:, per_group + 1:per_group + 2]
    g2 = m[5:6]

    slot_col = lax.broadcasted_iota(jnp.int32, (cap, 1), 0).astype(F32)
    slot_row = lax.broadcasted_iota(jnp.int32, (1, cap), 1).astype(F32)
    for g in range(N_GROUPS):
        n_g = count[g, 0].astype(jnp.int32)

        def body(s, carry, g=g):
            base = (s * cap).astype(F32)
            pick = ((rank_row - base == slot_col) & (grp_row == float(g))).astype(BF16)
            xg = _dot(pick, t_s[...]).astype(BF16)
            gs = _dot(pick, g4_s[...])
            acts = []
            for j in range(per_group):
                e = g * per_group + j
                hgate = _dot(xg, wg_ref[e])
                hup = _dot(xg, wu_ref[e])
                acts.append((hgate * _sigmoid(hgate) * hup * gs[:, j:j + 1]).astype(BF16))
            y = _dot(jnp.concatenate(acts, axis=1), wd_ref[g * per_group * f:(g + 1) * per_group * f, :])
            put = ((rank_col - base == slot_row) & (grp_col == float(g))).astype(BF16)
            o_ref[0] += _dot(put, (y * g2).astype(BF16))
            return carry

        lax.fori_loop(0, (n_g + cap - 1) // cap, body, 0)

    if final:
        o = o_ref[0]
        o_ref[0] = o * lax.rsqrt(jnp.mean(o * o, axis=-1, keepdims=True) + NORM_EPS) * fw_ref[...]


def _mix_moe(x, hy, hg, mod_l, mod_row, layer, w_out, router_w, rbias, wg, wu, wd, *, tm, final_w=None):
    bsz, L, d = x.shape
    c = hy.shape[-1]
    cg = hg.shape[-1]
    _, ne, _, f = wg.shape
    rw_hi = router_w.astype(BF16)
    rw_lo = (router_w - rw_hi.astype(F32)).astype(BF16)
    rw = jnp.zeros((d, 2 * LANES), BF16).at[:, :ne].set(rw_hi).at[:, LANES:LANES + ne].set(rw_lo)
    cap = 5 * tm // 16
    assert cap % 8 == 0
    final = final_w is not None
    extra_specs = [_resident((1, d))] if final else []
    extra_args = [final_w.reshape(1, d)] if final else []
    return pl.pallas_call(
        functools.partial(_mix_moe_kernel, cap=cap, final=final),
        out_shape=jax.ShapeDtypeStruct((bsz, L, d), F32),
        grid=(bsz, L // tm),
        in_specs=[
            pl.BlockSpec((1, tm, d), lambda b, i: (b, i, 0)),
            pl.BlockSpec((1, tm, c), lambda b, i: (b, i, 0)),
            pl.BlockSpec((1, tm, cg), lambda b, i: (b, i, 0)),
            pl.BlockSpec((1, N_MOD, d), lambda b, i: (mod_row(b), 0, 0)),
            _resident((c + cg, d), layer),
            _resident((d, 2 * LANES)),
            _resident((ne, 1)),
            _resident((ne, d, f), layer),
            _resident((ne, d, f), layer),
            _resident((ne * f, d), layer),
        ] + extra_specs,
        out_specs=pl.BlockSpec((1, tm, d), lambda b, i: (b, i, 0)),
        scratch_shapes=[pltpu.VMEM((tm, d), BF16), pltpu.VMEM((tm, LANES), BF16)],
        compiler_params=_cparams("parallel", "parallel"),
        name="mix_moe",
    )(x, hy, hg, mod_l, w_out, rw, rbias.reshape(ne, 1), wg, wu, wd.reshape(-1, ne * f, d), *extra_args)


def kernel(x, c, ctx, c_ctx, w_mod, b_mod, w_in, w_out, hy_conv_w, hy_conv_b, hy_w1, hy_b1, hy_w2, hy_b2, hy_w3, hy_b3, hy_bias, hgrn_lower_bounds, hgrn_norm_w, router_w, router_bias, moe_w_gate, moe_w_up, moe_w_down, final_norm_w):
    bsz, seq, d = x.shape
    ctx_len = ctx.shape[1]
    depth = w_mod.shape[0]
    dg = hgrn_lower_bounds.shape[-1]
    nh = dg // HEAD_DIM
    assert bsz + 1 <= MOD_ROWS and seq % GRID_W == 0

    cc = jnp.zeros((MOD_ROWS, d), F32).at[:bsz].set(c).at[bsz].set(c_ctx)
    mod = _modulation(cc, w_mod, b_mod).reshape(depth, MOD_ROWS, N_MOD, d)
    lat_row = lambda b: b
    ctx_row = lambda b: bsz

    lb_soft = jax.nn.softmax(hgrn_lower_bounds.astype(F32), axis=1)
    lower = jnp.cumsum(lb_soft, axis=1) - lb_soft[:, :1]
    lower = lower.reshape(2, depth, nh, HEAD_DIM).transpose(1, 2, 0, 3)

    dft_lat = _dft_matrices(seq // _hyena_blocks(seq))
    dft_ctx = _dft_matrices(ctx_len // _hyena_blocks(ctx_len))
    tm_lat = min(seq, 1024)
    tm_moe = min(seq, 512)

    w_in_b = w_in.astype(BF16)
    moe_w = (w_out.astype(BF16), router_w, router_bias,
             moe_w_gate.astype(BF16), moe_w_up.astype(BF16), moe_w_down.astype(BF16))

    xc = ctx
    for l in range(depth):
        last = l == depth - 1
        u_lat, hg_lat = _inproj(x, mod[l], lat_row, w_in_b, l, hy_conv_w[l], hy_conv_b[l],
                                period=GRID_W, tm=tm_lat)
        u_ctx, hg_ctx = _inproj(xc, mod[l], ctx_row, w_in_b, l, hy_conv_w[l], hy_conv_b[l],
                                period=ctx_len, tm=ctx_len)
        o_ctx, o_lat = _hgrn(hg_ctx, hg_lat, lower[l], hgrn_norm_w[l])
        taps_w = (hy_w1[l], hy_b1[l], hy_w2[l], hy_b2[l], hy_w3[l], hy_b3[l])
        hy_lat = _hyena(u_lat, dft_lat, taps_w, hy_bias[l])
        x = _mix_moe(x, hy_lat, o_lat, mod[l], lat_row, l, *moe_w, tm=tm_moe,
                     final_w=final_norm_w if last else None)
        if not last:
            hy_ctx = _hyena(u_ctx, dft_ctx, taps_w, hy_bias[l])
            xc = _mix_moe(xc, hy_ctx, o_ctx, mod[l], ctx_row, l, *moe_w, tm=ctx_len)

    return x
```

```python
import functools
import math

import numpy as np
import jax
import jax.numpy as jnp
from jax import lax
from jax.experimental import pallas as pl
from jax.experimental.pallas import tpu as pltpu

F32 = jnp.float32
BF16 = jnp.bfloat16
HIGHEST = lax.Precision.HIGHEST

GRID_W = 64
NORM_EPS = 1e-6
N_MOD = 6
HY_ORDER = 2
HY_BANDS = 16
HY_TARGET = 1e-2
HY_FAST_PCT = 0.3
HY_SLOW_PCT = 1.5
HEAD_DIM = 128
HY_BLOCKS = 2
N_EXPERTS = 16
N_GROUPS = 4
EXPERTS_PER_GROUP = N_EXPERTS // N_GROUPS
LANES = 128
MOD_ROWS = 16
HGRN_CHUNK = 128
HGRN_GROUP = 16
HGRN_FINE_LEVELS = (2, 4)
VMEM_LIMIT = 56 << 20


def _cparams(*sem):
    return pltpu.CompilerParams(dimension_semantics=sem, vmem_limit_bytes=VMEM_LIMIT)


def _sigmoid(x):
    return 1.0 / (1.0 + jnp.exp2(x * (-1.0 / math.log(2.0))))


def _dot(a, b, **kw):
    return jnp.dot(a, b, preferred_element_type=F32, **kw)


def _resident(shape, layer=None):
    if layer is None:
        return pl.BlockSpec(shape, lambda *_: (0,) * len(shape), pipeline_mode=pl.Buffered(1))
    return pl.BlockSpec((None,) + tuple(shape), lambda *_: (layer,) + (0,) * len(shape),
                        pipeline_mode=pl.Buffered(1))


def _dot_nt(a, b):
    return lax.dot_general(a, b, (((1,), (1,)), ((), ())), preferred_element_type=F32)


def _mod_kernel(c_ref, w_ref, b_ref, o_ref):
    c = c_ref[...]
    cs = c * _sigmoid(c)
    o_ref[0] = _dot(cs, w_ref[0], precision=HIGHEST) + b_ref[0]


def _modulation(cc, w_mod, b_mod):
    depth, d, n = w_mod.shape
    tn = n // 3
    return pl.pallas_call(
        _mod_kernel,
        out_shape=jax.ShapeDtypeStruct((depth, MOD_ROWS, n), F32),
        grid=(depth, n // tn),
        in_specs=[
            pl.BlockSpec((MOD_ROWS, d), lambda l, j: (0, 0)),
            pl.BlockSpec((1, d, tn), lambda l, j: (l, 0, j)),
            pl.BlockSpec((1, 1, tn), lambda l, j: (l, 0, j)),
        ],
        out_specs=pl.BlockSpec((1, MOD_ROWS, tn), lambda l, j: (l, 0, j)),
        compiler_params=_cparams("parallel", "parallel"),
        name="modulation",
    )(cc, w_mod, b_mod.reshape(depth, 1, n))


def _dft_kernel(ac_ref, as_ref, bc_ref, bs_ref, c_ref, s_ref, *, n1):
    bc = bc_ref[...]
    bs = bs_ref[...]
    ac = ac_ref[...]
    as_ = as_ref[...]
    for j in range(n1):
        a_c = ac[:, j:j + 1]
        a_s = as_[:, j:j + 1]
        c_ref[:, j * LANES:(j + 1) * LANES] = (a_c * bc - a_s * bs).astype(BF16)
        s_ref[:, j * LANES:(j + 1) * LANES] = (a_s * bc + a_c * bs).astype(BF16)


def _dft_tables(L):
    n1 = L // LANES
    period = 4 * L
    r = np.arange(L, dtype=np.int64)[:, None]
    c1 = np.arange(n1, dtype=np.int64)[None, :]
    c0 = np.arange(LANES, dtype=np.int64)[None, :]

    def cs(phase):
        ang = (phase % period).astype(np.float64) * (2.0 * np.pi / period)
        return np.cos(ang).astype(np.float32), np.sin(ang).astype(np.float32)

    fa = cs((2 * r + 1) * (LANES * c1))
    fb = cs((2 * r + 1) * c0)
    ta = cs((2 * LANES * c1) * r)
    tb = cs((2 * c0 + 1) * r)
    return (fa, fb), (ta, tb)


def _dft_matrices(L):
    n1 = L // LANES
    tr = min(L, 256)
    outs = []
    for (ac, as_), (bc, bs) in _dft_tables(L):
        c, s = pl.pallas_call(
            functools.partial(_dft_kernel, n1=n1),
            out_shape=(jax.ShapeDtypeStruct((L, L), BF16), jax.ShapeDtypeStruct((L, L), BF16)),
            grid=(L // tr,),
            in_specs=[
                pl.BlockSpec((tr, n1), lambda i: (i, 0)),
                pl.BlockSpec((tr, n1), lambda i: (i, 0)),
                pl.BlockSpec((tr, LANES), lambda i: (i, 0)),
                pl.BlockSpec((tr, LANES), lambda i: (i, 0)),
            ],
            out_specs=(pl.BlockSpec((tr, L), lambda i: (i, 0)), pl.BlockSpec((tr, L), lambda i: (i, 0))),
            compiler_params=_cparams("parallel"),
            name="dft_tables",
        )(jnp.asarray(ac), jnp.asarray(as_), jnp.asarray(bc), jnp.asarray(bs))
        outs.append((c, s))
    return outs


def _inproj_kernel(x_ref, mod_ref, w_ref, cw_ref, cb_ref, u_ref, hg_ref, *, period, n_hy, tn):
    x = x_ref[0]
    tm = x.shape[0]
    xn = x * lax.rsqrt(jnp.mean(x * x, axis=-1, keepdims=True) + NORM_EPS)
    m = mod_ref[0]
    a = (xn * (1.0 + m[1:2]) + m[0:1]).astype(BF16)
    pos = lax.broadcasted_iota(jnp.int32, (tm, 1), 0) % period
    first = pos == 0
    last = pos == period - 1
    n_total = w_ref.shape[1]
    for j in range(n_total // tn):
        p = _dot(a, w_ref[:, j * tn:(j + 1) * tn])
        if j * tn < n_hy:
            cw = cw_ref[:, j * tn:(j + 1) * tn]
            prev = jnp.where(first, 0.0, pltpu.roll(p, 1, 0))
            nxt = jnp.where(last, 0.0, pltpu.roll(p, tm - 1, 0))
            u_ref[0, :, j * tn:(j + 1) * tn] = (
                prev * cw[0:1] + p * cw[1:2] + nxt * cw[2:3] + cb_ref[:, j * tn:(j + 1) * tn]
            ).astype(u_ref.dtype)
        else:
            hg_ref[0, :, j * tn - n_hy:(j + 1) * tn - n_hy] = p.astype(hg_ref.dtype)


def _inproj(x, mod_l, mod_row, w_in, layer, conv_w, conv_b, *, period, tm):
    bsz, L, d = x.shape
    n_hy = conv_w.shape[1]
    n_all = w_in.shape[-1]
    tn = n_hy // 3
    assert tm % period == 0 or period == L == tm
    return pl.pallas_call(
        functools.partial(_inproj_kernel, period=period, n_hy=n_hy, tn=tn),
        out_shape=(jax.ShapeDtypeStruct((bsz, L, n_hy), BF16),
                   jax.ShapeDtypeStruct((bsz, L, n_all - n_hy), BF16)),
        grid=(bsz, L // tm),
        in_specs=[
            pl.BlockSpec((1, tm, d), lambda b, i: (b, i, 0)),
            pl.BlockSpec((1, N_MOD, d), lambda b, i: (mod_row(b), 0, 0)),
            _resident((d, n_all), layer),
            _resident((3, n_hy)),
            _resident((1, n_hy)),
        ],
        out_specs=(pl.BlockSpec((1, tm, n_hy), lambda b, i: (b, i, 0)),
                   pl.BlockSpec((1, tm, n_all - n_hy), lambda b, i: (b, i, 0))),
        compiler_params=_cparams("parallel", "parallel"),
        name="inproj",
    )(x, mod_l, w_in, conv_w, conv_b.reshape(1, n_hy))


def _rev_rows(x):
    nblk = x.shape[0] // LANES
    r = lax.broadcasted_iota(jnp.int32, (LANES, LANES), 0)
    c_ = lax.broadcasted_iota(jnp.int32, (LANES, LANES), 1)
    exch = (r + c_ == LANES - 1).astype(BF16)
    xb = x.astype(BF16)
    return jnp.concatenate(
        [_dot(exch, xb[(nblk - 1 - i) * LANES:(nblk - i) * LANES]) for i in range(nblk)], axis=0)


def _filter_kernel(z_ref, w1_ref, b1_ref, w2_ref, b2_ref, w3f_ref, b3f_ref, w3b_ref, b3b_ref,
                   dl_ref, kap_ref, rev_ref, h_s):
    z = z_ref[...]
    L = z.shape[0]
    hid = w2_ref.shape[0]

    @pl.when((pl.program_id(0) == 0) & (pl.program_id(1) == 0))
    def _():
        h1 = jnp.sin(_dot(z, w1_ref[...], precision=HIGHEST) + b1_ref[...])
        h2 = jnp.sin(_dot(h1, w2_ref[...], precision=HIGHEST) + b2_ref[...])
        h_hi = h2.astype(BF16)
        h_s[...] = jnp.concatenate([h_hi, (h2 - h_hi.astype(F32)).astype(BF16)], axis=1)

    def last_layer(w_ref, b_ref):
        return (_dot(h_s[...], w_ref[:2 * hid, :]) + _dot(h_s[:, :hid], w_ref[2 * hid:, :])) + b_ref[...]

    hf = last_layer(w3f_ref, b3f_ref)
    hb = last_layer(w3b_ref, b3b_ref)
    win = jnp.exp(-z[:, 0:1] * dl_ref[...])
    hf = hf * win
    hb = hb * win
    nrm = (jnp.sum(jnp.abs(hf), axis=0, keepdims=True)
           + jnp.sum(jnp.abs(hb), axis=0, keepdims=True))
    inv = 1.0 / nrm
    hf = hf * inv
    hb = hb * inv
    first = lax.broadcasted_iota(jnp.int32, (L, 1), 0) == 0
    down1 = lambda y: jnp.where(first, 0.0, pltpu.roll(y, 1, 0))
    kap_ref[0, 0:L, :] = down1(_rev_rows(hb)).astype(BF16)
    kap_ref[0, L:2 * L, :] = hf.astype(BF16)
    rev_ref[0, 0:L, :] = down1(_rev_rows(hf)).astype(BF16)
    rev_ref[0, L:2 * L, :] = jnp.where(first, hf[0:1], hb).astype(BF16)


def _hyena_filter_taps(L, w1, b1, w2, b2, w3, b3):
    nfeat, hid = w1.shape
    c = w3.shape[1] // (2 * HY_ORDER)
    tc = min(c, 256)
    nct = c // tc
    t = jnp.linspace(0.0, 1.0, L, dtype=F32)
    n = jnp.arange(L, dtype=F32)
    freqs = jnp.linspace(1e-4, HY_BANDS - 1, HY_BANDS, dtype=F32)
    ang = (2.0 * math.pi / L) * n[:, None] * freqs[None, :]
    z = jnp.concatenate([t[:, None], jnp.cos(ang), -jnp.sin(ang)], axis=-1)
    z = jnp.pad(z, ((0, 0), (0, LANES - nfeat)))
    w1p = jnp.pad(w1, ((0, LANES - nfeat), (0, 0)))
    deltas = jnp.abs(jnp.linspace(math.log(HY_TARGET) / HY_FAST_PCT, math.log(HY_TARGET) / HY_SLOW_PCT,
                                  c, dtype=F32)).reshape(1, c)
    w3_hi = w3.astype(BF16)
    w3p = jnp.concatenate([w3_hi, w3_hi, (w3 - w3_hi.astype(F32)).astype(BF16)], axis=0)
    full = lambda shape: pl.BlockSpec(shape, lambda o, j: (0,) * len(shape))
    out = jax.ShapeDtypeStruct((HY_ORDER, 2 * L, c), BF16)
    return pl.pallas_call(
        _filter_kernel,
        out_shape=(out, out),
        grid=(HY_ORDER, nct),
        in_specs=[
            full((L, LANES)), full((LANES, hid)), full((1, hid)), full((hid, hid)), full((1, hid)),
            pl.BlockSpec((3 * hid, tc), lambda o, j: (0, o * 2 * nct + j)),
            pl.BlockSpec((1, tc), lambda o, j: (0, o * 2 * nct + j)),
            pl.BlockSpec((3 * hid, tc), lambda o, j: (0, o * 2 * nct + nct + j)),
            pl.BlockSpec((1, tc), lambda o, j: (0, o * 2 * nct + nct + j)),
            pl.BlockSpec((1, tc), lambda o, j: (0, j)),
        ],
        out_specs=(pl.BlockSpec((1, 2 * L, tc), lambda o, j: (o, 0, j)),
                   pl.BlockSpec((1, 2 * L, tc), lambda o, j: (o, 0, j))),
        scratch_shapes=[pltpu.VMEM((L, 2 * hid), BF16)],
        compiler_params=_cparams("arbitrary", "arbitrary"),
        name="hyena_filter",
    )(z, w1p, b1.reshape(1, hid), w2, b2.reshape(1, hid), w3p, b3.reshape(1, -1), w3p, b3.reshape(1, -1),
      deltas)


def _kspec_kernel(cf_ref, sf_ref, pos_ref, neg_ref, o_ref, *, scale):
    pos = pos_ref[0].astype(F32)
    neg = neg_ref[0].astype(F32)
    first = lax.broadcasted_iota(jnp.int32, (pos.shape[0], 1), 0) == 0
    neg = jnp.where(first, 0.0, neg)
    o_ref[0, 0, 0] = _dot(cf_ref[...], (pos + neg).astype(BF16)) * scale
    o_ref[0, 0, 1] = _dot(sf_ref[...], (pos - neg).astype(BF16)) * scale


def _filter_spectrum(cft, sft, kap, rev, nb):
    _, two_l, c = kap.shape
    bk = two_l // (2 * nb)
    n_off = 2 * nb - 1
    return pl.pallas_call(
        functools.partial(_kspec_kernel, scale=1.0 / bk),
        out_shape=jax.ShapeDtypeStruct((HY_ORDER, n_off, 2, bk, c), F32),
        grid=(HY_ORDER, n_off),
        in_specs=[
            _resident((bk, bk)),
            _resident((bk, bk)),
            pl.BlockSpec((1, bk, c), lambda o, k: (o, k + 1, 0)),
            pl.BlockSpec((1, bk, c), lambda o, k: (o, 2 * nb - 1 - k, 0)),
        ],
        out_specs=pl.BlockSpec((1, 1, 2, bk, c), lambda o, k: (o, k, 0, 0, 0)),
        compiler_params=_cparams("parallel", "parallel"),
        name="hyena_filter_spectrum",
    )(cft, sft, kap, rev)


def _hy_fwd_kernel(cf_ref, sf_ref, v_ref, a_ref, b_ref, *, nb, bk):
    for j in range(nb):
        rows = slice(j * bk, (j + 1) * bk)
        v = v_ref[0, rows, :].astype(BF16)
        a_ref[0, rows, :] = _dot(cf_ref[...], v).astype(BF16)
        b_ref[0, rows, :] = _dot(sf_ref[...], v).astype(BF16)


def _hy_forward(cft, sft, src, src_col, c, nb):
    bsz, L, _ = src.shape
    bk = L // nb
    out = jax.ShapeDtypeStruct((bsz, L, c), BF16)
    return pl.pallas_call(
        functools.partial(_hy_fwd_kernel, nb=nb, bk=bk),
        out_shape=(out, out),
        grid=(bsz,),
        in_specs=[
            _resident((bk, bk)),
            _resident((bk, bk)),
            pl.BlockSpec((1, L, c), lambda b: (b, 0, src_col)),
        ],
        out_specs=(pl.BlockSpec((1, L, c), lambda b: (b, 0, 0)), pl.BlockSpec((1, L, c), lambda b: (b, 0, 0))),
        compiler_params=_cparams("parallel"),
        name="hyena_spectrum",
    )(cft, sft, src)


def _hy_inv_kernel(cf_ref, sf_ref, a_ref, b_ref, k_ref, gate_ref, src_ref, d_ref, o_ref, *, nb, bk):
    for i in range(nb):
        p = q = None
        for j in range(nb):
            rows = slice(j * bk, (j + 1) * bk)
            a = a_ref[0, rows, :].astype(F32)
            b = b_ref[0, rows, :].astype(F32)
            kr = k_ref[i - j + nb - 1, 0]
            ks = k_ref[i - j + nb - 1, 1]
            pj = a * kr - b * ks
            qj = a * ks + b * kr
            p = pj if p is None else p + pj
            q = qj if q is None else q + qj
        conv = _dot(cf_ref[...], p.astype(BF16)) + _dot(sf_ref[...], q.astype(BF16))
        rows = slice(i * bk, (i + 1) * bk)
        o_ref[0, rows, :] = (gate_ref[0, rows, :] * (conv + src_ref[0, rows, :] * d_ref[0])).astype(o_ref.dtype)


def _hy_inverse(cf, sf, a, b, kspec, gate, gate_col, src, src_col, d, order):
    bsz, L, c = a.shape
    _, n_off, _, bk, _ = kspec.shape
    nb = (n_off + 1) // 2
    return pl.pallas_call(
        functools.partial(_hy_inv_kernel, nb=nb, bk=bk),
        out_shape=jax.ShapeDtypeStruct((bsz, L, c), BF16),
        grid=(bsz,),
        in_specs=[
            _resident((bk, bk)),
            _resident((bk, bk)),
            pl.BlockSpec((1, L, c), lambda b: (b, 0, 0)),
            pl.BlockSpec((1, L, c), lambda b: (b, 0, 0)),
            _resident((n_off, 2, bk, c), order),
            pl.BlockSpec((1, L, c), lambda b: (b, 0, gate_col)),
            pl.BlockSpec((1, L, c), lambda b: (b, 0, src_col)),
            pl.BlockSpec((1, 1, c), lambda b: (order, 0, 0)),
        ],
        out_specs=pl.BlockSpec((1, L, c), lambda b: (b, 0, 0)),
        compiler_params=_cparams("parallel"),
        name="hyena_inverse",
    )(cf, sf, a, b, kspec, gate, src, d.reshape(HY_ORDER, 1, c))


def _hyena_blocks(L):
    bk = max(L // HY_BLOCKS, LANES)
    return L // bk


def _hyena(u, dft, taps_w, d):
    (cft, sft), (cf, sf) = dft
    L = u.shape[1]
    c = u.shape[2] // 3
    nb = _hyena_blocks(L)
    kap, rev = _hyena_filter_taps(L, *taps_w)
    kspec = _filter_spectrum(cft, sft, kap, rev, nb)
    a, b = _hy_forward(cft, sft, u, 0, c, nb)
    z = _hy_inverse(cf, sf, a, b, kspec, u, 1, u, 0, d, 0)
    a, b = _hy_forward(cft, sft, z, 0, c, nb)
    return _hy_inverse(cf, sf, a, b, kspec, u, 2, z, 0, d, 1)


def _anchor_rows(b, n, a):
    assert n % 8 == 0
    parts = [jnp.broadcast_to(b[s + a:s + a + 1, :], (n, b.shape[1])) for s in range(0, b.shape[0], n)]
    return parts[0] if len(parts) == 1 else jnp.concatenate(parts, axis=0)


def _neg_abs_anchor_diff(b, n, a, reverse):
    h = n // 2
    m = _anchor_rows(b, n, a)
    if h % 8:
        bits = lax.bitcast_convert_type(b - m, jnp.uint32) | jnp.uint32(0x80000000)
        return lax.bitcast_convert_type(bits, F32)
    pieces = []
    for s in range(0, b.shape[0], h):
        first_half = (s // h) % 2 == 0
        upstream = first_half != reverse
        x, y = (m, b) if upstream else (b, m)
        pieces.append(x[s:s + h] - y[s:s + h])
    return jnp.concatenate(pieces, axis=0)


def _hgrn_group(q_raw, v, f_logit, lb, st_ref, lv, tri, *, reverse):
    C = HGRN_CHUNK
    G = q_raw.shape[0] // C
    rows = lambda x, i: x[i * C:(i + 1) * C]
    q = q_raw * _sigmoid(q_raw)
    f = lb + (1.0 - lb) * _sigmoid(f_logit)
    kk = 1.0 - f
    g = jnp.log2(f)
    g_top = lax.bitcast_convert_type(
        lax.bitcast_convert_type(g, jnp.uint32) & jnp.uint32(0xFFFF0000), F32)
    gg = jnp.concatenate([g_top.astype(BF16), (g - g_top).astype(BF16)], axis=1)
    bb = [_dot(tri, rows(gg, i)) for i in range(G)]
    part = lambda k: jnp.concatenate(
        [x[k * C:(k + 1) * C, :HEAD_DIM] + x[k * C:(k + 1) * C, HEAD_DIM:] for x in bb], axis=0)
    b = part(0)
    fine = {n_: part(k + 1) for k, n_ in enumerate(HGRN_FINE_LEVELS)}

    qb = q.astype(BF16)
    kb = kk.astype(BF16)
    scores = [jnp.zeros((C, C), F32)] * G
    n = 2
    level = 1
    while n <= C:
        a = n // 2 if reverse else n // 2 - 1
        d = fine[n] if n in fine else _neg_abs_anchor_diff(b, n, a, reverse)
        e = jnp.exp2(d).astype(BF16)
        qe = qb * e
        ke = kb * e
        scores = [jnp.where(lv == level, _dot_nt(rows(qe, i), rows(ke, i)), scores[i]) for i in range(G)]
        n *= 2
        level += 1

    vb = v.astype(BF16)
    diag = jnp.sum(q * kk, axis=-1, keepdims=True) * v
    o = [_dot(scores[i].astype(BF16), rows(vb, i)) + rows(diag, i) for i in range(G)]
    b_end = _anchor_rows(b, C, 0 if reverse else C - 1)
    qd = qb * jnp.exp2(b).astype(BF16)
    kd = kb * jnp.exp2(b_end - b).astype(BF16)
    dec = jnp.exp2(b_end)
    kv = [_dot(rows(v, i).T.astype(BF16), rows(kd, i)) for i in range(G)]
    st = st_ref[...]
    for i in (range(G - 1, -1, -1) if reverse else range(G)):
        o[i] = o[i] + _dot_nt(rows(qd, i), st.astype(BF16))
        st = st * dec[i * C:i * C + 1] + kv[i]
    st_ref[...] = st
    return jnp.concatenate(o, axis=0)


def _hgrn_kernel(qc, ic, gc, fc, bc, ql, il, gl, fl, bl, lb_ref, nw_ref, oc_ref, ol_ref,
                 of_s, ob_s, stf, stb, *, n_ctx, n_lat):
    C = HGRN_CHUNK
    row = lax.broadcasted_iota(jnp.int32, (C, C), 0)
    col = lax.broadcasted_iota(jnp.int32, (C, C), 1)
    x = row ^ col
    lvl = jnp.zeros((C, C), jnp.int32)
    n = 1
    while n < C:
        lvl = lvl + (x >= n).astype(jnp.int32)
        n *= 2
    lv_f = jnp.where(row > col, lvl, 0)
    lv_b = jnp.where(row < col, lvl, 0)

    def cum_matrix(reverse):
        cum = lambda r: (col >= r) if reverse else (col <= r)
        mats = [cum(row).astype(F32)]
        for n_ in HGRN_FINE_LEVELS:
            anchor = row - row % n_ + (n_ // 2 if reverse else n_ // 2 - 1)
            upstream = (row % n_ < n_ // 2) != reverse
            diff = mats[0] - cum(anchor).astype(F32)
            mats.append(jnp.where(upstream, -diff, diff))
        return jnp.concatenate(mats, axis=0).astype(BF16)

    tri_f = cum_matrix(False)
    tri_b = cum_matrix(True)
    lb_f = lb_ref[0, 0:1, :]
    lb_b = lb_ref[0, 1:2, :]
    stf[...] = jnp.zeros_like(stf)
    stb[...] = jnp.zeros_like(stb)

    def run(q_ref, i_ref, f_ref, b_ref, n_chunks, base):
        G = min(n_chunks, HGRN_GROUP)
        R = G * C
        n_groups = n_chunks // G

        def body(j, carry):
            off_f = pl.multiple_of(j * R, R)
            off_b = pl.multiple_of((n_groups - 1 - j) * R, R)
            rf = pl.ds(off_f, R)
            rb = pl.ds(off_b, R)
            ld = lambda ref, r: ref[0, r, :].astype(F32)
            of_s[pl.ds(base + off_f, R), :] = _hgrn_group(
                ld(q_ref, rf), ld(i_ref, rf), ld(f_ref, rf), lb_f, stf, lv_f, tri_f, reverse=False)
            ob_s[pl.ds(base + off_b, R), :] = _hgrn_group(
                ld(q_ref, rb), ld(i_ref, rb), ld(b_ref, rb), lb_b, stb, lv_b, tri_b, reverse=True)
            return carry
        lax.fori_loop(0, n_groups, body, 0)

    run(qc, ic, fc, bc, n_ctx, 0)
    run(ql, il, fl, bl, n_lat, n_ctx * C)

    nw = nw_ref[...]

    def finish(g_ref, o_ref, n_chunks, base):
        R = C * math.gcd(n_chunks, 4)

        def body(j, carry):
            r = pl.ds(pl.multiple_of(j * R, R), R)
            rs = pl.ds(pl.multiple_of(base + j * R, C), R)
            o = of_s[rs, :] + ob_s[rs, :]
            o = o * lax.rsqrt(jnp.mean(o * o, axis=-1, keepdims=True) + NORM_EPS) * nw
            g = g_ref[0, r, :].astype(F32)
            o_ref[0, r, :] = (o * (g * _sigmoid(g))).astype(o_ref.dtype)
            return carry
        lax.fori_loop(0, n_chunks * C // R, body, 0)

    finish(gc, oc_ref, n_ctx, 0)
    finish(gl, ol_ref, n_lat, n_ctx * C)


def _hgrn(hg_ctx, hg_lat, lb, norm_w):
    bsz, lc, n5 = hg_ctx.shape
    ll = hg_lat.shape[1]
    dg = n5 // 5
    nh = dg // HEAD_DIM
    C = HGRN_CHUNK
    assert lc % C == 0 and ll % C == 0
    assert all((n // C) % min(n // C, HGRN_GROUP) == 0 for n in (lc, ll))

    def slab(L, part):
        return pl.BlockSpec((1, L, HEAD_DIM), lambda b, h: (b, 0, part * nh + h))

    return pl.pallas_call(
        functools.partial(_hgrn_kernel, n_ctx=lc // C, n_lat=ll // C),
        out_shape=(jax.ShapeDtypeStruct((bsz, lc, dg), BF16), jax.ShapeDtypeStruct((bsz, ll, dg), BF16)),
        grid=(bsz, nh),
        in_specs=[slab(lc, p) for p in range(5)] + [slab(ll, p) for p in range(5)] + [
            pl.BlockSpec((1, 2, HEAD_DIM), lambda b, h: (h, 0, 0)),
            pl.BlockSpec((1, HEAD_DIM), lambda b, h: (0, 0)),
        ],
        out_specs=(pl.BlockSpec((1, lc, HEAD_DIM), lambda b, h: (b, 0, h)),
                   pl.BlockSpec((1, ll, HEAD_DIM), lambda b, h: (b, 0, h))),
        scratch_shapes=[
            pltpu.VMEM((lc + ll, HEAD_DIM), F32),
            pltpu.VMEM((lc + ll, HEAD_DIM), F32),
            pltpu.VMEM((HEAD_DIM, HEAD_DIM), F32),
            pltpu.VMEM((HEAD_DIM, HEAD_DIM), F32),
        ],
        compiler_params=_cparams("parallel", "parallel"),
        name="hgrn2",
    )(*([hg_ctx] * 5), *([hg_lat] * 5), lb, norm_w.reshape(1, HEAD_DIM))


def _first_argmax(vals, idx, sentinel):
    m = jnp.max(vals, axis=0, keepdims=True)
    first = jnp.min(jnp.where(vals == m, idx, sentinel), axis=0, keepdims=True)
    return m, first


def _moe_gates(logits_t, rbias):
    ne, tm = logits_t.shape
    neg = -jnp.inf
    mx = jnp.max(logits_t, axis=0, keepdims=True)
    ex = jnp.exp(logits_t - mx)
    scores = ex / jnp.sum(ex, axis=0, keepdims=True)
    sel = scores + rbias
    eidx = lax.broadcasted_iota(jnp.int32, (ne, tm), 0)
    grp = eidx // EXPERTS_PER_GROUP
    best_s = jnp.full((1, tm), neg, F32)
    best_g = jnp.zeros((1, tm), jnp.int32)
    for gi in range(N_GROUPS):
        mg = jnp.where(grp == gi, sel, neg)
        m1, i1 = _first_argmax(mg, eidx, ne)
        m2 = jnp.max(jnp.where(eidx == i1, neg, mg), axis=0, keepdims=True)
        gs = m1 + m2
        upd = gs > best_s
        best_g = jnp.where(upd, gi, best_g)
        best_s = jnp.where(upd, gs, best_s)
    sg = jnp.where(grp == best_g, sel, neg)
    _, i1 = _first_argmax(sg, eidx, ne)
    _, i2 = _first_argmax(jnp.where(eidx == i1, neg, sg), eidx, ne)
    chosen = (eidx == i1) | (eidx == i2)
    w = jnp.where(chosen, scores, 0.0)
    return w / jnp.sum(w, axis=0, keepdims=True), best_g


def _mix_moe_kernel(x_ref, hy_ref, hg_ref, mod_ref, w_ref, rw_ref, rb_ref, tri_ref, wg_ref, wu_ref, wd_ref,
                    *rest, cap, final):
    fw_ref, o_ref, t_s, g4_s = rest if final else (None,) + rest
    c = hy_ref.shape[-1]
    mix = _dot(hy_ref[0], w_ref[:c, :]) + _dot(hg_ref[0], w_ref[c:, :])
    m = mod_ref[0]
    x1 = x_ref[0] + m[2:3] * mix
    o_ref[0] = x1
    xn = x1 * lax.rsqrt(jnp.mean(x1 * x1, axis=-1, keepdims=True) + NORM_EPS)
    t = xn * (1.0 + m[4:5]) + m[3:4]
    t_hi = t.astype(BF16)
    t_s[...] = t_hi
    t_lo = (t - t_hi.astype(F32)).astype(BF16)
    r = _dot(t_hi, rw_ref[...])
    lg = r[:, :LANES] + r[:, LANES:] + _dot(t_lo, rw_ref[:, :LANES])
    ne, _, f = wg_ref.shape
    tm = t.shape[0]
    per_group = ne // N_GROUPS
    gates_t, best_g = _moe_gates(lg.T[:ne], rb_ref[...])

    row8 = lax.broadcasted_iota(jnp.int32, (8, tm), 0)
    member = row8 == best_g
    before = _dot(member.astype(BF16), tri_ref[...])
    rank_row = jnp.sum(jnp.where(member, before, 0.0), axis=0, keepdims=True)
    grp_row = best_g.astype(F32)
    count = jnp.sum(member.astype(F32), axis=1, keepdims=True)
    g4_t = gates_t
    for g in range(1, N_GROUPS):
        g4_t = g4_t + pltpu.roll(gates_t, g * per_group, 0)
    row_e = lax.broadcasted_iota(jnp.int32, (ne, tm), 0)
    info_t = jnp.where(row_e < per_group, g4_t,
                       jnp.where(row_e == per_group, rank_row, jnp.where(row_e == per_group + 1, grp_row, 0.0)))
    info = jnp.concatenate([info_t, jnp.zeros((LANES - ne, tm), F32)], axis=0).T
    g4_s[...] = info.astype(BF16)
    rank_col = info[:, per_group:per_group + 1]
    grp_col = info[:, per_group + 1:per_group + 2]
    g2 = m[5:6]

    slot_col = lax.broadcasted_iota(jnp.int32, (cap, 1), 0).astype(F32)
    slot_row = lax.broadcasted_iota(jnp.int32, (1, cap), 1).astype(F32)
    for g in range(N_GROUPS):
        n_g = count[g, 0].astype(jnp.int32)

        def body(s, carry, g=g):
            base = (s * cap).astype(F32)
            pick = ((rank_row - base == slot_col) & (grp_row == float(g))).astype(BF16)
            xg = _dot(pick, t_s[...]).astype(BF16)
            gs = _dot(pick, g4_s[...])
            acts = []
            for j in range(per_group):
                e = g * per_group + j
                hgate = _dot(xg, wg_ref[e])
                hup = _dot(xg, wu_ref[e])
                acts.append((hgate * _sigmoid(hgate) * hup * gs[:, j:j + 1]).astype(BF16))
            y = _dot(jnp.concatenate(acts, axis=1), wd_ref[g * per_group * f:(g + 1) * per_group * f, :])
            put = ((rank_col - base == slot_row) & (grp_col == float(g))).astype(BF16)
            o_ref[0] += _dot(put, (y * g2).astype(BF16))
            return carry

        lax.fori_loop(0, (n_g + cap - 1) // cap, body, 0)

    if final:
        o = o_ref[0]
        o_ref[0] = o * lax.rsqrt(jnp.mean(o * o, axis=-1, keepdims=True) + NORM_EPS) * fw_ref[...]


def _mix_moe(x, hy, hg, mod_l, mod_row, layer, w_out, router_w, rbias, wg, wu, wd, *, tm, final_w=None):
    bsz, L, d = x.shape
    c = hy.shape[-1]
    cg = hg.shape[-1]
    _, ne, _, f = wg.shape
    rw_hi = router_w.astype(BF16)
    rw_lo = (router_w - rw_hi.astype(F32)).astype(BF16)
    rw = jnp.zeros((d, 2 * LANES), BF16).at[:, :ne].set(rw_hi).at[:, LANES:LANES + ne].set(rw_lo)
    cap = 5 * tm // 16
    assert cap % 8 == 0
    pos = jnp.arange(tm, dtype=jnp.int32)
    tri = (pos[:, None] < pos[None, :]).astype(BF16)
    final = final_w is not None
    extra_specs = [_resident((1, d))] if final else []
    extra_args = [final_w.reshape(1, d)] if final else []
    return pl.pallas_call(
        functools.partial(_mix_moe_kernel, cap=cap, final=final),
        out_shape=jax.ShapeDtypeStruct((bsz, L, d), F32),
        grid=(bsz, L // tm),
        in_specs=[
            pl.BlockSpec((1, tm, d), lambda b, i: (b, i, 0)),
            pl.BlockSpec((1, tm, c), lambda b, i: (b, i, 0)),
            pl.BlockSpec((1, tm, cg), lambda b, i: (b, i, 0)),
            pl.BlockSpec((1, N_MOD, d), lambda b, i: (mod_row(b), 0, 0)),
            _resident((c + cg, d), layer),
            _resident((d, 2 * LANES)),
            _resident((ne, 1)),
            _resident((tm, tm)),
            _resident((ne, d, f), layer),
            _resident((ne, d, f), layer),
            _resident((ne * f, d), layer),
        ] + extra_specs,
        out_specs=pl.BlockSpec((1, tm, d), lambda b, i: (b, i, 0)),
        scratch_shapes=[pltpu.VMEM((tm, d), BF16), pltpu.VMEM((tm, LANES), BF16)],
        compiler_params=_cparams("parallel", "parallel"),
        name="mix_moe",
    )(x, hy, hg, mod_l, w_out, rw, rbias.reshape(ne, 1), tri, wg, wu, wd.reshape(-1, ne * f, d), *extra_args)


def kernel(x, c, ctx, c_ctx, w_mod, b_mod, w_in, w_out, hy_conv_w, hy_conv_b, hy_w1, hy_b1, hy_w2, hy_b2, hy_w3, hy_b3, hy_bias, hgrn_lower_bounds, hgrn_norm_w, router_w, router_bias, moe_w_gate, moe_w_up, moe_w_down, final_norm_w):
    bsz, seq, d = x.shape
    ctx_len = ctx.shape[1]
    depth = w_mod.shape[0]
    dg = hgrn_lower_bounds.shape[-1]
    nh = dg // HEAD_DIM
    assert bsz + 1 <= MOD_ROWS and seq % GRID_W == 0

    cc = jnp.zeros((MOD_ROWS, d), F32).at[:bsz].set(c).at[bsz].set(c_ctx)
    mod = _modulation(cc, w_mod, b_mod).reshape(depth, MOD_ROWS, N_MOD, d)
    lat_row = lambda b: b
    ctx_row = lambda b: bsz

    lb_soft = jax.nn.softmax(hgrn_lower_bounds.astype(F32), axis=1)
    lower = jnp.cumsum(lb_soft, axis=1) - lb_soft[:, :1]
    lower = lower.reshape(2, depth, nh, HEAD_DIM).transpose(1, 2, 0, 3)

    dft_lat = _dft_matrices(seq // _hyena_blocks(seq))
    dft_ctx = _dft_matrices(ctx_len // _hyena_blocks(ctx_len))
    tm_lat = min(seq, 1024)
    tm_moe = min(seq, 512)

    w_in_b = w_in.astype(BF16)
    moe_w = (w_out.astype(BF16), router_w, router_bias,
             moe_w_gate.astype(BF16), moe_w_up.astype(BF16), moe_w_down.astype(BF16))

    xc = ctx
    for l in range(depth):
        last = l == depth - 1
        u_lat, hg_lat = _inproj(x, mod[l], lat_row, w_in_b, l, hy_conv_w[l], hy_conv_b[l],
                                period=GRID_W, tm=tm_lat)
        u_ctx, hg_ctx = _inproj(xc, mod[l], ctx_row, w_in_b, l, hy_conv_w[l], hy_conv_b[l],
                                period=ctx_len, tm=ctx_len)
        o_ctx, o_lat = _hgrn(hg_ctx, hg_lat, lower[l], hgrn_norm_w[l])
        taps_w = (hy_w1[l], hy_b1[l], hy_w2[l], hy_b2[l], hy_w3[l], hy_b3[l])
        hy_lat = _hyena(u_lat, dft_lat, taps_w, hy_bias[l])
        x = _mix_moe(x, hy_lat, o_lat, mod[l], lat_row, l, *moe_w, tm=tm_moe,
                     final_w=final_norm_w if last else None)
        if not last:
            hy_ctx = _hyena(u_ctx, dft_ctx, taps_w, hy_bias[l])
            xc = _mix_moe(xc, hy_ctx, o_ctx, mod[l], ctx_row, l, *moe_w, tm=ctx_len)

    return x
```

```python
import functools
import math

import numpy as np
import jax
import jax.numpy as jnp
from jax import lax
from jax.experimental import pallas as pl
from jax.experimental.pallas import tpu as pltpu

F32 = jnp.float32
BF16 = jnp.bfloat16
HIGHEST = lax.Precision.HIGHEST

GRID_W = 64
NORM_EPS = 1e-6
N_MOD = 6
HY_ORDER = 2
HY_BANDS = 16
HY_TARGET = 1e-2
HY_FAST_PCT = 0.3
HY_SLOW_PCT = 1.5
HEAD_DIM = 128
HY_BLOCKS = 2
N_EXPERTS = 16
N_GROUPS = 4
EXPERTS_PER_GROUP = N_EXPERTS // N_GROUPS
LANES = 128
MOD_ROWS = 16
HGRN_CHUNK = 128
HGRN_GROUP = 16
HGRN_FINE_LEVELS = (2, 4)
VMEM_LIMIT = 56 << 20


def _cparams(*sem):
    return pltpu.CompilerParams(dimension_semantics=sem, vmem_limit_bytes=VMEM_LIMIT)


def _sigmoid(x):
    return 1.0 / (1.0 + jnp.exp2(x * (-1.0 / math.log(2.0))))


def _dot(a, b, **kw):
    return jnp.dot(a, b, preferred_element_type=F32, **kw)


def _resident(shape, layer=None):
    if layer is None:
        return pl.BlockSpec(shape, lambda *_: (0,) * len(shape), pipeline_mode=pl.Buffered(1))
    return pl.BlockSpec((None,) + tuple(shape), lambda *_: (layer,) + (0,) * len(shape),
                        pipeline_mode=pl.Buffered(1))


def _dot_nt(a, b):
    return lax.dot_general(a, b, (((1,), (1,)), ((), ())), preferred_element_type=F32)


def _mod_kernel(c_ref, w_ref, b_ref, o_ref):
    c = c_ref[...]
    cs = c * _sigmoid(c)
    o_ref[0] = _dot(cs, w_ref[0], precision=HIGHEST) + b_ref[0]


def _modulation(cc, w_mod, b_mod):
    depth, d, n = w_mod.shape
    tn = n // 3
    return pl.pallas_call(
        _mod_kernel,
        out_shape=jax.ShapeDtypeStruct((depth, MOD_ROWS, n), F32),
        grid=(depth, n // tn),
        in_specs=[
            pl.BlockSpec((MOD_ROWS, d), lambda l, j: (0, 0)),
            pl.BlockSpec((1, d, tn), lambda l, j: (l, 0, j)),
            pl.BlockSpec((1, 1, tn), lambda l, j: (l, 0, j)),
        ],
        out_specs=pl.BlockSpec((1, MOD_ROWS, tn), lambda l, j: (l, 0, j)),
        compiler_params=_cparams("parallel", "parallel"),
        name="modulation",
    )(cc, w_mod, b_mod.reshape(depth, 1, n))


def _dft_kernel(ac_ref, as_ref, bc_ref, bs_ref, c_ref, s_ref, *, n1):
    bc = bc_ref[...]
    bs = bs_ref[...]
    ac = ac_ref[...]
    as_ = as_ref[...]
    for j in range(n1):
        a_c = ac[:, j:j + 1]
        a_s = as_[:, j:j + 1]
        c_ref[:, j * LANES:(j + 1) * LANES] = (a_c * bc - a_s * bs).astype(BF16)
        s_ref[:, j * LANES:(j + 1) * LANES] = (a_s * bc + a_c * bs).astype(BF16)


def _dft_tables(L):
    n1 = L // LANES
    period = 4 * L
    r = np.arange(L, dtype=np.int64)[:, None]
    c1 = np.arange(n1, dtype=np.int64)[None, :]
    c0 = np.arange(LANES, dtype=np.int64)[None, :]

    def cs(phase):
        ang = (phase % period).astype(np.float64) * (2.0 * np.pi / period)
        return np.cos(ang).astype(np.float32), np.sin(ang).astype(np.float32)

    fa = cs((2 * r + 1) * (LANES * c1))
    fb = cs((2 * r + 1) * c0)
    ta = cs((2 * LANES * c1) * r)
    tb = cs((2 * c0 + 1) * r)
    return (fa, fb), (ta, tb)


def _dft_matrices(L):
    n1 = L // LANES
    tr = min(L, 256)
    outs = []
    for (ac, as_), (bc, bs) in _dft_tables(L):
        c, s = pl.pallas_call(
            functools.partial(_dft_kernel, n1=n1),
            out_shape=(jax.ShapeDtypeStruct((L, L), BF16), jax.ShapeDtypeStruct((L, L), BF16)),
            grid=(L // tr,),
            in_specs=[
                pl.BlockSpec((tr, n1), lambda i: (i, 0)),
                pl.BlockSpec((tr, n1), lambda i: (i, 0)),
                pl.BlockSpec((tr, LANES), lambda i: (i, 0)),
                pl.BlockSpec((tr, LANES), lambda i: (i, 0)),
            ],
            out_specs=(pl.BlockSpec((tr, L), lambda i: (i, 0)), pl.BlockSpec((tr, L), lambda i: (i, 0))),
            compiler_params=_cparams("parallel"),
            name="dft_tables",
        )(jnp.asarray(ac), jnp.asarray(as_), jnp.asarray(bc), jnp.asarray(bs))
        outs.append((c, s))
    return outs


def _inproj_kernel(x_ref, mod_ref, w_ref, cw_ref, cb_ref, u_ref, hg_ref, *, period, n_hy, tn):
    x = x_ref[0]
    tm = x.shape[0]
    xn = x * lax.rsqrt(jnp.mean(x * x, axis=-1, keepdims=True) + NORM_EPS)
    m = mod_ref[0]
    a = (xn * (1.0 + m[1:2]) + m[0:1]).astype(BF16)
    pos = lax.broadcasted_iota(jnp.int32, (tm, 1), 0) % period
    first = pos == 0
    last = pos == period - 1
    n_total = w_ref.shape[1]
    for j in range(n_total // tn):
        p = _dot(a, w_ref[:, j * tn:(j + 1) * tn])
        if j * tn < n_hy:
            cw = cw_ref[:, j * tn:(j + 1) * tn]
            prev = jnp.where(first, 0.0, pltpu.roll(p, 1, 0))
            nxt = jnp.where(last, 0.0, pltpu.roll(p, tm - 1, 0))
            u_ref[0, :, j * tn:(j + 1) * tn] = (
                prev * cw[0:1] + p * cw[1:2] + nxt * cw[2:3] + cb_ref[:, j * tn:(j + 1) * tn]
            ).astype(u_ref.dtype)
        else:
            hg_ref[0, :, j * tn - n_hy:(j + 1) * tn - n_hy] = p.astype(hg_ref.dtype)


def _inproj(x, mod_l, mod_row, w_in, layer, conv_w, conv_b, *, period, tm):
    bsz, L, d = x.shape
    n_hy = conv_w.shape[1]
    n_all = w_in.shape[-1]
    tn = n_hy // 3
    assert tm % period == 0 or period == L == tm
    return pl.pallas_call(
        functools.partial(_inproj_kernel, period=period, n_hy=n_hy, tn=tn),
        out_shape=(jax.ShapeDtypeStruct((bsz, L, n_hy), BF16),
                   jax.ShapeDtypeStruct((bsz, L, n_all - n_hy), BF16)),
        grid=(bsz, L // tm),
        in_specs=[
            pl.BlockSpec((1, tm, d), lambda b, i: (b, i, 0)),
            pl.BlockSpec((1, N_MOD, d), lambda b, i: (mod_row(b), 0, 0)),
            _resident((d, n_all), layer),
            _resident((3, n_hy)),
            _resident((1, n_hy)),
        ],
        out_specs=(pl.BlockSpec((1, tm, n_hy), lambda b, i: (b, i, 0)),
                   pl.BlockSpec((1, tm, n_all - n_hy), lambda b, i: (b, i, 0))),
        compiler_params=_cparams("parallel", "parallel"),
        name="inproj",
    )(x, mod_l, w_in, conv_w, conv_b.reshape(1, n_hy))


def _rev_rows(x):
    nblk = x.shape[0] // LANES
    r = lax.broadcasted_iota(jnp.int32, (LANES, LANES), 0)
    c_ = lax.broadcasted_iota(jnp.int32, (LANES, LANES), 1)
    exch = (r + c_ == LANES - 1).astype(BF16)
    xb = x.astype(BF16)
    return jnp.concatenate(
        [_dot(exch, xb[(nblk - 1 - i) * LANES:(nblk - i) * LANES]) for i in range(nblk)], axis=0)


def _filter_kernel(z_ref, w1_ref, b1_ref, w2_ref, b2_ref, w3f_ref, b3f_ref, w3b_ref, b3b_ref,
                   dl_ref, kap_ref, rev_ref, h_s):
    z = z_ref[...]
    L = z.shape[0]
    hid = w2_ref.shape[0]

    @pl.when((pl.program_id(0) == 0) & (pl.program_id(1) == 0))
    def _():
        h1 = jnp.sin(_dot(z, w1_ref[...], precision=HIGHEST) + b1_ref[...])
        h2 = jnp.sin(_dot(h1, w2_ref[...], precision=HIGHEST) + b2_ref[...])
        h_hi = h2.astype(BF16)
        h_s[...] = jnp.concatenate([h_hi, (h2 - h_hi.astype(F32)).astype(BF16)], axis=1)

    def last_layer(w_ref, b_ref):
        return (_dot(h_s[...], w_ref[:2 * hid, :]) + _dot(h_s[:, :hid], w_ref[2 * hid:, :])) + b_ref[...]

    hf = last_layer(w3f_ref, b3f_ref)
    hb = last_layer(w3b_ref, b3b_ref)
    win = jnp.exp(-z[:, 0:1] * dl_ref[...])
    hf = hf * win
    hb = hb * win
    nrm = (jnp.sum(jnp.abs(hf), axis=0, keepdims=True)
           + jnp.sum(jnp.abs(hb), axis=0, keepdims=True))
    inv = 1.0 / nrm
    hf = hf * inv
    hb = hb * inv
    first = lax.broadcasted_iota(jnp.int32, (L, 1), 0) == 0
    down1 = lambda y: jnp.where(first, 0.0, pltpu.roll(y, 1, 0))
    kap_ref[0, 0:L, :] = down1(_rev_rows(hb)).astype(BF16)
    kap_ref[0, L:2 * L, :] = hf.astype(BF16)
    rev_ref[0, 0:L, :] = down1(_rev_rows(hf)).astype(BF16)
    rev_ref[0, L:2 * L, :] = jnp.where(first, hf[0:1], hb).astype(BF16)


def _hyena_filter_taps(L, w1, b1, w2, b2, w3, b3):
    nfeat, hid = w1.shape
    c = w3.shape[1] // (2 * HY_ORDER)
    tc = min(c, 256)
    nct = c // tc
    t = jnp.linspace(0.0, 1.0, L, dtype=F32)
    n = jnp.arange(L, dtype=F32)
    freqs = jnp.linspace(1e-4, HY_BANDS - 1, HY_BANDS, dtype=F32)
    ang = (2.0 * math.pi / L) * n[:, None] * freqs[None, :]
    z = jnp.concatenate([t[:, None], jnp.cos(ang), -jnp.sin(ang)], axis=-1)
    z = jnp.pad(z, ((0, 0), (0, LANES - nfeat)))
    w1p = jnp.pad(w1, ((0, LANES - nfeat), (0, 0)))
    deltas = jnp.abs(jnp.linspace(math.log(HY_TARGET) / HY_FAST_PCT, math.log(HY_TARGET) / HY_SLOW_PCT,
                                  c, dtype=F32)).reshape(1, c)
    w3_hi = w3.astype(BF16)
    w3p = jnp.concatenate([w3_hi, w3_hi, (w3 - w3_hi.astype(F32)).astype(BF16)], axis=0)
    full = lambda shape: pl.BlockSpec(shape, lambda o, j: (0,) * len(shape))
    out = jax.ShapeDtypeStruct((HY_ORDER, 2 * L, c), BF16)
    return pl.pallas_call(
        _filter_kernel,
        out_shape=(out, out),
        grid=(HY_ORDER, nct),
        in_specs=[
            full((L, LANES)), full((LANES, hid)), full((1, hid)), full((hid, hid)), full((1, hid)),
            pl.BlockSpec((3 * hid, tc), lambda o, j: (0, o * 2 * nct + j)),
            pl.BlockSpec((1, tc), lambda o, j: (0, o * 2 * nct + j)),
            pl.BlockSpec((3 * hid, tc), lambda o, j: (0, o * 2 * nct + nct + j)),
            pl.BlockSpec((1, tc), lambda o, j: (0, o * 2 * nct + nct + j)),
            pl.BlockSpec((1, tc), lambda o, j: (0, j)),
        ],
        out_specs=(pl.BlockSpec((1, 2 * L, tc), lambda o, j: (o, 0, j)),
                   pl.BlockSpec((1, 2 * L, tc), lambda o, j: (o, 0, j))),
        scratch_shapes=[pltpu.VMEM((L, 2 * hid), BF16)],
        compiler_params=_cparams("arbitrary", "arbitrary"),
        name="hyena_filter",
    )(z, w1p, b1.reshape(1, hid), w2, b2.reshape(1, hid), w3p, b3.reshape(1, -1), w3p, b3.reshape(1, -1),
      deltas)


def _kspec_kernel(cf_ref, sf_ref, pos_ref, neg_ref, o_ref, *, scale):
    pos = pos_ref[0].astype(F32)
    neg = neg_ref[0].astype(F32)
    first = lax.broadcasted_iota(jnp.int32, (pos.shape[0], 1), 0) == 0
    neg = jnp.where(first, 0.0, neg)
    o_ref[0, 0, 0] = _dot(cf_ref[...], (pos + neg).astype(BF16)) * scale
    o_ref[0, 0, 1] = _dot(sf_ref[...], (pos - neg).astype(BF16)) * scale


def _filter_spectrum(cft, sft, kap, rev, nb):
    _, two_l, c = kap.shape
    bk = two_l // (2 * nb)
    n_off = 2 * nb - 1
    return pl.pallas_call(
        functools.partial(_kspec_kernel, scale=1.0 / bk),
        out_shape=jax.ShapeDtypeStruct((HY_ORDER, n_off, 2, bk, c), F32),
        grid=(HY_ORDER, n_off),
        in_specs=[
            _resident((bk, bk)),
            _resident((bk, bk)),
            pl.BlockSpec((1, bk, c), lambda o, k: (o, k + 1, 0)),
            pl.BlockSpec((1, bk, c), lambda o, k: (o, 2 * nb - 1 - k, 0)),
        ],
        out_specs=pl.BlockSpec((1, 1, 2, bk, c), lambda o, k: (o, k, 0, 0, 0)),
        compiler_params=_cparams("parallel", "parallel"),
        name="hyena_filter_spectrum",
    )(cft, sft, kap, rev)


def _hy_fwd_kernel(cf_ref, sf_ref, v_ref, a_ref, b_ref, *, nb, bk):
    for j in range(nb):
        rows = slice(j * bk, (j + 1) * bk)
        v = v_ref[0, rows, :].astype(BF16)
        a_ref[0, rows, :] = _dot(cf_ref[...], v).astype(BF16)
        b_ref[0, rows, :] = _dot(sf_ref[...], v).astype(BF16)


def _hy_forward(cft, sft, src, src_col, c, nb):
    bsz, L, _ = src.shape
    bk = L // nb
    out = jax.ShapeDtypeStruct((bsz, L, c), BF16)
    return pl.pallas_call(
        functools.partial(_hy_fwd_kernel, nb=nb, bk=bk),
        out_shape=(out, out),
        grid=(bsz,),
        in_specs=[
            _resident((bk, bk)),
            _resident((bk, bk)),
            pl.BlockSpec((1, L, c), lambda b: (b, 0, src_col)),
        ],
        out_specs=(pl.BlockSpec((1, L, c), lambda b: (b, 0, 0)), pl.BlockSpec((1, L, c), lambda b: (b, 0, 0))),
        compiler_params=_cparams("parallel"),
        name="hyena_spectrum",
    )(cft, sft, src)


def _hy_inv_kernel(cf_ref, sf_ref, a_ref, b_ref, k_ref, gate_ref, src_ref, d_ref, o_ref, *, nb, bk):
    for i in range(nb):
        p = q = None
        for j in range(nb):
            rows = slice(j * bk, (j + 1) * bk)
            a = a_ref[0, rows, :].astype(F32)
            b = b_ref[0, rows, :].astype(F32)
            kr = k_ref[i - j + nb - 1, 0]
            ks = k_ref[i - j + nb - 1, 1]
            pj = a * kr - b * ks
            qj = a * ks + b * kr
            p = pj if p is None else p + pj
            q = qj if q is None else q + qj
        conv = _dot(cf_ref[...], p.astype(BF16)) + _dot(sf_ref[...], q.astype(BF16))
        rows = slice(i * bk, (i + 1) * bk)
        o_ref[0, rows, :] = (gate_ref[0, rows, :] * (conv + src_ref[0, rows, :] * d_ref[0])).astype(o_ref.dtype)


def _hy_inverse(cf, sf, a, b, kspec, gate, gate_col, src, src_col, d, order):
    bsz, L, c = a.shape
    _, n_off, _, bk, _ = kspec.shape
    nb = (n_off + 1) // 2
    return pl.pallas_call(
        functools.partial(_hy_inv_kernel, nb=nb, bk=bk),
        out_shape=jax.ShapeDtypeStruct((bsz, L, c), BF16),
        grid=(bsz,),
        in_specs=[
            _resident((bk, bk)),
            _resident((bk, bk)),
            pl.BlockSpec((1, L, c), lambda b: (b, 0, 0)),
            pl.BlockSpec((1, L, c), lambda b: (b, 0, 0)),
            _resident((n_off, 2, bk, c), order),
            pl.BlockSpec((1, L, c), lambda b: (b, 0, gate_col)),
            pl.BlockSpec((1, L, c), lambda b: (b, 0, src_col)),
            pl.BlockSpec((1, 1, c), lambda b: (order, 0, 0)),
        ],
        out_specs=pl.BlockSpec((1, L, c), lambda b: (b, 0, 0)),
        compiler_params=_cparams("parallel"),
        name="hyena_inverse",
    )(cf, sf, a, b, kspec, gate, src, d.reshape(HY_ORDER, 1, c))


def _hyena_blocks(L):
    bk = max(L // HY_BLOCKS, LANES)
    return L // bk


def _hyena(u, dft, taps_w, d):
    (cft, sft), (cf, sf) = dft
    L = u.shape[1]
    c = u.shape[2] // 3
    nb = _hyena_blocks(L)
    kap, rev = _hyena_filter_taps(L, *taps_w)
    kspec = _filter_spectrum(cft, sft, kap, rev, nb)
    a, b = _hy_forward(cft, sft, u, 0, c, nb)
    z = _hy_inverse(cf, sf, a, b, kspec, u, 1, u, 0, d, 0)
    a, b = _hy_forward(cft, sft, z, 0, c, nb)
    return _hy_inverse(cf, sf, a, b, kspec, u, 2, z, 0, d, 1)


def _anchor_rows(b, n, a):
    assert n % 8 == 0
    parts = [jnp.broadcast_to(b[s + a:s + a + 1, :], (n, b.shape[1])) for s in range(0, b.shape[0], n)]
    return parts[0] if len(parts) == 1 else jnp.concatenate(parts, axis=0)


def _neg_abs_anchor_diff(b, n, a, reverse):
    h = n // 2
    m = _anchor_rows(b, n, a)
    if h % 8:
        bits = lax.bitcast_convert_type(b - m, jnp.uint32) | jnp.uint32(0x80000000)
        return lax.bitcast_convert_type(bits, F32)
    pieces = []
    for s in range(0, b.shape[0], h):
        first_half = (s // h) % 2 == 0
        upstream = first_half != reverse
        x, y = (m, b) if upstream else (b, m)
        pieces.append(x[s:s + h] - y[s:s + h])
    return jnp.concatenate(pieces, axis=0)


def _hgrn_group(q, v, f_logit, lb, st_ref, lv, tri, *, reverse):
    C = HGRN_CHUNK
    G = q.shape[0] // C
    rows = lambda x, i: x[i * C:(i + 1) * C]
    f = lb + (1.0 - lb) * _sigmoid(f_logit)
    kk = 1.0 - f
    g = jnp.log2(f)
    g_top = lax.bitcast_convert_type(
        lax.bitcast_convert_type(g, jnp.uint32) & jnp.uint32(0xFFFF0000), F32)
    gg = jnp.concatenate([g_top.astype(BF16), (g - g_top).astype(BF16)], axis=1)
    bb = [_dot(tri, rows(gg, i)) for i in range(G)]
    part = lambda k: jnp.concatenate(
        [x[k * C:(k + 1) * C, :HEAD_DIM] + x[k * C:(k + 1) * C, HEAD_DIM:] for x in bb], axis=0)
    b = part(0)
    fine = {n_: part(k + 1) for k, n_ in enumerate(HGRN_FINE_LEVELS)}

    qb = q.astype(BF16)
    kb = kk.astype(BF16)
    scores = [jnp.zeros((C, C), F32)] * G
    n = 2
    level = 1
    while n <= C:
        a = n // 2 if reverse else n // 2 - 1
        d = fine[n] if n in fine else _neg_abs_anchor_diff(b, n, a, reverse)
        e = jnp.exp2(d).astype(BF16)
        qe = qb * e
        ke = kb * e
        scores = [jnp.where(lv == level, _dot_nt(rows(qe, i), rows(ke, i)), scores[i]) for i in range(G)]
        n *= 2
        level += 1

    vb = v.astype(BF16)
    diag = jnp.sum(q * kk, axis=-1, keepdims=True) * v
    o = [_dot(scores[i].astype(BF16), rows(vb, i)) + rows(diag, i) for i in range(G)]
    b_end = _anchor_rows(b, C, 0 if reverse else C - 1)
    qd = qb * jnp.exp2(b).astype(BF16)
    kd = kb * jnp.exp2(b_end - b).astype(BF16)
    dec = jnp.exp2(b_end)
    kv = [_dot(rows(v, i).T.astype(BF16), rows(kd, i)) for i in range(G)]
    st = st_ref[...]
    for i in (range(G - 1, -1, -1) if reverse else range(G)):
        o[i] = o[i] + _dot_nt(rows(qd, i), st.astype(BF16))
        st = st * dec[i * C:i * C + 1] + kv[i]
    st_ref[...] = st
    return jnp.concatenate(o, axis=0)


def _hgrn_kernel(qc, ic, gc, fc, bc, ql, il, gl, fl, bl, lb_ref, nw_ref, oc_ref, ol_ref, stf, stb):
    C = HGRN_CHUNK
    row = lax.broadcasted_iota(jnp.int32, (C, C), 0)
    col = lax.broadcasted_iota(jnp.int32, (C, C), 1)
    x = row ^ col
    lvl = jnp.zeros((C, C), jnp.int32)
    n = 1
    while n < C:
        lvl = lvl + (x >= n).astype(jnp.int32)
        n *= 2
    lv_f = jnp.where(row > col, lvl, 0)
    lv_b = jnp.where(row < col, lvl, 0)

    def cum_matrix(reverse):
        cum = lambda r: (col >= r) if reverse else (col <= r)
        mats = [cum(row).astype(F32)]
        for n_ in HGRN_FINE_LEVELS:
            anchor = row - row % n_ + (n_ // 2 if reverse else n_ // 2 - 1)
            upstream = (row % n_ < n_ // 2) != reverse
            diff = mats[0] - cum(anchor).astype(F32)
            mats.append(jnp.where(upstream, -diff, diff))
        return jnp.concatenate(mats, axis=0).astype(BF16)

    tri_f = cum_matrix(False)
    tri_b = cum_matrix(True)
    lb_f = lb_ref[0, 0:1, :]
    lb_b = lb_ref[0, 1:2, :]
    nw = nw_ref[...]
    stf[...] = jnp.zeros_like(stf)
    stb[...] = jnp.zeros_like(stb)

    def run(q_ref, i_ref, g_ref, f_ref, b_ref, o_ref):
        ld = lambda ref: ref[0].astype(F32)
        q_raw = ld(q_ref)
        q = q_raw * _sigmoid(q_raw)
        v = ld(i_ref)
        o = (_hgrn_group(q, v, ld(f_ref), lb_f, stf, lv_f, tri_f, reverse=False)
             + _hgrn_group(q, v, ld(b_ref), lb_b, stb, lv_b, tri_b, reverse=True))
        o = o * lax.rsqrt(jnp.mean(o * o, axis=-1, keepdims=True) + NORM_EPS) * nw
        g = ld(g_ref)
        o_ref[0] = (o * (g * _sigmoid(g))).astype(o_ref.dtype)

    run(qc, ic, gc, fc, bc, oc_ref)
    run(ql, il, gl, fl, bl, ol_ref)


def _hgrn(hg_ctx, hg_lat, lb, norm_w):
    bsz, lc, n5 = hg_ctx.shape
    ll = hg_lat.shape[1]
    dg = n5 // 5
    nh = dg // HEAD_DIM
    C = HGRN_CHUNK
    assert lc % C == 0 and ll % C == 0 and max(lc, ll) <= HGRN_GROUP * C

    def slab(L, part):
        return pl.BlockSpec((1, L, HEAD_DIM), lambda b, h: (b, 0, part * nh + h))

    return pl.pallas_call(
        _hgrn_kernel,
        out_shape=(jax.ShapeDtypeStruct((bsz, lc, dg), BF16), jax.ShapeDtypeStruct((bsz, ll, dg), BF16)),
        grid=(bsz, nh),
        in_specs=[slab(lc, p) for p in range(5)] + [slab(ll, p) for p in range(5)] + [
            pl.BlockSpec((1, 2, HEAD_DIM), lambda b, h: (h, 0, 0)),
            pl.BlockSpec((1, HEAD_DIM), lambda b, h: (0, 0)),
        ],
        out_specs=(pl.BlockSpec((1, lc, HEAD_DIM), lambda b, h: (b, 0, h)),
                   pl.BlockSpec((1, ll, HEAD_DIM), lambda b, h: (b, 0, h))),
        scratch_shapes=[pltpu.VMEM((HEAD_DIM, HEAD_DIM), F32), pltpu.VMEM((HEAD_DIM, HEAD_DIM), F32)],
        compiler_params=_cparams("parallel", "parallel"),
        name="hgrn2",
    )(*([hg_ctx] * 5), *([hg_lat] * 5), lb, norm_w.reshape(1, HEAD_DIM))


def _first_argmax(vals, idx, sentinel):
    m = jnp.max(vals, axis=0, keepdims=True)
    first = jnp.min(jnp.where(vals == m, idx, sentinel), axis=0, keepdims=True)
    return m, first


def _moe_gates(logits_t, rbias):
    ne, tm = logits_t.shape
    neg = -jnp.inf
    mx = jnp.max(logits_t, axis=0, keepdims=True)
    ex = jnp.exp(logits_t - mx)
    scores = ex / jnp.sum(ex, axis=0, keepdims=True)
    sel = scores + rbias
    eidx = lax.broadcasted_iota(jnp.int32, (ne, tm), 0)
    grp = eidx // EXPERTS_PER_GROUP
    best_s = jnp.full((1, tm), neg, F32)
    best_g = jnp.zeros((1, tm), jnp.int32)
    for gi in range(N_GROUPS):
        mg = jnp.where(grp == gi, sel, neg)
        m1, i1 = _first_argmax(mg, eidx, ne)
        m2 = jnp.max(jnp.where(eidx == i1, neg, mg), axis=0, keepdims=True)
        gs = m1 + m2
        upd = gs > best_s
        best_g = jnp.where(upd, gi, best_g)
        best_s = jnp.where(upd, gs, best_s)
    sg = jnp.where(grp == best_g, sel, neg)
    _, i1 = _first_argmax(sg, eidx, ne)
    _, i2 = _first_argmax(jnp.where(eidx == i1, neg, sg), eidx, ne)
    chosen = (eidx == i1) | (eidx == i2)
    w = jnp.where(chosen, scores, 0.0)
    return w / jnp.sum(w, axis=0, keepdims=True), best_g


def _mix_moe_kernel(x_ref, hy_ref, hg_ref, mod_ref, w_ref, rw_ref, rb_ref, tri_ref, wg_ref, wu_ref, wd_ref,
                    *rest, cap, final):
    fw_ref, o_ref, t_s, g4_s = rest if final else (None,) + rest
    c = hy_ref.shape[-1]
    mix = _dot(hy_ref[0], w_ref[:c, :]) + _dot(hg_ref[0], w_ref[c:, :])
    m = mod_ref[0]
    x1 = x_ref[0] + m[2:3] * mix
    o_ref[0] = x1
    xn = x1 * lax.rsqrt(jnp.mean(x1 * x1, axis=-1, keepdims=True) + NORM_EPS)
    t = xn * (1.0 + m[4:5]) + m[3:4]
    t_hi = t.astype(BF16)
    t_s[...] = t_hi
    t_lo = (t - t_hi.astype(F32)).astype(BF16)
    r = _dot(t_hi, rw_ref[...])
    lg = r[:, :LANES] + r[:, LANES:] + _dot(t_lo, rw_ref[:, :LANES])
    ne, _, f = wg_ref.shape
    tm = t.shape[0]
    per_group = ne // N_GROUPS
    gates_t, best_g = _moe_gates(lg.T[:ne], rb_ref[...])

    row8 = lax.broadcasted_iota(jnp.int32, (8, tm), 0)
    member = row8 == best_g
    before = _dot(member.astype(BF16), tri_ref[...])
    rank_row = jnp.sum(jnp.where(member, before, 0.0), axis=0, keepdims=True)
    grp_row = best_g.astype(F32)
    count = jnp.sum(member.astype(F32), axis=1, keepdims=True)
    g4_t = gates_t
    for g in range(1, N_GROUPS):
        g4_t = g4_t + pltpu.roll(gates_t, g * per_group, 0)
    row_e = lax.broadcasted_iota(jnp.int32, (ne, tm), 0)
    info_t = jnp.where(row_e < per_group, g4_t,
                       jnp.where(row_e == per_group, rank_row, jnp.where(row_e == per_group + 1, grp_row, 0.0)))
    info = jnp.concatenate([info_t, jnp.zeros((LANES - ne, tm), F32)], axis=0).T
    g4_s[...] = info.astype(BF16)
    rank_col = info[:, per_group:per_group + 1]
    grp_col = info[:, per_group + 1:per_group + 2]
    g2 = m[5:6]

    slot_col = lax.broadcasted_iota(jnp.int32, (cap, 1), 0).astype(F32)
    slot_row = lax.broadcasted_iota(jnp.int32, (1, cap), 1).astype(F32)
    for g in range(N_GROUPS):
        n_g = count[g, 0].astype(jnp.int32)

        def body(s, carry, g=g):
            base = (s * cap).astype(F32)
            pick = ((rank_row - base == slot_col) & (grp_row == float(g))).astype(BF16)
            xg = _dot(pick, t_s[...]).astype(BF16)
            gs = _dot(pick, g4_s[...])
            acts = []
            for j in range(per_group):
                e = g * per_group + j
                hgate = _dot(xg, wg_ref[e])
                hup = _dot(xg, wu_ref[e])
                acts.append((hgate * _sigmoid(hgate) * hup * gs[:, j:j + 1]).astype(BF16))
            y = _dot(jnp.concatenate(acts, axis=1), wd_ref[g * per_group * f:(g + 1) * per_group * f, :])
            put = ((rank_col - base == slot_row) & (grp_col == float(g))).astype(BF16)
            o_ref[0] += _dot(put, (y * g2).astype(BF16))
            return carry

        lax.fori_loop(0, (n_g + cap - 1) // cap, body, 0)

    if final:
        o = o_ref[0]
        o_ref[0] = o * lax.rsqrt(jnp.mean(o * o, axis=-1, keepdims=True) + NORM_EPS) * fw_ref[...]


def _mix_moe(x, hy, hg, mod_l, mod_row, layer, w_out, router_w, rbias, wg, wu, wd, *, tm, final_w=None):
    bsz, L, d = x.shape
    c = hy.shape[-1]
    cg = hg.shape[-1]
    _, ne, _, f = wg.shape
    rw_hi = router_w.astype(BF16)
    rw_lo = (router_w - rw_hi.astype(F32)).astype(BF16)
    rw = jnp.zeros((d, 2 * LANES), BF16).at[:, :ne].set(rw_hi).at[:, LANES:LANES + ne].set(rw_lo)
    cap = 5 * tm // 16
    assert cap % 8 == 0
    pos = jnp.arange(tm, dtype=jnp.int32)
    tri = (pos[:, None] < pos[None, :]).astype(BF16)
    final = final_w is not None
    extra_specs = [_resident((1, d))] if final else []
    extra_args = [final_w.reshape(1, d)] if final else []
    return pl.pallas_call(
        functools.partial(_mix_moe_kernel, cap=cap, final=final),
        out_shape=jax.ShapeDtypeStruct((bsz, L, d), F32),
        grid=(bsz, L // tm),
        in_specs=[
            pl.BlockSpec((1, tm, d), lambda b, i: (b, i, 0)),
            pl.BlockSpec((1, tm, c), lambda b, i: (b, i, 0)),
            pl.BlockSpec((1, tm, cg), lambda b, i: (b, i, 0)),
            pl.BlockSpec((1, N_MOD, d), lambda b, i: (mod_row(b), 0, 0)),
            _resident((c + cg, d), layer),
            _resident((d, 2 * LANES)),
            _resident((ne, 1)),
            _resident((tm, tm)),
            _resident((ne, d, f), layer),
            _resident((ne, d, f), layer),
            _resident((ne * f, d), layer),
        ] + extra_specs,
        out_specs=pl.BlockSpec((1, tm, d), lambda b, i: (b, i, 0)),
        scratch_shapes=[pltpu.VMEM((tm, d), BF16), pltpu.VMEM((tm, LANES), BF16)],
        compiler_params=_cparams("parallel", "parallel"),
        name="mix_moe",
    )(x, hy, hg, mod_l, w_out, rw, rbias.reshape(ne, 1), tri, wg, wu, wd.reshape(-1, ne * f, d), *extra_args)


def kernel(x, c, ctx, c_ctx, w_mod, b_mod, w_in, w_out, hy_conv_w, hy_conv_b, hy_w1, hy_b1, hy_w2, hy_b2, hy_w3, hy_b3, hy_bias, hgrn_lower_bounds, hgrn_norm_w, router_w, router_bias, moe_w_gate, moe_w_up, moe_w_down, final_norm_w):
    bsz, seq, d = x.shape
    ctx_len = ctx.shape[1]
    depth = w_mod.shape[0]
    dg = hgrn_lower_bounds.shape[-1]
    nh = dg // HEAD_DIM
    assert bsz + 1 <= MOD_ROWS and seq % GRID_W == 0

    cc = jnp.zeros((MOD_ROWS, d), F32).at[:bsz].set(c).at[bsz].set(c_ctx)
    mod = _modulation(cc, w_mod, b_mod).reshape(depth, MOD_ROWS, N_MOD, d)
    lat_row = lambda b: b
    ctx_row = lambda b: bsz

    lb_soft = jax.nn.softmax(hgrn_lower_bounds.astype(F32), axis=1)
    lower = jnp.cumsum(lb_soft, axis=1) - lb_soft[:, :1]
    lower = lower.reshape(2, depth, nh, HEAD_DIM).transpose(1, 2, 0, 3)

    dft_lat = _dft_matrices(seq // _hyena_blocks(seq))
    dft_ctx = _dft_matrices(ctx_len // _hyena_blocks(ctx_len))
    tm_lat = min(seq, 1024)
    tm_moe = min(seq, 512)

    w_in_b = w_in.astype(BF16)
    moe_w = (w_out.astype(BF16), router_w, router_bias,
             moe_w_gate.astype(BF16), moe_w_up.astype(BF16), moe_w_down.astype(BF16))

    xc = ctx
    for l in range(depth):
        last = l == depth - 1
        u_lat, hg_lat = _inproj(x, mod[l], lat_row, w_in_b, l, hy_conv_w[l], hy_conv_b[l],
                                period=GRID_W, tm=tm_lat)
        u_ctx, hg_ctx = _inproj(xc, mod[l], ctx_row, w_in_b, l, hy_conv_w[l], hy_conv_b[l],
                                period=ctx_len, tm=ctx_len)
        o_ctx, o_lat = _hgrn(hg_ctx, hg_lat, lower[l], hgrn_norm_w[l])
        taps_w = (hy_w1[l], hy_b1[l], hy_w2[l], hy_b2[l], hy_w3[l], hy_b3[l])
        hy_lat = _hyena(u_lat, dft_lat, taps_w, hy_bias[l])
        x = _mix_moe(x, hy_lat, o_lat, mod[l], lat_row, l, *moe_w, tm=tm_moe,
                     final_w=final_norm_w if last else None)
        if not last:
            hy_ctx = _hyena(u_ctx, dft_ctx, taps_w, hy_bias[l])
            xc = _mix_moe(xc, hy_ctx, o_ctx, mod[l], ctx_row, l, *moe_w, tm=ctx_len)

    return x
```

```python
import functools
import math

import numpy as np
import jax
import jax.numpy as jnp
from jax import lax
from jax.experimental import pallas as pl
from jax.experimental.pallas import tpu as pltpu

F32 = jnp.float32
BF16 = jnp.bfloat16
HIGHEST = lax.Precision.HIGHEST

GRID_W = 64
NORM_EPS = 1e-6
N_MOD = 6
HY_ORDER = 2
HY_BANDS = 16
HY_TARGET = 1e-2
HY_FAST_PCT = 0.3
HY_SLOW_PCT = 1.5
HEAD_DIM = 128
HY_BLOCKS = 2
N_EXPERTS = 16
N_GROUPS = 4
EXPERTS_PER_GROUP = N_EXPERTS // N_GROUPS
LANES = 128
MOD_ROWS = 16
HGRN_CHUNK = 128
HGRN_GROUP = 16
HGRN_FINE_LEVELS = (2, 4)
VMEM_LIMIT = 56 << 20


def _cparams(*sem):
    return pltpu.CompilerParams(dimension_semantics=sem, vmem_limit_bytes=VMEM_LIMIT)


def _sigmoid(x):
    return 1.0 / (1.0 + jnp.exp2(x * (-1.0 / math.log(2.0))))


def _dot(a, b, **kw):
    return jnp.dot(a, b, preferred_element_type=F32, **kw)


def _resident(shape, layer=None):
    if layer is None:
        return pl.BlockSpec(shape, lambda *_: (0,) * len(shape), pipeline_mode=pl.Buffered(1))
    return pl.BlockSpec((None,) + tuple(shape), lambda *_: (layer,) + (0,) * len(shape),
                        pipeline_mode=pl.Buffered(1))


def _dot_nt(a, b):
    return lax.dot_general(a, b, (((1,), (1,)), ((), ())), preferred_element_type=F32)


def _mod_kernel(c_ref, w_ref, b_ref, o_ref):
    c = c_ref[...]
    cs = c * _sigmoid(c)
    o_ref[0] = _dot(cs, w_ref[0], precision=HIGHEST) + b_ref[0]


def _modulation(cc, w_mod, b_mod):
    depth, d, n = w_mod.shape
    tn = n // 3
    return pl.pallas_call(
        _mod_kernel,
        out_shape=jax.ShapeDtypeStruct((depth, MOD_ROWS, n), F32),
        grid=(depth, n // tn),
        in_specs=[
            pl.BlockSpec((MOD_ROWS, d), lambda l, j: (0, 0)),
            pl.BlockSpec((1, d, tn), lambda l, j: (l, 0, j)),
            pl.BlockSpec((1, 1, tn), lambda l, j: (l, 0, j)),
        ],
        out_specs=pl.BlockSpec((1, MOD_ROWS, tn), lambda l, j: (l, 0, j)),
        compiler_params=_cparams("parallel", "parallel"),
        name="modulation",
    )(cc, w_mod, b_mod.reshape(depth, 1, n))


def _dft_kernel(ac_ref, as_ref, bc_ref, bs_ref, c_ref, s_ref, *, n1):
    bc = bc_ref[...]
    bs = bs_ref[...]
    ac = ac_ref[...]
    as_ = as_ref[...]
    for j in range(n1):
        a_c = ac[:, j:j + 1]
        a_s = as_[:, j:j + 1]
        c_ref[:, j * LANES:(j + 1) * LANES] = (a_c * bc - a_s * bs).astype(BF16)
        s_ref[:, j * LANES:(j + 1) * LANES] = (a_s * bc + a_c * bs).astype(BF16)


def _dft_tables(L):
    n1 = L // LANES
    period = 4 * L
    r = np.arange(L, dtype=np.int64)[:, None]
    c1 = np.arange(n1, dtype=np.int64)[None, :]
    c0 = np.arange(LANES, dtype=np.int64)[None, :]

    def cs(phase):
        ang = (phase % period).astype(np.float64) * (2.0 * np.pi / period)
        return np.cos(ang).astype(np.float32), np.sin(ang).astype(np.float32)

    fa = cs((2 * r + 1) * (LANES * c1))
    fb = cs((2 * r + 1) * c0)
    ta = cs((2 * LANES * c1) * r)
    tb = cs((2 * c0 + 1) * r)
    return (fa, fb), (ta, tb)


def _dft_matrices(L):
    n1 = L // LANES
    tr = min(L, 256)
    outs = []
    for (ac, as_), (bc, bs) in _dft_tables(L):
        c, s = pl.pallas_call(
            functools.partial(_dft_kernel, n1=n1),
            out_shape=(jax.ShapeDtypeStruct((L, L), BF16), jax.ShapeDtypeStruct((L, L), BF16)),
            grid=(L // tr,),
            in_specs=[
                pl.BlockSpec((tr, n1), lambda i: (i, 0)),
                pl.BlockSpec((tr, n1), lambda i: (i, 0)),
                pl.BlockSpec((tr, LANES), lambda i: (i, 0)),
                pl.BlockSpec((tr, LANES), lambda i: (i, 0)),
            ],
            out_specs=(pl.BlockSpec((tr, L), lambda i: (i, 0)), pl.BlockSpec((tr, L), lambda i: (i, 0))),
            compiler_params=_cparams("parallel"),
            name="dft_tables",
        )(jnp.asarray(ac), jnp.asarray(as_), jnp.asarray(bc), jnp.asarray(bs))
        outs.append((c, s))
    return outs


def _inproj_kernel(x_ref, mod_ref, w_ref, cw_ref, cb_ref, u_ref, hg_ref, *, period, n_hy, tn):
    x = x_ref[0]
    tm = x.shape[0]
    xn = x * lax.rsqrt(jnp.mean(x * x, axis=-1, keepdims=True) + NORM_EPS)
    m = mod_ref[0]
    a = (xn * (1.0 + m[1:2]) + m[0:1]).astype(BF16)
    pos = lax.broadcasted_iota(jnp.int32, (tm, 1), 0) % period
    first = pos == 0
    last = pos == period - 1
    n_total = w_ref.shape[1]
    for j in range(n_total // tn):
        p = _dot(a, w_ref[:, j * tn:(j + 1) * tn])
        if j * tn < n_hy:
            cw = cw_ref[:, j * tn:(j + 1) * tn]
            prev = jnp.where(first, 0.0, pltpu.roll(p, 1, 0))
            nxt = jnp.where(last, 0.0, pltpu.roll(p, tm - 1, 0))
            u_ref[0, :, j * tn:(j + 1) * tn] = (
                prev * cw[0:1] + p * cw[1:2] + nxt * cw[2:3] + cb_ref[:, j * tn:(j + 1) * tn]
            ).astype(u_ref.dtype)
        else:
            hg_ref[0, :, j * tn - n_hy:(j + 1) * tn - n_hy] = p.astype(hg_ref.dtype)


def _inproj(x, mod_l, mod_row, w_in, layer, conv_w, conv_b, *, period, tm):
    bsz, L, d = x.shape
    n_hy = conv_w.shape[1]
    n_all = w_in.shape[-1]
    tn = n_hy // 3
    assert tm % period == 0 or period == L == tm
    return pl.pallas_call(
        functools.partial(_inproj_kernel, period=period, n_hy=n_hy, tn=tn),
        out_shape=(jax.ShapeDtypeStruct((bsz, L, n_hy), BF16),
                   jax.ShapeDtypeStruct((bsz, L, n_all - n_hy), BF16)),
        grid=(bsz, L // tm),
        in_specs=[
            pl.BlockSpec((1, tm, d), lambda b, i: (b, i, 0)),
            pl.BlockSpec((1, N_MOD, d), lambda b, i: (mod_row(b), 0, 0)),
            _resident((d, n_all), layer),
            _resident((3, n_hy)),
            _resident((1, n_hy)),
        ],
        out_specs=(pl.BlockSpec((1, tm, n_hy), lambda b, i: (b, i, 0)),
                   pl.BlockSpec((1, tm, n_all - n_hy), lambda b, i: (b, i, 0))),
        compiler_params=_cparams("parallel", "parallel"),
        name="inproj",
    )(x, mod_l, w_in, conv_w, conv_b.reshape(1, n_hy))


def _rev_rows(x):
    nblk = x.shape[0] // LANES
    r = lax.broadcasted_iota(jnp.int32, (LANES, LANES), 0)
    c_ = lax.broadcasted_iota(jnp.int32, (LANES, LANES), 1)
    exch = (r + c_ == LANES - 1).astype(BF16)
    xb = x.astype(BF16)
    return jnp.concatenate(
        [_dot(exch, xb[(nblk - 1 - i) * LANES:(nblk - i) * LANES]) for i in range(nblk)], axis=0)


def _filter_kernel(z_ref, w1_ref, b1_ref, w2_ref, b2_ref, w3f_ref, b3f_ref, w3b_ref, b3b_ref,
                   dl_ref, kap_ref, rev_ref, h_s):
    z = z_ref[...]
    L = z.shape[0]
    hid = w2_ref.shape[0]

    @pl.when((pl.program_id(0) == 0) & (pl.program_id(1) == 0))
    def _():
        h1 = jnp.sin(_dot(z, w1_ref[...], precision=HIGHEST) + b1_ref[...])
        h2 = jnp.sin(_dot(h1, w2_ref[...], precision=HIGHEST) + b2_ref[...])
        h_hi = h2.astype(BF16)
        h_s[...] = jnp.concatenate([h_hi, (h2 - h_hi.astype(F32)).astype(BF16)], axis=1)

    def last_layer(w_ref, b_ref):
        return (_dot(h_s[...], w_ref[:2 * hid, :]) + _dot(h_s[:, :hid], w_ref[2 * hid:, :])) + b_ref[...]

    hf = last_layer(w3f_ref, b3f_ref)
    hb = last_layer(w3b_ref, b3b_ref)
    win = jnp.exp(-z[:, 0:1] * dl_ref[...])
    hf = hf * win
    hb = hb * win
    nrm = (jnp.sum(jnp.abs(hf), axis=0, keepdims=True)
           + jnp.sum(jnp.abs(hb), axis=0, keepdims=True))
    inv = 1.0 / nrm
    hf = hf * inv
    hb = hb * inv
    first = lax.broadcasted_iota(jnp.int32, (L, 1), 0) == 0
    down1 = lambda y: jnp.where(first, 0.0, pltpu.roll(y, 1, 0))
    kap_ref[0, 0:L, :] = down1(_rev_rows(hb)).astype(BF16)
    kap_ref[0, L:2 * L, :] = hf.astype(BF16)
    rev_ref[0, 0:L, :] = down1(_rev_rows(hf)).astype(BF16)
    rev_ref[0, L:2 * L, :] = jnp.where(first, hf[0:1], hb).astype(BF16)


def _hyena_filter_taps(L, w1, b1, w2, b2, w3, b3):
    nfeat, hid = w1.shape
    c = w3.shape[1] // (2 * HY_ORDER)
    tc = min(c, 256)
    nct = c // tc
    t = jnp.linspace(0.0, 1.0, L, dtype=F32)
    n = jnp.arange(L, dtype=F32)
    freqs = jnp.linspace(1e-4, HY_BANDS - 1, HY_BANDS, dtype=F32)
    ang = (2.0 * math.pi / L) * n[:, None] * freqs[None, :]
    z = jnp.concatenate([t[:, None], jnp.cos(ang), -jnp.sin(ang)], axis=-1)
    z = jnp.pad(z, ((0, 0), (0, LANES - nfeat)))
    w1p = jnp.pad(w1, ((0, LANES - nfeat), (0, 0)))
    deltas = jnp.abs(jnp.linspace(math.log(HY_TARGET) / HY_FAST_PCT, math.log(HY_TARGET) / HY_SLOW_PCT,
                                  c, dtype=F32)).reshape(1, c)
    w3_hi = w3.astype(BF16)
    w3p = jnp.concatenate([w3_hi, w3_hi, (w3 - w3_hi.astype(F32)).astype(BF16)], axis=0)
    full = lambda shape: pl.BlockSpec(shape, lambda o, j: (0,) * len(shape))
    out = jax.ShapeDtypeStruct((HY_ORDER, 2 * L, c), BF16)
    return pl.pallas_call(
        _filter_kernel,
        out_shape=(out, out),
        grid=(HY_ORDER, nct),
        in_specs=[
            full((L, LANES)), full((LANES, hid)), full((1, hid)), full((hid, hid)), full((1, hid)),
            pl.BlockSpec((3 * hid, tc), lambda o, j: (0, o * 2 * nct + j)),
            pl.BlockSpec((1, tc), lambda o, j: (0, o * 2 * nct + j)),
            pl.BlockSpec((3 * hid, tc), lambda o, j: (0, o * 2 * nct + nct + j)),
            pl.BlockSpec((1, tc), lambda o, j: (0, o * 2 * nct + nct + j)),
            pl.BlockSpec((1, tc), lambda o, j: (0, j)),
        ],
        out_specs=(pl.BlockSpec((1, 2 * L, tc), lambda o, j: (o, 0, j)),
                   pl.BlockSpec((1, 2 * L, tc), lambda o, j: (o, 0, j))),
        scratch_shapes=[pltpu.VMEM((L, 2 * hid), BF16)],
        compiler_params=_cparams("arbitrary", "arbitrary"),
        name="hyena_filter",
    )(z, w1p, b1.reshape(1, hid), w2, b2.reshape(1, hid), w3p, b3.reshape(1, -1), w3p, b3.reshape(1, -1),
      deltas)


def _kspec_kernel(cf_ref, sf_ref, pos_ref, neg_ref, o_ref, *, scale):
    pos = pos_ref[0].astype(F32)
    neg = neg_ref[0].astype(F32)
    first = lax.broadcasted_iota(jnp.int32, (pos.shape[0], 1), 0) == 0
    neg = jnp.where(first, 0.0, neg)
    o_ref[0, 0, 0] = _dot(cf_ref[...], (pos + neg).astype(BF16)) * scale
    o_ref[0, 0, 1] = _dot(sf_ref[...], (pos - neg).astype(BF16)) * scale


def _filter_spectrum(cft, sft, kap, rev, nb):
    _, two_l, c = kap.shape
    bk = two_l // (2 * nb)
    n_off = 2 * nb - 1
    return pl.pallas_call(
        functools.partial(_kspec_kernel, scale=1.0 / bk),
        out_shape=jax.ShapeDtypeStruct((HY_ORDER, n_off, 2, bk, c), F32),
        grid=(HY_ORDER, n_off),
        in_specs=[
            _resident((bk, bk)),
            _resident((bk, bk)),
            pl.BlockSpec((1, bk, c), lambda o, k: (o, k + 1, 0)),
            pl.BlockSpec((1, bk, c), lambda o, k: (o, 2 * nb - 1 - k, 0)),
        ],
        out_specs=pl.BlockSpec((1, 1, 2, bk, c), lambda o, k: (o, k, 0, 0, 0)),
        compiler_params=_cparams("parallel", "parallel"),
        name="hyena_filter_spectrum",
    )(cft, sft, kap, rev)


def _hy_fwd_kernel(cf_ref, sf_ref, v_ref, a_ref, b_ref, *, nb, bk):
    for j in range(nb):
        rows = slice(j * bk, (j + 1) * bk)
        v = v_ref[0, rows, :].astype(BF16)
        a_ref[0, rows, :] = _dot(cf_ref[...], v).astype(BF16)
        b_ref[0, rows, :] = _dot(sf_ref[...], v).astype(BF16)


def _hy_forward(cft, sft, src, src_col, c, nb):
    bsz, L, _ = src.shape
    bk = L // nb
    out = jax.ShapeDtypeStruct((bsz, L, c), BF16)
    return pl.pallas_call(
        functools.partial(_hy_fwd_kernel, nb=nb, bk=bk),
        out_shape=(out, out),
        grid=(bsz,),
        in_specs=[
            _resident((bk, bk)),
            _resident((bk, bk)),
            pl.BlockSpec((1, L, c), lambda b: (b, 0, src_col)),
        ],
        out_specs=(pl.BlockSpec((1, L, c), lambda b: (b, 0, 0)), pl.BlockSpec((1, L, c), lambda b: (b, 0, 0))),
        compiler_params=_cparams("parallel"),
        name="hyena_spectrum",
    )(cft, sft, src)


def _hy_inv_kernel(cf_ref, sf_ref, a_ref, b_ref, k_ref, gate_ref, src_ref, d_ref, o_ref, *, nb, bk):
    for i in range(nb):
        p = q = None
        for j in range(nb):
            rows = slice(j * bk, (j + 1) * bk)
            a = a_ref[0, rows, :].astype(F32)
            b = b_ref[0, rows, :].astype(F32)
            kr = k_ref[i - j + nb - 1, 0]
            ks = k_ref[i - j + nb - 1, 1]
            pj = a * kr - b * ks
            qj = a * ks + b * kr
            p = pj if p is None else p + pj
            q = qj if q is None else q + qj
        conv = _dot(cf_ref[...], p.astype(BF16)) + _dot(sf_ref[...], q.astype(BF16))
        rows = slice(i * bk, (i + 1) * bk)
        o_ref[0, rows, :] = (gate_ref[0, rows, :] * (conv + src_ref[0, rows, :] * d_ref[0])).astype(o_ref.dtype)


def _hy_inverse(cf, sf, a, b, kspec, gate, gate_col, src, src_col, d, order):
    bsz, L, c = a.shape
    _, n_off, _, bk, _ = kspec.shape
    nb = (n_off + 1) // 2
    return pl.pallas_call(
        functools.partial(_hy_inv_kernel, nb=nb, bk=bk),
        out_shape=jax.ShapeDtypeStruct((bsz, L, c), BF16),
        grid=(bsz,),
        in_specs=[
            _resident((bk, bk)),
            _resident((bk, bk)),
            pl.BlockSpec((1, L, c), lambda b: (b, 0, 0)),
            pl.BlockSpec((1, L, c), lambda b: (b, 0, 0)),
            _resident((n_off, 2, bk, c), order),
            pl.BlockSpec((1, L, c), lambda b: (b, 0, gate_col)),
            pl.BlockSpec((1, L, c), lambda b: (b, 0, src_col)),
            pl.BlockSpec((1, 1, c), lambda b: (order, 0, 0)),
        ],
        out_specs=pl.BlockSpec((1, L, c), lambda b: (b, 0, 0)),
        compiler_params=_cparams("parallel"),
        name="hyena_inverse",
    )(cf, sf, a, b, kspec, gate, src, d.reshape(HY_ORDER, 1, c))


def _hyena_blocks(L):
    bk = max(L // HY_BLOCKS, LANES)
    return L // bk


def _hyena(u, dft, taps_w, d):
    (cft, sft), (cf, sf) = dft
    L = u.shape[1]
    c = u.shape[2] // 3
    nb = _hyena_blocks(L)
    kap, rev = _hyena_filter_taps(L, *taps_w)
    kspec = _filter_spectrum(cft, sft, kap, rev, nb)
    a, b = _hy_forward(cft, sft, u, 0, c, nb)
    z = _hy_inverse(cf, sf, a, b, kspec, u, 1, u, 0, d, 0)
    a, b = _hy_forward(cft, sft, z, 0, c, nb)
    return _hy_inverse(cf, sf, a, b, kspec, u, 2, z, 0, d, 1)


def _anchor_rows(b, n, a):
    assert n % 8 == 0
    parts = [jnp.broadcast_to(b[s + a:s + a + 1, :], (n, b.shape[1])) for s in range(0, b.shape[0], n)]
    return parts[0] if len(parts) == 1 else jnp.concatenate(parts, axis=0)


def _neg_abs_anchor_diff(b, n, a, reverse):
    h = n // 2
    m = _anchor_rows(b, n, a)
    if h % 8:
        bits = lax.bitcast_convert_type(b - m, jnp.uint32) | jnp.uint32(0x80000000)
        return lax.bitcast_convert_type(bits, F32)
    pieces = []
    for s in range(0, b.shape[0], h):
        first_half = (s // h) % 2 == 0
        upstream = first_half != reverse
        x, y = (m, b) if upstream else (b, m)
        pieces.append(x[s:s + h] - y[s:s + h])
    return jnp.concatenate(pieces, axis=0)


def _hgrn_group(q, v, f_logit, lb, st_ref, lv, tri, *, reverse, other=None):
    C = HGRN_CHUNK
    G = q.shape[0] // C
    rows = lambda x, i: x[i * C:(i + 1) * C]
    f = lb + (1.0 - lb) * _sigmoid(f_logit)
    kk = 1.0 - f
    g = jnp.log2(f)
    g_top = lax.bitcast_convert_type(
        lax.bitcast_convert_type(g, jnp.uint32) & jnp.uint32(0xFFFF0000), F32)
    gg = jnp.concatenate([g_top.astype(BF16), (g - g_top).astype(BF16)], axis=1)
    bb = [_dot(tri, rows(gg, i)) for i in range(G)]
    part = lambda k: jnp.concatenate(
        [x[k * C:(k + 1) * C, :HEAD_DIM] + x[k * C:(k + 1) * C, HEAD_DIM:] for x in bb], axis=0)
    b = part(0)
    fine = {n_: part(k + 1) for k, n_ in enumerate(HGRN_FINE_LEVELS)}

    qb = q.astype(BF16)
    kb = kk.astype(BF16)
    scores, kk_other = ([jnp.zeros((C, C), F32)] * G, None) if other is None else other
    n = 2
    level = 1
    while n <= C:
        a = n // 2 if reverse else n // 2 - 1
        d = fine[n] if n in fine else _neg_abs_anchor_diff(b, n, a, reverse)
        e = jnp.exp2(d).astype(BF16)
        qe = qb * e
        ke = kb * e
        scores = [jnp.where(lv == level, _dot_nt(rows(qe, i), rows(ke, i)), scores[i]) for i in range(G)]
        n *= 2
        level += 1

    b_end = _anchor_rows(b, C, 0 if reverse else C - 1)
    qd = qb * jnp.exp2(b).astype(BF16)
    kd = kb * jnp.exp2(b_end - b).astype(BF16)
    dec = jnp.exp2(b_end)
    kv = [_dot(rows(v, i).T.astype(BF16), rows(kd, i)) for i in range(G)]
    st = st_ref[...]
    o = [None] * G
    for i in (range(G - 1, -1, -1) if reverse else range(G)):
        o[i] = _dot_nt(rows(qd, i), st.astype(BF16))
        st = st * dec[i * C:i * C + 1] + kv[i]
    st_ref[...] = st
    if other is None:
        return jnp.concatenate(o, axis=0), (scores, kk)
    vb = v.astype(BF16)
    diag = jnp.sum(q * (kk + kk_other), axis=-1, keepdims=True) * v
    o = [o[i] + _dot(scores[i].astype(BF16), rows(vb, i)) + rows(diag, i) for i in range(G)]
    return jnp.concatenate(o, axis=0)


def _hgrn_kernel(qc, ic, gc, fc, bc, ql, il, gl, fl, bl, lb_ref, nw_ref, oc_ref, ol_ref, stf, stb):
    C = HGRN_CHUNK
    row = lax.broadcasted_iota(jnp.int32, (C, C), 0)
    col = lax.broadcasted_iota(jnp.int32, (C, C), 1)
    x = row ^ col
    lvl = jnp.zeros((C, C), jnp.int32)
    n = 1
    while n < C:
        lvl = lvl + (x >= n).astype(jnp.int32)
        n *= 2
    lv_f = jnp.where(row > col, lvl, 0)
    lv_b = jnp.where(row < col, lvl, 0)

    def cum_matrix(reverse):
        cum = lambda r: (col >= r) if reverse else (col <= r)
        mats = [cum(row).astype(F32)]
        for n_ in HGRN_FINE_LEVELS:
            anchor = row - row % n_ + (n_ // 2 if reverse else n_ // 2 - 1)
            upstream = (row % n_ < n_ // 2) != reverse
            diff = mats[0] - cum(anchor).astype(F32)
            mats.append(jnp.where(upstream, -diff, diff))
        return jnp.concatenate(mats, axis=0).astype(BF16)

    tri_f = cum_matrix(False)
    tri_b = cum_matrix(True)
    lb_f = lb_ref[0, 0:1, :]
    lb_b = lb_ref[0, 1:2, :]
    nw = nw_ref[...]
    stf[...] = jnp.zeros_like(stf)
    stb[...] = jnp.zeros_like(stb)

    def run(q_ref, i_ref, g_ref, f_ref, b_ref, o_ref):
        ld = lambda ref: ref[0].astype(F32)
        q_raw = ld(q_ref)
        q = q_raw * _sigmoid(q_raw)
        v = ld(i_ref)
        o_f, weights = _hgrn_group(q, v, ld(f_ref), lb_f, stf, lv_f, tri_f, reverse=False)
        o = o_f + _hgrn_group(q, v, ld(b_ref), lb_b, stb, lv_b, tri_b, reverse=True, other=weights)
        o = o * lax.rsqrt(jnp.mean(o * o, axis=-1, keepdims=True) + NORM_EPS) * nw
        g = ld(g_ref)
        o_ref[0] = (o * (g * _sigmoid(g))).astype(o_ref.dtype)

    run(qc, ic, gc, fc, bc, oc_ref)
    run(ql, il, gl, fl, bl, ol_ref)


def _hgrn(hg_ctx, hg_lat, lb, norm_w):
    bsz, lc, n5 = hg_ctx.shape
    ll = hg_lat.shape[1]
    dg = n5 // 5
    nh = dg // HEAD_DIM
    C = HGRN_CHUNK
    assert lc % C == 0 and ll % C == 0 and max(lc, ll) <= HGRN_GROUP * C

    def slab(L, part):
        return pl.BlockSpec((1, L, HEAD_DIM), lambda b, h: (b, 0, part * nh + h))

    return pl.pallas_call(
        _hgrn_kernel,
        out_shape=(jax.ShapeDtypeStruct((bsz, lc, dg), BF16), jax.ShapeDtypeStruct((bsz, ll, dg), BF16)),
        grid=(bsz, nh),
        in_specs=[slab(lc, p) for p in range(5)] + [slab(ll, p) for p in range(5)] + [
            pl.BlockSpec((1, 2, HEAD_DIM), lambda b, h: (h, 0, 0)),
            pl.BlockSpec((1, HEAD_DIM), lambda b, h: (0, 0)),
        ],
        out_specs=(pl.BlockSpec((1, lc, HEAD_DIM), lambda b, h: (b, 0, h)),
                   pl.BlockSpec((1, ll, HEAD_DIM), lambda b, h: (b, 0, h))),
        scratch_shapes=[pltpu.VMEM((HEAD_DIM, HEAD_DIM), F32), pltpu.VMEM((HEAD_DIM, HEAD_DIM), F32)],
        compiler_params=_cparams("parallel", "parallel"),
        name="hgrn2",
    )(*([hg_ctx] * 5), *([hg_lat] * 5), lb, norm_w.reshape(1, HEAD_DIM))


def _first_argmax(vals, idx, sentinel):
    m = jnp.max(vals, axis=0, keepdims=True)
    first = jnp.min(jnp.where(vals == m, idx, sentinel), axis=0, keepdims=True)
    return m, first


def _moe_gates(logits_t, rbias):
    ne, tm = logits_t.shape
    neg = -jnp.inf
    mx = jnp.max(logits_t, axis=0, keepdims=True)
    ex = jnp.exp(logits_t - mx)
    scores = ex / jnp.sum(ex, axis=0, keepdims=True)
    sel = scores + rbias
    eidx = lax.broadcasted_iota(jnp.int32, (ne, tm), 0)
    grp = eidx // EXPERTS_PER_GROUP
    best_s = jnp.full((1, tm), neg, F32)
    best_g = jnp.zeros((1, tm), jnp.int32)
    for gi in range(N_GROUPS):
        mg = jnp.where(grp == gi, sel, neg)
        m1, i1 = _first_argmax(mg, eidx, ne)
        m2 = jnp.max(jnp.where(eidx == i1, neg, mg), axis=0, keepdims=True)
        gs = m1 + m2
        upd = gs > best_s
        best_g = jnp.where(upd, gi, best_g)
        best_s = jnp.where(upd, gs, best_s)
    sg = jnp.where(grp == best_g, sel, neg)
    _, i1 = _first_argmax(sg, eidx, ne)
    _, i2 = _first_argmax(jnp.where(eidx == i1, neg, sg), eidx, ne)
    chosen = (eidx == i1) | (eidx == i2)
    w = jnp.where(chosen, scores, 0.0)
    return w / jnp.sum(w, axis=0, keepdims=True), best_g


def _mix_moe_kernel(x_ref, hy_ref, hg_ref, mod_ref, w_ref, rw_ref, rb_ref, tri_ref, wg_ref, wu_ref, wd_ref,
                    *rest, cap, final):
    fw_ref, o_ref, t_s, g4_s = rest if final else (None,) + rest
    c = hy_ref.shape[-1]
    mix = _dot(hy_ref[0], w_ref[:c, :]) + _dot(hg_ref[0], w_ref[c:, :])
    m = mod_ref[0]
    x1 = x_ref[0] + m[2:3] * mix
    o_ref[0] = x1
    xn = x1 * lax.rsqrt(jnp.mean(x1 * x1, axis=-1, keepdims=True) + NORM_EPS)
    t = xn * (1.0 + m[4:5]) + m[3:4]
    t_hi = t.astype(BF16)
    t_s[...] = t_hi
    t_lo = (t - t_hi.astype(F32)).astype(BF16)
    r = _dot(t_hi, rw_ref[...])
    lg = r[:, :LANES] + r[:, LANES:] + _dot(t_lo, rw_ref[:, :LANES])
    ne, _, f = wg_ref.shape
    tm = t.shape[0]
    per_group = ne // N_GROUPS
    gates_t, best_g = _moe_gates(lg.T[:ne], rb_ref[...])

    row8 = lax.broadcasted_iota(jnp.int32, (8, tm), 0)
    member = row8 == best_g
    before = _dot(member.astype(BF16), tri_ref[...])
    rank_row = jnp.sum(jnp.where(member, before, 0.0), axis=0, keepdims=True)
    grp_row = best_g.astype(F32)
    count = jnp.sum(member.astype(F32), axis=1, keepdims=True)
    g4_t = gates_t
    for g in range(1, N_GROUPS):
        g4_t = g4_t + pltpu.roll(gates_t, g * per_group, 0)
    row_e = lax.broadcasted_iota(jnp.int32, (ne, tm), 0)
    info_t = jnp.where(row_e < per_group, g4_t,
                       jnp.where(row_e == per_group, rank_row, jnp.where(row_e == per_group + 1, grp_row, 0.0)))
    info = jnp.concatenate([info_t, jnp.zeros((LANES - ne, tm), F32)], axis=0).T
    g4_s[...] = info.astype(BF16)
    rank_col = info[:, per_group:per_group + 1]
    grp_col = info[:, per_group + 1:per_group + 2]
    g2 = m[5:6]

    slot_col = lax.broadcasted_iota(jnp.int32, (cap, 1), 0).astype(F32)
    slot_row = lax.broadcasted_iota(jnp.int32, (1, cap), 1).astype(F32)
    for g in range(N_GROUPS):
        n_g = count[g, 0].astype(jnp.int32)

        def body(s, carry, g=g):
            base = (s * cap).astype(F32)
            pick = ((rank_row - base == slot_col) & (grp_row == float(g))).astype(BF16)
            xg = _dot(pick, t_s[...]).astype(BF16)
            gs = _dot(pick, g4_s[...])
            acts = []
            for j in range(per_group):
                e = g * per_group + j
                hgate = _dot(xg, wg_ref[e])
                hup = _dot(xg, wu_ref[e])
                acts.append((hgate * _sigmoid(hgate) * hup * gs[:, j:j + 1]).astype(BF16))
            y = _dot(jnp.concatenate(acts, axis=1), wd_ref[g * per_group * f:(g + 1) * per_group * f, :])
            put = ((rank_col - base == slot_row) & (grp_col == float(g))).astype(BF16)
            o_ref[0] += _dot(put, (y * g2).astype(BF16))
            return carry

        lax.fori_loop(0, (n_g + cap - 1) // cap, body, 0)

    if final:
        o = o_ref[0]
        o_ref[0] = o * lax.rsqrt(jnp.mean(o * o, axis=-1, keepdims=True) + NORM_EPS) * fw_ref[...]


def _mix_moe(x, hy, hg, mod_l, mod_row, layer, w_out, router_w, rbias, wg, wu, wd, *, tm, final_w=None):
    bsz, L, d = x.shape
    c = hy.shape[-1]
    cg = hg.shape[-1]
    _, ne, _, f = wg.shape
    rw_hi = router_w.astype(BF16)
    rw_lo = (router_w - rw_hi.astype(F32)).astype(BF16)
    rw = jnp.zeros((d, 2 * LANES), BF16).at[:, :ne].set(rw_hi).at[:, LANES:LANES + ne].set(rw_lo)
    cap = 5 * tm // 16
    assert cap % 8 == 0
    pos = jnp.arange(tm, dtype=jnp.int32)
    tri = (pos[:, None] < pos[None, :]).astype(BF16)
    final = final_w is not None
    extra_specs = [_resident((1, d))] if final else []
    extra_args = [final_w.reshape(1, d)] if final else []
    return pl.pallas_call(
        functools.partial(_mix_moe_kernel, cap=cap, final=final),
        out_shape=jax.ShapeDtypeStruct((bsz, L, d), F32),
        grid=(bsz, L // tm),
        in_specs=[
            pl.BlockSpec((1, tm, d), lambda b, i: (b, i, 0)),
            pl.BlockSpec((1, tm, c), lambda b, i: (b, i, 0)),
            pl.BlockSpec((1, tm, cg), lambda b, i: (b, i, 0)),
            pl.BlockSpec((1, N_MOD, d), lambda b, i: (mod_row(b), 0, 0)),
            _resident((c + cg, d), layer),
            _resident((d, 2 * LANES)),
            _resident((ne, 1)),
            _resident((tm, tm)),
            _resident((ne, d, f), layer),
            _resident((ne, d, f), layer),
            _resident((ne * f, d), layer),
        ] + extra_specs,
        out_specs=pl.BlockSpec((1, tm, d), lambda b, i: (b, i, 0)),
        scratch_shapes=[pltpu.VMEM((tm, d), BF16), pltpu.VMEM((tm, LANES), BF16)],
        compiler_params=_cparams("parallel", "parallel"),
        name="mix_moe",
    )(x, hy, hg, mod_l, w_out, rw, rbias.reshape(ne, 1), tri, wg, wu, wd.reshape(-1, ne * f, d), *extra_args)


def kernel(x, c, ctx, c_ctx, w_mod, b_mod, w_in, w_out, hy_conv_w, hy_conv_b, hy_w1, hy_b1, hy_w2, hy_b2, hy_w3, hy_b3, hy_bias, hgrn_lower_bounds, hgrn_norm_w, router_w, router_bias, moe_w_gate, moe_w_up, moe_w_down, final_norm_w):
    bsz, seq, d = x.shape
    ctx_len = ctx.shape[1]
    depth = w_mod.shape[0]
    dg = hgrn_lower_bounds.shape[-1]
    nh = dg // HEAD_DIM
    assert bsz + 1 <= MOD_ROWS and seq % GRID_W == 0

    cc = jnp.zeros((MOD_ROWS, d), F32).at[:bsz].set(c).at[bsz].set(c_ctx)
    mod = _modulation(cc, w_mod, b_mod).reshape(depth, MOD_ROWS, N_MOD, d)
    lat_row = lambda b: b
    ctx_row = lambda b: bsz

    lb_soft = jax.nn.softmax(hgrn_lower_bounds.astype(F32), axis=1)
    lower = jnp.cumsum(lb_soft, axis=1) - lb_soft[:, :1]
    lower = lower.reshape(2, depth, nh, HEAD_DIM).transpose(1, 2, 0, 3)

    dft_lat = _dft_matrices(seq // _hyena_blocks(seq))
    dft_ctx = _dft_matrices(ctx_len // _hyena_blocks(ctx_len))
    tm_lat = min(seq, 1024)
    tm_moe = min(seq, 512)

    w_in_b = w_in.astype(BF16)
    moe_w = (w_out.astype(BF16), router_w, router_bias,
             moe_w_gate.astype(BF16), moe_w_up.astype(BF16), moe_w_down.astype(BF16))

    xc = ctx
    for l in range(depth):
        last = l == depth - 1
        u_lat, hg_lat = _inproj(x, mod[l], lat_row, w_in_b, l, hy_conv_w[l], hy_conv_b[l],
                                period=GRID_W, tm=tm_lat)
        u_ctx, hg_ctx = _inproj(xc, mod[l], ctx_row, w_in_b, l, hy_conv_w[l], hy_conv_b[l],
                                period=ctx_len, tm=ctx_len)
        o_ctx, o_lat = _hgrn(hg_ctx, hg_lat, lower[l], hgrn_norm_w[l])
        taps_w = (hy_w1[l], hy_b1[l], hy_w2[l], hy_b2[l], hy_w3[l], hy_b3[l])
        hy_lat = _hyena(u_lat, dft_lat, taps_w, hy_bias[l])
        x = _mix_moe(x, hy_lat, o_lat, mod[l], lat_row, l, *moe_w, tm=tm_moe,
                     final_w=final_norm_w if last else None)
        if not last:
            hy_ctx = _hyena(u_ctx, dft_ctx, taps_w, hy_bias[l])
            xc = _mix_moe(xc, hy_ctx, o_ctx, mod[l], ctx_row, l, *moe_w, tm=ctx_len)

    return x
```

```python
import functools
import math

import numpy as np
import jax
import jax.numpy as jnp
from jax import lax
from jax.experimental import pallas as pl
from jax.experimental.pallas import tpu as pltpu

F32 = jnp.float32
BF16 = jnp.bfloat16
HIGHEST = lax.Precision.HIGHEST

GRID_W = 64
NORM_EPS = 1e-6
N_MOD = 6
HY_ORDER = 2
HY_BANDS = 16
HY_TARGET = 1e-2
HY_FAST_PCT = 0.3
HY_SLOW_PCT = 1.5
HEAD_DIM = 128
HY_BLOCKS = 2
N_EXPERTS = 16
N_GROUPS = 4
EXPERTS_PER_GROUP = N_EXPERTS // N_GROUPS
LANES = 128
MOD_ROWS = 16
HGRN_CHUNK = 128
HGRN_GROUP = 16
HGRN_FINE_LEVELS = (2, 4)
VMEM_LIMIT = 56 << 20


def _cparams(*sem):
    return pltpu.CompilerParams(dimension_semantics=sem, vmem_limit_bytes=VMEM_LIMIT)


def _sigmoid(x):
    return 1.0 / (1.0 + jnp.exp2(x * (-1.0 / math.log(2.0))))


def _dot(a, b, **kw):
    return jnp.dot(a, b, preferred_element_type=F32, **kw)


def _resident(shape, layer=None):
    if layer is None:
        return pl.BlockSpec(shape, lambda *_: (0,) * len(shape), pipeline_mode=pl.Buffered(1))
    return pl.BlockSpec((None,) + tuple(shape), lambda *_: (layer,) + (0,) * len(shape),
                        pipeline_mode=pl.Buffered(1))


def _dot_nt(a, b):
    return lax.dot_general(a, b, (((1,), (1,)), ((), ())), preferred_element_type=F32)


def _mod_kernel(c_ref, w_ref, b_ref, o_ref):
    c = c_ref[...]
    cs = c * _sigmoid(c)
    o_ref[0] = _dot(cs, w_ref[0], precision=HIGHEST) + b_ref[0]


def _modulation(cc, w_mod, b_mod):
    depth, d, n = w_mod.shape
    tn = n // 3
    return pl.pallas_call(
        _mod_kernel,
        out_shape=jax.ShapeDtypeStruct((depth, MOD_ROWS, n), F32),
        grid=(depth, n // tn),
        in_specs=[
            pl.BlockSpec((MOD_ROWS, d), lambda l, j: (0, 0)),
            pl.BlockSpec((1, d, tn), lambda l, j: (l, 0, j)),
            pl.BlockSpec((1, 1, tn), lambda l, j: (l, 0, j)),
        ],
        out_specs=pl.BlockSpec((1, MOD_ROWS, tn), lambda l, j: (l, 0, j)),
        compiler_params=_cparams("parallel", "parallel"),
        name="modulation",
    )(cc, w_mod, b_mod.reshape(depth, 1, n))


def _dft_kernel(ac_ref, as_ref, bc_ref, bs_ref, c_ref, s_ref, *, n1):
    bc = bc_ref[...]
    bs = bs_ref[...]
    ac = ac_ref[...]
    as_ = as_ref[...]
    for j in range(n1):
        a_c = ac[:, j:j + 1]
        a_s = as_[:, j:j + 1]
        c_ref[:, j * LANES:(j + 1) * LANES] = (a_c * bc - a_s * bs).astype(BF16)
        s_ref[:, j * LANES:(j + 1) * LANES] = (a_s * bc + a_c * bs).astype(BF16)


def _dft_tables(L):
    n1 = L // LANES
    period = 4 * L
    r = np.arange(L, dtype=np.int64)[:, None]
    c1 = np.arange(n1, dtype=np.int64)[None, :]
    c0 = np.arange(LANES, dtype=np.int64)[None, :]

    def cs(phase):
        ang = (phase % period).astype(np.float64) * (2.0 * np.pi / period)
        return np.cos(ang).astype(np.float32), np.sin(ang).astype(np.float32)

    fa = cs((2 * r + 1) * (LANES * c1))
    fb = cs((2 * r + 1) * c0)
    ta = cs((2 * LANES * c1) * r)
    tb = cs((2 * c0 + 1) * r)
    return (fa, fb), (ta, tb)


def _dft_matrices(L):
    n1 = L // LANES
    tr = min(L, 256)
    outs = []
    for (ac, as_), (bc, bs) in _dft_tables(L):
        c, s = pl.pallas_call(
            functools.partial(_dft_kernel, n1=n1),
            out_shape=(jax.ShapeDtypeStruct((L, L), BF16), jax.ShapeDtypeStruct((L, L), BF16)),
            grid=(L // tr,),
            in_specs=[
                pl.BlockSpec((tr, n1), lambda i: (i, 0)),
                pl.BlockSpec((tr, n1), lambda i: (i, 0)),
                pl.BlockSpec((tr, LANES), lambda i: (i, 0)),
                pl.BlockSpec((tr, LANES), lambda i: (i, 0)),
            ],
            out_specs=(pl.BlockSpec((tr, L), lambda i: (i, 0)), pl.BlockSpec((tr, L), lambda i: (i, 0))),
            compiler_params=_cparams("parallel"),
            name="dft_tables",
        )(jnp.asarray(ac), jnp.asarray(as_), jnp.asarray(bc), jnp.asarray(bs))
        outs.append((c, s))
    return outs


def _inproj_kernel(x_ref, mod_ref, w_ref, cw_ref, cb_ref, u_ref, hg_ref, *, period, n_hy, tn):
    x = x_ref[0]
    tm = x.shape[0]
    xn = x * lax.rsqrt(jnp.mean(x * x, axis=-1, keepdims=True) + NORM_EPS)
    m = mod_ref[0]
    a = (xn * (1.0 + m[1:2]) + m[0:1]).astype(BF16)
    pos = lax.broadcasted_iota(jnp.int32, (tm, 1), 0) % period
    first = pos == 0
    last = pos == period - 1
    n_total = w_ref.shape[1]
    for j in range(n_total // tn):
        p = _dot(a, w_ref[:, j * tn:(j + 1) * tn])
        if j * tn < n_hy:
            cw = cw_ref[:, j * tn:(j + 1) * tn]
            prev = jnp.where(first, 0.0, pltpu.roll(p, 1, 0))
            nxt = jnp.where(last, 0.0, pltpu.roll(p, tm - 1, 0))
            u_ref[0, :, j * tn:(j + 1) * tn] = (
                prev * cw[0:1] + p * cw[1:2] + nxt * cw[2:3] + cb_ref[:, j * tn:(j + 1) * tn]
            ).astype(u_ref.dtype)
        else:
            hg_ref[0, :, j * tn - n_hy:(j + 1) * tn - n_hy] = p.astype(hg_ref.dtype)


def _inproj(x, mod_l, mod_row, w_in, layer, conv_w, conv_b, *, period, tm):
    bsz, L, d = x.shape
    n_hy = conv_w.shape[1]
    n_all = w_in.shape[-1]
    tn = n_hy // 3
    assert tm % period == 0 or period == L == tm
    return pl.pallas_call(
        functools.partial(_inproj_kernel, period=period, n_hy=n_hy, tn=tn),
        out_shape=(jax.ShapeDtypeStruct((bsz, L, n_hy), BF16),
                   jax.ShapeDtypeStruct((bsz, L, n_all - n_hy), BF16)),
        grid=(bsz, L // tm),
        in_specs=[
            pl.BlockSpec((1, tm, d), lambda b, i: (b, i, 0)),
            pl.BlockSpec((1, N_MOD, d), lambda b, i: (mod_row(b), 0, 0)),
            _resident((d, n_all), layer),
            _resident((3, n_hy)),
            _resident((1, n_hy)),
        ],
        out_specs=(pl.BlockSpec((1, tm, n_hy), lambda b, i: (b, i, 0)),
                   pl.BlockSpec((1, tm, n_all - n_hy), lambda b, i: (b, i, 0))),
        compiler_params=_cparams("parallel", "parallel"),
        name="inproj",
    )(x, mod_l, w_in, conv_w, conv_b.reshape(1, n_hy))


def _rev_rows(x):
    nblk = x.shape[0] // LANES
    r = lax.broadcasted_iota(jnp.int32, (LANES, LANES), 0)
    c_ = lax.broadcasted_iota(jnp.int32, (LANES, LANES), 1)
    exch = (r + c_ == LANES - 1).astype(BF16)
    xb = x.astype(BF16)
    return jnp.concatenate(
        [_dot(exch, xb[(nblk - 1 - i) * LANES:(nblk - i) * LANES]) for i in range(nblk)], axis=0)


def _filter_kernel(z_ref, w1_ref, b1_ref, w2_ref, b2_ref, w3f_ref, b3f_ref, w3b_ref, b3b_ref,
                   dl_ref, kap_ref, rev_ref, h_s):
    z = z_ref[...]
    L = z.shape[0]
    hid = w2_ref.shape[0]

    @pl.when((pl.program_id(0) == 0) & (pl.program_id(1) == 0))
    def _():
        h1 = jnp.sin(_dot(z, w1_ref[...], precision=HIGHEST) + b1_ref[...])
        h2 = jnp.sin(_dot(h1, w2_ref[...], precision=HIGHEST) + b2_ref[...])
        h_hi = h2.astype(BF16)
        h_s[...] = jnp.concatenate([h_hi, (h2 - h_hi.astype(F32)).astype(BF16)], axis=1)

    def last_layer(w_ref, b_ref):
        return (_dot(h_s[...], w_ref[:2 * hid, :]) + _dot(h_s[:, :hid], w_ref[2 * hid:, :])) + b_ref[...]

    hf = last_layer(w3f_ref, b3f_ref)
    hb = last_layer(w3b_ref, b3b_ref)
    win = jnp.exp(-z[:, 0:1] * dl_ref[...])
    hf = hf * win
    hb = hb * win
    nrm = (jnp.sum(jnp.abs(hf), axis=0, keepdims=True)
           + jnp.sum(jnp.abs(hb), axis=0, keepdims=True))
    inv = 1.0 / nrm
    hf = hf * inv
    hb = hb * inv
    first = lax.broadcasted_iota(jnp.int32, (L, 1), 0) == 0
    down1 = lambda y: jnp.where(first, 0.0, pltpu.roll(y, 1, 0))
    kap_ref[0, 0:L, :] = down1(_rev_rows(hb)).astype(BF16)
    kap_ref[0, L:2 * L, :] = hf.astype(BF16)
    rev_ref[0, 0:L, :] = down1(_rev_rows(hf)).astype(BF16)
    rev_ref[0, L:2 * L, :] = jnp.where(first, hf[0:1], hb).astype(BF16)


def _hyena_filter_taps(L, w1, b1, w2, b2, w3, b3):
    nfeat, hid = w1.shape
    c = w3.shape[1] // (2 * HY_ORDER)
    tc = min(c, 256)
    nct = c // tc
    t = jnp.linspace(0.0, 1.0, L, dtype=F32)
    n = jnp.arange(L, dtype=F32)
    freqs = jnp.linspace(1e-4, HY_BANDS - 1, HY_BANDS, dtype=F32)
    ang = (2.0 * math.pi / L) * n[:, None] * freqs[None, :]
    z = jnp.concatenate([t[:, None], jnp.cos(ang), -jnp.sin(ang)], axis=-1)
    z = jnp.pad(z, ((0, 0), (0, LANES - nfeat)))
    w1p = jnp.pad(w1, ((0, LANES - nfeat), (0, 0)))
    deltas = jnp.abs(jnp.linspace(math.log(HY_TARGET) / HY_FAST_PCT, math.log(HY_TARGET) / HY_SLOW_PCT,
                                  c, dtype=F32)).reshape(1, c)
    w3_hi = w3.astype(BF16)
    w3p = jnp.concatenate([w3_hi, w3_hi, (w3 - w3_hi.astype(F32)).astype(BF16)], axis=0)
    full = lambda shape: pl.BlockSpec(shape, lambda o, j: (0,) * len(shape))
    out = jax.ShapeDtypeStruct((HY_ORDER, 2 * L, c), BF16)
    return pl.pallas_call(
        _filter_kernel,
        out_shape=(out, out),
        grid=(HY_ORDER, nct),
        in_specs=[
            full((L, LANES)), full((LANES, hid)), full((1, hid)), full((hid, hid)), full((1, hid)),
            pl.BlockSpec((3 * hid, tc), lambda o, j: (0, o * 2 * nct + j)),
            pl.BlockSpec((1, tc), lambda o, j: (0, o * 2 * nct + j)),
            pl.BlockSpec((3 * hid, tc), lambda o, j: (0, o * 2 * nct + nct + j)),
            pl.BlockSpec((1, tc), lambda o, j: (0, o * 2 * nct + nct + j)),
            pl.BlockSpec((1, tc), lambda o, j: (0, j)),
        ],
        out_specs=(pl.BlockSpec((1, 2 * L, tc), lambda o, j: (o, 0, j)),
                   pl.BlockSpec((1, 2 * L, tc), lambda o, j: (o, 0, j))),
        scratch_shapes=[pltpu.VMEM((L, 2 * hid), BF16)],
        compiler_params=_cparams("arbitrary", "arbitrary"),
        name="hyena_filter",
    )(z, w1p, b1.reshape(1, hid), w2, b2.reshape(1, hid), w3p, b3.reshape(1, -1), w3p, b3.reshape(1, -1),
      deltas)


def _kspec_kernel(cf_ref, sf_ref, pos_ref, neg_ref, o_ref, *, scale):
    pos = pos_ref[0].astype(F32)
    neg = neg_ref[0].astype(F32)
    first = lax.broadcasted_iota(jnp.int32, (pos.shape[0], 1), 0) == 0
    neg = jnp.where(first, 0.0, neg)
    o_ref[0, 0, 0] = _dot(cf_ref[...], (pos + neg).astype(BF16)) * scale
    o_ref[0, 0, 1] = _dot(sf_ref[...], (pos - neg).astype(BF16)) * scale


def _filter_spectrum(cft, sft, kap, rev, nb):
    _, two_l, c = kap.shape
    bk = two_l // (2 * nb)
    n_off = 2 * nb - 1
    return pl.pallas_call(
        functools.partial(_kspec_kernel, scale=1.0 / bk),
        out_shape=jax.ShapeDtypeStruct((HY_ORDER, n_off, 2, bk, c), F32),
        grid=(HY_ORDER, n_off),
        in_specs=[
            _resident((bk, bk)),
            _resident((bk, bk)),
            pl.BlockSpec((1, bk, c), lambda o, k: (o, k + 1, 0)),
            pl.BlockSpec((1, bk, c), lambda o, k: (o, 2 * nb - 1 - k, 0)),
        ],
        out_specs=pl.BlockSpec((1, 1, 2, bk, c), lambda o, k: (o, k, 0, 0, 0)),
        compiler_params=_cparams("parallel", "parallel"),
        name="hyena_filter_spectrum",
    )(cft, sft, kap, rev)


def _hy_fwd_kernel(cf_ref, sf_ref, v_ref, a_ref, b_ref, *, nb, bk):
    for j in range(nb):
        rows = slice(j * bk, (j + 1) * bk)
        v = v_ref[0, rows, :].astype(BF16)
        a_ref[0, rows, :] = _dot(cf_ref[...], v).astype(BF16)
        b_ref[0, rows, :] = _dot(sf_ref[...], v).astype(BF16)


def _hy_forward(cft, sft, src, src_col, c, nb):
    bsz, L, _ = src.shape
    bk = L // nb
    out = jax.ShapeDtypeStruct((bsz, L, c), BF16)
    return pl.pallas_call(
        functools.partial(_hy_fwd_kernel, nb=nb, bk=bk),
        out_shape=(out, out),
        grid=(bsz,),
        in_specs=[
            _resident((bk, bk)),
            _resident((bk, bk)),
            pl.BlockSpec((1, L, c), lambda b: (b, 0, src_col)),
        ],
        out_specs=(pl.BlockSpec((1, L, c), lambda b: (b, 0, 0)), pl.BlockSpec((1, L, c), lambda b: (b, 0, 0))),
        compiler_params=_cparams("parallel"),
        name="hyena_spectrum",
    )(cft, sft, src)


def _hy_inv_kernel(cf_ref, sf_ref, a_ref, b_ref, k_ref, gate_ref, src_ref, d_ref, o_ref, *, nb, bk):
    for i in range(nb):
        p = q = None
        for j in range(nb):
            rows = slice(j * bk, (j + 1) * bk)
            a = a_ref[0, rows, :].astype(F32)
            b = b_ref[0, rows, :].astype(F32)
            kr = k_ref[i - j + nb - 1, 0]
            ks = k_ref[i - j + nb - 1, 1]
            pj = a * kr - b * ks
            qj = a * ks + b * kr
            p = pj if p is None else p + pj
            q = qj if q is None else q + qj
        conv = _dot(cf_ref[...], p.astype(BF16)) + _dot(sf_ref[...], q.astype(BF16))
        rows = slice(i * bk, (i + 1) * bk)
        o_ref[0, rows, :] = (gate_ref[0, rows, :] * (conv + src_ref[0, rows, :] * d_ref[0])).astype(o_ref.dtype)


def _hy_inverse(cf, sf, a, b, kspec, gate, gate_col, src, src_col, d, order):
    bsz, L, c = a.shape
    _, n_off, _, bk, _ = kspec.shape
    nb = (n_off + 1) // 2
    return pl.pallas_call(
        functools.partial(_hy_inv_kernel, nb=nb, bk=bk),
        out_shape=jax.ShapeDtypeStruct((bsz, L, c), BF16),
        grid=(bsz,),
        in_specs=[
            _resident((bk, bk)),
            _resident((bk, bk)),
            pl.BlockSpec((1, L, c), lambda b: (b, 0, 0)),
            pl.BlockSpec((1, L, c), lambda b: (b, 0, 0)),
            _resident((n_off, 2, bk, c), order),
            pl.BlockSpec((1, L, c), lambda b: (b, 0, gate_col)),
            pl.BlockSpec((1, L, c), lambda b: (b, 0, src_col)),
            pl.BlockSpec((1, 1, c), lambda b: (order, 0, 0)),
        ],
        out_specs=pl.BlockSpec((1, L, c), lambda b: (b, 0, 0)),
        compiler_params=_cparams("parallel"),
        name="hyena_inverse",
    )(cf, sf, a, b, kspec, gate, src, d.reshape(HY_ORDER, 1, c))


def _hyena_blocks(L):
    bk = max(L // HY_BLOCKS, LANES)
    return L // bk


def _hyena(u, dft, taps_w, d):
    (cft, sft), (cf, sf) = dft
    L = u.shape[1]
    c = u.shape[2] // 3
    nb = _hyena_blocks(L)
    kap, rev = _hyena_filter_taps(L, *taps_w)
    kspec = _filter_spectrum(cft, sft, kap, rev, nb)
    a, b = _hy_forward(cft, sft, u, 0, c, nb)
    z = _hy_inverse(cf, sf, a, b, kspec, u, 1, u, 0, d, 0)
    a, b = _hy_forward(cft, sft, z, 0, c, nb)
    return _hy_inverse(cf, sf, a, b, kspec, u, 2, z, 0, d, 1)


def _anchor_rows(b, n, a):
    assert n % 8 == 0
    parts = [jnp.broadcast_to(b[s + a:s + a + 1, :], (n, b.shape[1])) for s in range(0, b.shape[0], n)]
    return parts[0] if len(parts) == 1 else jnp.concatenate(parts, axis=0)


def _neg_abs_anchor_diff(b, n, a, reverse):
    h = n // 2
    m = _anchor_rows(b, n, a)
    if h % 8:
        bits = lax.bitcast_convert_type(b - m, jnp.uint32) | jnp.uint32(0x80000000)
        return lax.bitcast_convert_type(bits, F32)
    pieces = []
    for s in range(0, b.shape[0], h):
        first_half = (s // h) % 2 == 0
        upstream = first_half != reverse
        x, y = (m, b) if upstream else (b, m)
        pieces.append(x[s:s + h] - y[s:s + h])
    return jnp.concatenate(pieces, axis=0)


def _hgrn_weights(q, f_logit, lb, lv, tri, scores, *, reverse):
    C = HGRN_CHUNK
    G = q.shape[0] // C
    rows = lambda x, i: x[i * C:(i + 1) * C]
    f = lb + (1.0 - lb) * _sigmoid(f_logit)
    kk = 1.0 - f
    g = jnp.log2(f)
    g_top = lax.bitcast_convert_type(
        lax.bitcast_convert_type(g, jnp.uint32) & jnp.uint32(0xFFFF0000), F32)
    gg = jnp.concatenate([g_top.astype(BF16), (g - g_top).astype(BF16)], axis=1)
    bb = [_dot(tri, rows(gg, i)) for i in range(G)]
    part = lambda k: jnp.concatenate(
        [x[k * C:(k + 1) * C, :HEAD_DIM] + x[k * C:(k + 1) * C, HEAD_DIM:] for x in bb], axis=0)
    b = part(0)
    fine = {n_: part(k + 1) for k, n_ in enumerate(HGRN_FINE_LEVELS)}

    qb = q.astype(BF16)
    kb = kk.astype(BF16)
    n = 2
    level = 1
    while n <= C:
        a = n // 2 if reverse else n // 2 - 1
        d = fine[n] if n in fine else _neg_abs_anchor_diff(b, n, a, reverse)
        e = jnp.exp2(d).astype(BF16)
        qe = qb * e
        ke = kb * e
        scores = [jnp.where(lv == level, _dot_nt(rows(qe, i), rows(ke, i)), scores[i]) for i in range(G)]
        n *= 2
        level += 1

    b_end = _anchor_rows(b, C, 0 if reverse else C - 1)
    qd = qb * jnp.exp2(b).astype(BF16)
    kd = kb * jnp.exp2(b_end - b).astype(BF16)
    return scores, kk, qd, kd, jnp.exp2(b_end)


def _hgrn_kernel(qc, ic, gc, fc, bc, ql, il, gl, fl, bl, lb_ref, nw_ref, oc_ref, ol_ref, stf, stb):
    C = HGRN_CHUNK
    row = lax.broadcasted_iota(jnp.int32, (C, C), 0)
    col = lax.broadcasted_iota(jnp.int32, (C, C), 1)
    x = row ^ col
    lvl = jnp.zeros((C, C), jnp.int32)
    n = 1
    while n < C:
        lvl = lvl + (x >= n).astype(jnp.int32)
        n *= 2
    lv_f = jnp.where(row > col, lvl, 0)
    lv_b = jnp.where(row < col, lvl, 0)

    def cum_matrix(reverse):
        cum = lambda r: (col >= r) if reverse else (col <= r)
        mats = [cum(row).astype(F32)]
        for n_ in HGRN_FINE_LEVELS:
            anchor = row - row % n_ + (n_ // 2 if reverse else n_ // 2 - 1)
            upstream = (row % n_ < n_ // 2) != reverse
            diff = mats[0] - cum(anchor).astype(F32)
            mats.append(jnp.where(upstream, -diff, diff))
        return jnp.concatenate(mats, axis=0).astype(BF16)

    tri_f = cum_matrix(False)
    tri_b = cum_matrix(True)
    lb_f = lb_ref[0, 0:1, :]
    lb_b = lb_ref[0, 1:2, :]
    nw = nw_ref[...]
    stf[...] = jnp.zeros_like(stf)
    stb[...] = jnp.zeros_like(stb)

    def run(q_ref, i_ref, g_ref, f_ref, b_ref, o_ref):
        ld = lambda ref: ref[0].astype(F32)
        q_raw = ld(q_ref)
        q = q_raw * _sigmoid(q_raw)
        v = ld(i_ref)
        G = q.shape[0] // C
        rows = lambda x, i: x[i * C:(i + 1) * C]
        scores = [jnp.zeros((C, C), F32)] * G
        scores, k_f, qd_f, kd_f, dec_f = _hgrn_weights(q, ld(f_ref), lb_f, lv_f, tri_f, scores, reverse=False)
        scores, k_b, qd_b, kd_b, dec_b = _hgrn_weights(q, ld(b_ref), lb_b, lv_b, tri_b, scores, reverse=True)
        kd = jnp.concatenate([kd_f, kd_b], axis=1)
        qd = jnp.concatenate([qd_f, qd_b], axis=1)
        kv = [_dot(rows(v, i).T.astype(BF16), rows(kd, i)) for i in range(G)]
        before = [None] * G
        st = stf[...]
        for i in range(G):
            before[i] = st
            st = st * dec_f[i * C:i * C + 1] + kv[i][:, :HEAD_DIM]
        stf[...] = st
        st = stb[...]
        for i in range(G - 1, -1, -1):
            before[i] = jnp.concatenate([before[i], st], axis=1).astype(BF16)
            st = st * dec_b[i * C:i * C + 1] + kv[i][:, HEAD_DIM:]
        stb[...] = st
        vb = v.astype(BF16)
        diag = jnp.sum(q * (k_f + k_b), axis=-1, keepdims=True) * v
        o = jnp.concatenate(
            [_dot(scores[i].astype(BF16), rows(vb, i)) + _dot_nt(rows(qd, i), before[i]) for i in range(G)],
            axis=0) + diag
        o = o * lax.rsqrt(jnp.mean(o * o, axis=-1, keepdims=True) + NORM_EPS) * nw
        g = ld(g_ref)
        o_ref[0] = (o * (g * _sigmoid(g))).astype(o_ref.dtype)

    run(qc, ic, gc, fc, bc, oc_ref)
    run(ql, il, gl, fl, bl, ol_ref)


def _hgrn(hg_ctx, hg_lat, lb, norm_w):
    bsz, lc, n5 = hg_ctx.shape
    ll = hg_lat.shape[1]
    dg = n5 // 5
    nh = dg // HEAD_DIM
    C = HGRN_CHUNK
    assert lc % C == 0 and ll % C == 0 and max(lc, ll) <= HGRN_GROUP * C

    def slab(L, part):
        return pl.BlockSpec((1, L, HEAD_DIM), lambda b, h: (b, 0, part * nh + h))

    return pl.pallas_call(
        _hgrn_kernel,
        out_shape=(jax.ShapeDtypeStruct((bsz, lc, dg), BF16), jax.ShapeDtypeStruct((bsz, ll, dg), BF16)),
        grid=(bsz, nh),
        in_specs=[slab(lc, p) for p in range(5)] + [slab(ll, p) for p in range(5)] + [
            pl.BlockSpec((1, 2, HEAD_DIM), lambda b, h: (h, 0, 0)),
            pl.BlockSpec((1, HEAD_DIM), lambda b, h: (0, 0)),
        ],
        out_specs=(pl.BlockSpec((1, lc, HEAD_DIM), lambda b, h: (b, 0, h)),
                   pl.BlockSpec((1, ll, HEAD_DIM), lambda b, h: (b, 0, h))),
        scratch_shapes=[pltpu.VMEM((HEAD_DIM, HEAD_DIM), F32), pltpu.VMEM((HEAD_DIM, HEAD_DIM), F32)],
        compiler_params=_cparams("parallel", "parallel"),
        name="hgrn2",
    )(*([hg_ctx] * 5), *([hg_lat] * 5), lb, norm_w.reshape(1, HEAD_DIM))


def _first_argmax(vals, idx, sentinel):
    m = jnp.max(vals, axis=0, keepdims=True)
    first = jnp.min(jnp.where(vals == m, idx, sentinel), axis=0, keepdims=True)
    return m, first


def _moe_gates(logits_t, rbias):
    ne, tm = logits_t.shape
    neg = -jnp.inf
    mx = jnp.max(logits_t, axis=0, keepdims=True)
    ex = jnp.exp(logits_t - mx)
    scores = ex / jnp.sum(ex, axis=0, keepdims=True)
    sel = scores + rbias
    eidx = lax.broadcasted_iota(jnp.int32, (ne, tm), 0)
    grp = eidx // EXPERTS_PER_GROUP
    best_s = jnp.full((1, tm), neg, F32)
    best_g = jnp.zeros((1, tm), jnp.int32)
    for gi in range(N_GROUPS):
        mg = jnp.where(grp == gi, sel, neg)
        m1, i1 = _first_argmax(mg, eidx, ne)
        m2 = jnp.max(jnp.where(eidx == i1, neg, mg), axis=0, keepdims=True)
        gs = m1 + m2
        upd = gs > best_s
        best_g = jnp.where(upd, gi, best_g)
        best_s = jnp.where(upd, gs, best_s)
    sg = jnp.where(grp == best_g, sel, neg)
    _, i1 = _first_argmax(sg, eidx, ne)
    _, i2 = _first_argmax(jnp.where(eidx == i1, neg, sg), eidx, ne)
    chosen = (eidx == i1) | (eidx == i2)
    w = jnp.where(chosen, scores, 0.0)
    return w / jnp.sum(w, axis=0, keepdims=True), best_g


def _mix_moe_kernel(x_ref, hy_ref, hg_ref, mod_ref, w_ref, rw_ref, rb_ref, tri_ref, wg_ref, wu_ref, wd_ref,
                    *rest, cap, final):
    fw_ref, o_ref, t_s, g4_s = rest if final else (None,) + rest
    c = hy_ref.shape[-1]
    mix = _dot(hy_ref[0], w_ref[:c, :]) + _dot(hg_ref[0], w_ref[c:, :])
    m = mod_ref[0]
    x1 = x_ref[0] + m[2:3] * mix
    o_ref[0] = x1
    xn = x1 * lax.rsqrt(jnp.mean(x1 * x1, axis=-1, keepdims=True) + NORM_EPS)
    t = xn * (1.0 + m[4:5]) + m[3:4]
    t_hi = t.astype(BF16)
    t_s[...] = t_hi
    t_lo = (t - t_hi.astype(F32)).astype(BF16)
    r = _dot(t_hi, rw_ref[...])
    lg = r[:, :LANES] + r[:, LANES:] + _dot(t_lo, rw_ref[:, :LANES])
    ne, _, f = wg_ref.shape
    tm = t.shape[0]
    per_group = ne // N_GROUPS
    gates_t, best_g = _moe_gates(lg.T[:ne], rb_ref[...])

    row8 = lax.broadcasted_iota(jnp.int32, (8, tm), 0)
    member = row8 == best_g
    before = _dot(member.astype(BF16), tri_ref[...])
    rank_row = jnp.sum(jnp.where(member, before, 0.0), axis=0, keepdims=True)
    grp_row = best_g.astype(F32)
    count = jnp.sum(member.astype(F32), axis=1, keepdims=True)
    g4_t = gates_t
    for g in range(1, N_GROUPS):
        g4_t = g4_t + pltpu.roll(gates_t, g * per_group, 0)
    row_e = lax.broadcasted_iota(jnp.int32, (ne, tm), 0)
    info_t = jnp.where(row_e < per_group, g4_t,
                       jnp.where(row_e == per_group, rank_row, jnp.where(row_e == per_group + 1, grp_row, 0.0)))
    info = jnp.concatenate([info_t, jnp.zeros((LANES - ne, tm), F32)], axis=0).T
    g4_s[...] = info.astype(BF16)
    rank_col = info[:, per_group:per_group + 1]
    grp_col = info[:, per_group + 1:per_group + 2]
    g2 = m[5:6]

    slot_col = lax.broadcasted_iota(jnp.int32, (cap, 1), 0).astype(F32)
    slot_row = lax.broadcasted_iota(jnp.int32, (1, cap), 1).astype(F32)
    for g in range(N_GROUPS):
        n_g = count[g, 0].astype(jnp.int32)

        def body(s, carry, g=g):
            base = (s * cap).astype(F32)
            pick = ((rank_row - base == slot_col) & (grp_row == float(g))).astype(BF16)
            xg = _dot(pick, t_s[...]).astype(BF16)
            gs = _dot(pick, g4_s[...])
            acts = []
            for j in range(per_group):
                e = g * per_group + j
                hgate = _dot(xg, wg_ref[e])
                hup = _dot(xg, wu_ref[e])
                acts.append((hgate * _sigmoid(hgate) * hup * gs[:, j:j + 1]).astype(BF16))
            y = _dot(jnp.concatenate(acts, axis=1), wd_ref[g * per_group * f:(g + 1) * per_group * f, :])
            put = ((rank_col - base == slot_row) & (grp_col == float(g))).astype(BF16)
            o_ref[0] += _dot(put, (y * g2).astype(BF16))
            return carry

        lax.fori_loop(0, (n_g + cap - 1) // cap, body, 0)

    if final:
        o = o_ref[0]
        o_ref[0] = o * lax.rsqrt(jnp.mean(o * o, axis=-1, keepdims=True) + NORM_EPS) * fw_ref[...]


def _mix_moe(x, hy, hg, mod_l, mod_row, layer, w_out, router_w, rbias, wg, wu, wd, *, tm, final_w=None):
    bsz, L, d = x.shape
    c = hy.shape[-1]
    cg = hg.shape[-1]
    _, ne, _, f = wg.shape
    rw_hi = router_w.astype(BF16)
    rw_lo = (router_w - rw_hi.astype(F32)).astype(BF16)
    rw = jnp.zeros((d, 2 * LANES), BF16).at[:, :ne].set(rw_hi).at[:, LANES:LANES + ne].set(rw_lo)
    cap = 5 * tm // 16
    assert cap % 8 == 0
    pos = jnp.arange(tm, dtype=jnp.int32)
    tri = (pos[:, None] < pos[None, :]).astype(BF16)
    final = final_w is not None
    extra_specs = [_resident((1, d))] if final else []
    extra_args = [final_w.reshape(1, d)] if final else []
    return pl.pallas_call(
        functools.partial(_mix_moe_kernel, cap=cap, final=final),
        out_shape=jax.ShapeDtypeStruct((bsz, L, d), F32),
        grid=(bsz, L // tm),
        in_specs=[
            pl.BlockSpec((1, tm, d), lambda b, i: (b, i, 0)),
            pl.BlockSpec((1, tm, c), lambda b, i: (b, i, 0)),
            pl.BlockSpec((1, tm, cg), lambda b, i: (b, i, 0)),
            pl.BlockSpec((1, N_MOD, d), lambda b, i: (mod_row(b), 0, 0)),
            _resident((c + cg, d), layer),
            _resident((d, 2 * LANES)),
            _resident((ne, 1)),
            _resident((tm, tm)),
            _resident((ne, d, f), layer),
            _resident((ne, d, f), layer),
            _resident((ne * f, d), layer),
        ] + extra_specs,
        out_specs=pl.BlockSpec((1, tm, d), lambda b, i: (b, i, 0)),
        scratch_shapes=[pltpu.VMEM((tm, d), BF16), pltpu.VMEM((tm, LANES), BF16)],
        compiler_params=_cparams("parallel", "parallel"),
        name="mix_moe",
    )(x, hy, hg, mod_l, w_out, rw, rbias.reshape(ne, 1), tri, wg, wu, wd.reshape(-1, ne * f, d), *extra_args)


def kernel(x, c, ctx, c_ctx, w_mod, b_mod, w_in, w_out, hy_conv_w, hy_conv_b, hy_w1, hy_b1, hy_w2, hy_b2, hy_w3, hy_b3, hy_bias, hgrn_lower_bounds, hgrn_norm_w, router_w, router_bias, moe_w_gate, moe_w_up, moe_w_down, final_norm_w):
    bsz, seq, d = x.shape
    ctx_len = ctx.shape[1]
    depth = w_mod.shape[0]
    dg = hgrn_lower_bounds.shape[-1]
    nh = dg // HEAD_DIM
    assert bsz + 1 <= MOD_ROWS and seq % GRID_W == 0

    cc = jnp.zeros((MOD_ROWS, d), F32).at[:bsz].set(c).at[bsz].set(c_ctx)
    mod = _modulation(cc, w_mod, b_mod).reshape(depth, MOD_ROWS, N_MOD, d)
    lat_row = lambda b: b
    ctx_row = lambda b: bsz

    lb_soft = jax.nn.softmax(hgrn_lower_bounds.astype(F32), axis=1)
    lower = jnp.cumsum(lb_soft, axis=1) - lb_soft[:, :1]
    lower = lower.reshape(2, depth, nh, HEAD_DIM).transpose(1, 2, 0, 3)

    dft_lat = _dft_matrices(seq // _hyena_blocks(seq))
    dft_ctx = _dft_matrices(ctx_len // _hyena_blocks(ctx_len))
    tm_lat = min(seq, 1024)
    tm_moe = min(seq, 512)

    w_in_b = w_in.astype(BF16)
    moe_w = (w_out.astype(BF16), router_w, router_bias,
             moe_w_gate.astype(BF16), moe_w_up.astype(BF16), moe_w_down.astype(BF16))

    xc = ctx
    for l in range(depth):
        last = l == depth - 1
        u_lat, hg_lat = _inproj(x, mod[l], lat_row, w_in_b, l, hy_conv_w[l], hy_conv_b[l],
                                period=GRID_W, tm=tm_lat)
        u_ctx, hg_ctx = _inproj(xc, mod[l], ctx_row, w_in_b, l, hy_conv_w[l], hy_conv_b[l],
                                period=ctx_len, tm=ctx_len)
        o_ctx, o_lat = _hgrn(hg_ctx, hg_lat, lower[l], hgrn_norm_w[l])
        taps_w = (hy_w1[l], hy_b1[l], hy_w2[l], hy_b2[l], hy_w3[l], hy_b3[l])
        hy_lat = _hyena(u_lat, dft_lat, taps_w, hy_bias[l])
        x = _mix_moe(x, hy_lat, o_lat, mod[l], lat_row, l, *moe_w, tm=tm_moe,
                     final_w=final_norm_w if last else None)
        if not last:
            hy_ctx = _hyena(u_ctx, dft_ctx, taps_w, hy_bias[l])
            xc = _mix_moe(xc, hy_ctx, o_ctx, mod[l], ctx_row, l, *moe_w, tm=ctx_len)

    return x
```

```python
import functools
import math

import numpy as np
import jax
import jax.numpy as jnp
from jax import lax
from jax.experimental import pallas as pl
from jax.experimental.pallas import tpu as pltpu

F32 = jnp.float32
BF16 = jnp.bfloat16
HIGHEST = lax.Precision.HIGHEST

GRID_W = 64
NORM_EPS = 1e-6
N_MOD = 6
HY_ORDER = 2
HY_BANDS = 16
HY_TARGET = 1e-2
HY_FAST_PCT = 0.3
HY_SLOW_PCT = 1.5
HEAD_DIM = 128
HY_BLOCKS = 2
N_EXPERTS = 16
N_GROUPS = 4
EXPERTS_PER_GROUP = N_EXPERTS // N_GROUPS
LANES = 128
MOD_ROWS = 16
HGRN_CHUNK = 128
HGRN_GROUP = 16
HGRN_FINE_LEVELS = (4,)
VMEM_LIMIT = 56 << 20


def _cparams(*sem):
    return pltpu.CompilerParams(dimension_semantics=sem, vmem_limit_bytes=VMEM_LIMIT)


def _sigmoid(x):
    return 1.0 / (1.0 + jnp.exp2(x * (-1.0 / math.log(2.0))))


def _dot(a, b, **kw):
    return jnp.dot(a, b, preferred_element_type=F32, **kw)


def _resident(shape, layer=None):
    if layer is None:
        return pl.BlockSpec(shape, lambda *_: (0,) * len(shape), pipeline_mode=pl.Buffered(1))
    return pl.BlockSpec((None,) + tuple(shape), lambda *_: (layer,) + (0,) * len(shape),
                        pipeline_mode=pl.Buffered(1))


def _dot_nt(a, b):
    return lax.dot_general(a, b, (((1,), (1,)), ((), ())), preferred_element_type=F32)


def _mod_kernel(c_ref, w_ref, b_ref, o_ref):
    c = c_ref[...]
    cs = c * _sigmoid(c)
    o_ref[0] = _dot(cs, w_ref[0], precision=HIGHEST) + b_ref[0]


def _modulation(cc, w_mod, b_mod):
    depth, d, n = w_mod.shape
    tn = n // 3
    return pl.pallas_call(
        _mod_kernel,
        out_shape=jax.ShapeDtypeStruct((depth, MOD_ROWS, n), F32),
        grid=(depth, n // tn),
        in_specs=[
            pl.BlockSpec((MOD_ROWS, d), lambda l, j: (0, 0)),
            pl.BlockSpec((1, d, tn), lambda l, j: (l, 0, j)),
            pl.BlockSpec((1, 1, tn), lambda l, j: (l, 0, j)),
        ],
        out_specs=pl.BlockSpec((1, MOD_ROWS, tn), lambda l, j: (l, 0, j)),
        compiler_params=_cparams("parallel", "parallel"),
        name="modulation",
    )(cc, w_mod, b_mod.reshape(depth, 1, n))


def _dft_kernel(ac_ref, as_ref, bc_ref, bs_ref, c_ref, s_ref, *, n1):
    bc = bc_ref[...]
    bs = bs_ref[...]
    ac = ac_ref[...]
    as_ = as_ref[...]
    for j in range(n1):
        a_c = ac[:, j:j + 1]
        a_s = as_[:, j:j + 1]
        c_ref[:, j * LANES:(j + 1) * LANES] = (a_c * bc - a_s * bs).astype(BF16)
        s_ref[:, j * LANES:(j + 1) * LANES] = (a_s * bc + a_c * bs).astype(BF16)


def _dft_tables(L):
    n1 = L // LANES
    period = 4 * L
    r = np.arange(L, dtype=np.int64)[:, None]
    c1 = np.arange(n1, dtype=np.int64)[None, :]
    c0 = np.arange(LANES, dtype=np.int64)[None, :]

    def cs(phase):
        ang = (phase % period).astype(np.float64) * (2.0 * np.pi / period)
        return np.cos(ang).astype(np.float32), np.sin(ang).astype(np.float32)

    fa = cs((2 * r + 1) * (LANES * c1))
    fb = cs((2 * r + 1) * c0)
    ta = cs((2 * LANES * c1) * r)
    tb = cs((2 * c0 + 1) * r)
    return (fa, fb), (ta, tb)


def _dft_matrices(L):
    n1 = L // LANES
    tr = min(L, 256)
    outs = []
    for (ac, as_), (bc, bs) in _dft_tables(L):
        c, s = pl.pallas_call(
            functools.partial(_dft_kernel, n1=n1),
            out_shape=(jax.ShapeDtypeStruct((L, L), BF16), jax.ShapeDtypeStruct((L, L), BF16)),
            grid=(L // tr,),
            in_specs=[
                pl.BlockSpec((tr, n1), lambda i: (i, 0)),
                pl.BlockSpec((tr, n1), lambda i: (i, 0)),
                pl.BlockSpec((tr, LANES), lambda i: (i, 0)),
                pl.BlockSpec((tr, LANES), lambda i: (i, 0)),
            ],
            out_specs=(pl.BlockSpec((tr, L), lambda i: (i, 0)), pl.BlockSpec((tr, L), lambda i: (i, 0))),
            compiler_params=_cparams("parallel"),
            name="dft_tables",
        )(jnp.asarray(ac), jnp.asarray(as_), jnp.asarray(bc), jnp.asarray(bs))
        outs.append((c, s))
    return outs


def _inproj_kernel(x_ref, mod_ref, w_ref, cw_ref, cb_ref, u_ref, hg_ref, *, period, n_hy, tn):
    x = x_ref[0]
    tm = x.shape[0]
    xn = x * lax.rsqrt(jnp.mean(x * x, axis=-1, keepdims=True) + NORM_EPS)
    m = mod_ref[0]
    a = (xn * (1.0 + m[1:2]) + m[0:1]).astype(BF16)
    pos = lax.broadcasted_iota(jnp.int32, (tm, 1), 0) % period
    first = pos == 0
    last = pos == period - 1
    n_total = w_ref.shape[1]
    for j in range(n_total // tn):
        p = _dot(a, w_ref[:, j * tn:(j + 1) * tn])
        if j * tn < n_hy:
            cw = cw_ref[:, j * tn:(j + 1) * tn]
            prev = jnp.where(first, 0.0, pltpu.roll(p, 1, 0))
            nxt = jnp.where(last, 0.0, pltpu.roll(p, tm - 1, 0))
            u_ref[0, :, j * tn:(j + 1) * tn] = (
                prev * cw[0:1] + p * cw[1:2] + nxt * cw[2:3] + cb_ref[:, j * tn:(j + 1) * tn]
            ).astype(u_ref.dtype)
        else:
            hg_ref[0, :, j * tn - n_hy:(j + 1) * tn - n_hy] = p.astype(hg_ref.dtype)


def _inproj(x, mod_l, mod_row, w_in, layer, conv_w, conv_b, *, period, tm):
    bsz, L, d = x.shape
    n_hy = conv_w.shape[1]
    n_all = w_in.shape[-1]
    tn = n_hy // 3
    assert tm % period == 0 or period == L == tm
    return pl.pallas_call(
        functools.partial(_inproj_kernel, period=period, n_hy=n_hy, tn=tn),
        out_shape=(jax.ShapeDtypeStruct((bsz, L, n_hy), BF16),
                   jax.ShapeDtypeStruct((bsz, L, n_all - n_hy), BF16)),
        grid=(bsz, L // tm),
        in_specs=[
            pl.BlockSpec((1, tm, d), lambda b, i: (b, i, 0)),
            pl.BlockSpec((1, N_MOD, d), lambda b, i: (mod_row(b), 0, 0)),
            _resident((d, n_all), layer),
            _resident((3, n_hy)),
            _resident((1, n_hy)),
        ],
        out_specs=(pl.BlockSpec((1, tm, n_hy), lambda b, i: (b, i, 0)),
                   pl.BlockSpec((1, tm, n_all - n_hy), lambda b, i: (b, i, 0))),
        compiler_params=_cparams("parallel", "parallel"),
        name="inproj",
    )(x, mod_l, w_in, conv_w, conv_b.reshape(1, n_hy))


def _rev_rows(x):
    nblk = x.shape[0] // LANES
    r = lax.broadcasted_iota(jnp.int32, (LANES, LANES), 0)
    c_ = lax.broadcasted_iota(jnp.int32, (LANES, LANES), 1)
    exch = (r + c_ == LANES - 1).astype(BF16)
    xb = x.astype(BF16)
    return jnp.concatenate(
        [_dot(exch, xb[(nblk - 1 - i) * LANES:(nblk - i) * LANES]) for i in range(nblk)], axis=0)


def _filter_kernel(z_ref, w1_ref, b1_ref, w2_ref, b2_ref, w3f_ref, b3f_ref, w3b_ref, b3b_ref,
                   dl_ref, kap_ref, rev_ref, h_s):
    z = z_ref[...]
    L = z.shape[0]
    hid = w2_ref.shape[0]

    @pl.when((pl.program_id(0) == 0) & (pl.program_id(1) == 0))
    def _():
        h1 = jnp.sin(_dot(z, w1_ref[...], precision=HIGHEST) + b1_ref[...])
        h2 = jnp.sin(_dot(h1, w2_ref[...], precision=HIGHEST) + b2_ref[...])
        h_hi = h2.astype(BF16)
        h_s[...] = jnp.concatenate([h_hi, (h2 - h_hi.astype(F32)).astype(BF16)], axis=1)

    def last_layer(w_ref, b_ref):
        return (_dot(h_s[...], w_ref[:2 * hid, :]) + _dot(h_s[:, :hid], w_ref[2 * hid:, :])) + b_ref[...]

    hf = last_layer(w3f_ref, b3f_ref)
    hb = last_layer(w3b_ref, b3b_ref)
    win = jnp.exp(-z[:, 0:1] * dl_ref[...])
    hf = hf * win
    hb = hb * win
    nrm = (jnp.sum(jnp.abs(hf), axis=0, keepdims=True)
           + jnp.sum(jnp.abs(hb), axis=0, keepdims=True))
    inv = 1.0 / nrm
    hf = hf * inv
    hb = hb * inv
    first = lax.broadcasted_iota(jnp.int32, (L, 1), 0) == 0
    down1 = lambda y: jnp.where(first, 0.0, pltpu.roll(y, 1, 0))
    kap_ref[0, 0:L, :] = down1(_rev_rows(hb)).astype(BF16)
    kap_ref[0, L:2 * L, :] = hf.astype(BF16)
    rev_ref[0, 0:L, :] = down1(_rev_rows(hf)).astype(BF16)
    rev_ref[0, L:2 * L, :] = jnp.where(first, hf[0:1], hb).astype(BF16)


def _hyena_filter_taps(L, w1, b1, w2, b2, w3, b3):
    nfeat, hid = w1.shape
    c = w3.shape[1] // (2 * HY_ORDER)
    tc = min(c, 256)
    nct = c // tc
    t = jnp.linspace(0.0, 1.0, L, dtype=F32)
    n = jnp.arange(L, dtype=F32)
    freqs = jnp.linspace(1e-4, HY_BANDS - 1, HY_BANDS, dtype=F32)
    ang = (2.0 * math.pi / L) * n[:, None] * freqs[None, :]
    z = jnp.concatenate([t[:, None], jnp.cos(ang), -jnp.sin(ang)], axis=-1)
    z = jnp.pad(z, ((0, 0), (0, LANES - nfeat)))
    w1p = jnp.pad(w1, ((0, LANES - nfeat), (0, 0)))
    deltas = jnp.abs(jnp.linspace(math.log(HY_TARGET) / HY_FAST_PCT, math.log(HY_TARGET) / HY_SLOW_PCT,
                                  c, dtype=F32)).reshape(1, c)
    w3_hi = w3.astype(BF16)
    w3p = jnp.concatenate([w3_hi, w3_hi, (w3 - w3_hi.astype(F32)).astype(BF16)], axis=0)
    full = lambda shape: pl.BlockSpec(shape, lambda o, j: (0,) * len(shape))
    out = jax.ShapeDtypeStruct((HY_ORDER, 2 * L, c), BF16)
    return pl.pallas_call(
        _filter_kernel,
        out_shape=(out, out),
        grid=(HY_ORDER, nct),
        in_specs=[
            full((L, LANES)), full((LANES, hid)), full((1, hid)), full((hid, hid)), full((1, hid)),
            pl.BlockSpec((3 * hid, tc), lambda o, j: (0, o * 2 * nct + j)),
            pl.BlockSpec((1, tc), lambda o, j: (0, o * 2 * nct + j)),
            pl.BlockSpec((3 * hid, tc), lambda o, j: (0, o * 2 * nct + nct + j)),
            pl.BlockSpec((1, tc), lambda o, j: (0, o * 2 * nct + nct + j)),
            pl.BlockSpec((1, tc), lambda o, j: (0, j)),
        ],
        out_specs=(pl.BlockSpec((1, 2 * L, tc), lambda o, j: (o, 0, j)),
                   pl.BlockSpec((1, 2 * L, tc), lambda o, j: (o, 0, j))),
        scratch_shapes=[pltpu.VMEM((L, 2 * hid), BF16)],
        compiler_params=_cparams("arbitrary", "arbitrary"),
        name="hyena_filter",
    )(z, w1p, b1.reshape(1, hid), w2, b2.reshape(1, hid), w3p, b3.reshape(1, -1), w3p, b3.reshape(1, -1),
      deltas)


def _kspec_kernel(cf_ref, sf_ref, pos_ref, neg_ref, o_ref, *, scale):
    pos = pos_ref[0].astype(F32)
    neg = neg_ref[0].astype(F32)
    first = lax.broadcasted_iota(jnp.int32, (pos.shape[0], 1), 0) == 0
    neg = jnp.where(first, 0.0, neg)
    o_ref[0, 0, 0] = _dot(cf_ref[...], (pos + neg).astype(BF16)) * scale
    o_ref[0, 0, 1] = _dot(sf_ref[...], (pos - neg).astype(BF16)) * scale


def _filter_spectrum(cft, sft, kap, rev, nb):
    _, two_l, c = kap.shape
    bk = two_l // (2 * nb)
    n_off = 2 * nb - 1
    return pl.pallas_call(
        functools.partial(_kspec_kernel, scale=1.0 / bk),
        out_shape=jax.ShapeDtypeStruct((HY_ORDER, n_off, 2, bk, c), F32),
        grid=(HY_ORDER, n_off),
        in_specs=[
            _resident((bk, bk)),
            _resident((bk, bk)),
            pl.BlockSpec((1, bk, c), lambda o, k: (o, k + 1, 0)),
            pl.BlockSpec((1, bk, c), lambda o, k: (o, 2 * nb - 1 - k, 0)),
        ],
        out_specs=pl.BlockSpec((1, 1, 2, bk, c), lambda o, k: (o, k, 0, 0, 0)),
        compiler_params=_cparams("parallel", "parallel"),
        name="hyena_filter_spectrum",
    )(cft, sft, kap, rev)


def _hy_fwd_kernel(cf_ref, sf_ref, v_ref, a_ref, b_ref, *, nb, bk):
    for j in range(nb):
        rows = slice(j * bk, (j + 1) * bk)
        v = v_ref[0, rows, :].astype(BF16)
        a_ref[0, rows, :] = _dot(cf_ref[...], v).astype(BF16)
        b_ref[0, rows, :] = _dot(sf_ref[...], v).astype(BF16)


def _hy_forward(cft, sft, src, src_col, c, nb):
    bsz, L, _ = src.shape
    bk = L // nb
    out = jax.ShapeDtypeStruct((bsz, L, c), BF16)
    return pl.pallas_call(
        functools.partial(_hy_fwd_kernel, nb=nb, bk=bk),
        out_shape=(out, out),
        grid=(bsz,),
        in_specs=[
            _resident((bk, bk)),
            _resident((bk, bk)),
            pl.BlockSpec((1, L, c), lambda b: (b, 0, src_col)),
        ],
        out_specs=(pl.BlockSpec((1, L, c), lambda b: (b, 0, 0)), pl.BlockSpec((1, L, c), lambda b: (b, 0, 0))),
        compiler_params=_cparams("parallel"),
        name="hyena_spectrum",
    )(cft, sft, src)


def _hy_inv_kernel(cf_ref, sf_ref, a_ref, b_ref, k_ref, gate_ref, src_ref, d_ref, o_ref, *, nb, bk):
    for i in range(nb):
        p = q = None
        for j in range(nb):
            rows = slice(j * bk, (j + 1) * bk)
            a = a_ref[0, rows, :].astype(F32)
            b = b_ref[0, rows, :].astype(F32)
            kr = k_ref[i - j + nb - 1, 0]
            ks = k_ref[i - j + nb - 1, 1]
            pj = a * kr - b * ks
            qj = a * ks + b * kr
            p = pj if p is None else p + pj
            q = qj if q is None else q + qj
        conv = _dot(cf_ref[...], p.astype(BF16)) + _dot(sf_ref[...], q.astype(BF16))
        rows = slice(i * bk, (i + 1) * bk)
        o_ref[0, rows, :] = (gate_ref[0, rows, :] * (conv + src_ref[0, rows, :] * d_ref[0])).astype(o_ref.dtype)


def _hy_inverse(cf, sf, a, b, kspec, gate, gate_col, src, src_col, d, order):
    bsz, L, c = a.shape
    _, n_off, _, bk, _ = kspec.shape
    nb = (n_off + 1) // 2
    return pl.pallas_call(
        functools.partial(_hy_inv_kernel, nb=nb, bk=bk),
        out_shape=jax.ShapeDtypeStruct((bsz, L, c), BF16),
        grid=(bsz,),
        in_specs=[
            _resident((bk, bk)),
            _resident((bk, bk)),
            pl.BlockSpec((1, L, c), lambda b: (b, 0, 0)),
            pl.BlockSpec((1, L, c), lambda b: (b, 0, 0)),
            _resident((n_off, 2, bk, c), order),
            pl.BlockSpec((1, L, c), lambda b: (b, 0, gate_col)),
            pl.BlockSpec((1, L, c), lambda b: (b, 0, src_col)),
            pl.BlockSpec((1, 1, c), lambda b: (order, 0, 0)),
        ],
        out_specs=pl.BlockSpec((1, L, c), lambda b: (b, 0, 0)),
        compiler_params=_cparams("parallel"),
        name="hyena_inverse",
    )(cf, sf, a, b, kspec, gate, src, d.reshape(HY_ORDER, 1, c))


def _hyena_blocks(L):
    bk = max(L // HY_BLOCKS, LANES)
    return L // bk


def _hyena(u, dft, taps_w, d):
    (cft, sft), (cf, sf) = dft
    L = u.shape[1]
    c = u.shape[2] // 3
    nb = _hyena_blocks(L)
    kap, rev = _hyena_filter_taps(L, *taps_w)
    kspec = _filter_spectrum(cft, sft, kap, rev, nb)
    a, b = _hy_forward(cft, sft, u, 0, c, nb)
    z = _hy_inverse(cf, sf, a, b, kspec, u, 1, u, 0, d, 0)
    a, b = _hy_forward(cft, sft, z, 0, c, nb)
    return _hy_inverse(cf, sf, a, b, kspec, u, 2, z, 0, d, 1)


def _anchor_rows(b, n, a):
    assert n % 8 == 0
    parts = [jnp.broadcast_to(b[s + a:s + a + 1, :], (n, b.shape[1])) for s in range(0, b.shape[0], n)]
    return parts[0] if len(parts) == 1 else jnp.concatenate(parts, axis=0)


def _neg_abs_anchor_diff(b, n, a, reverse):
    h = n // 2
    m = _anchor_rows(b, n, a)
    if h % 8:
        bits = lax.bitcast_convert_type(b - m, jnp.uint32) | jnp.uint32(0x80000000)
        return lax.bitcast_convert_type(bits, F32)
    pieces = []
    for s in range(0, b.shape[0], h):
        first_half = (s // h) % 2 == 0
        upstream = first_half != reverse
        x, y = (m, b) if upstream else (b, m)
        pieces.append(x[s:s + h] - y[s:s + h])
    return jnp.concatenate(pieces, axis=0)


def _hgrn_weights(q, f_logit, lb, lv, tri, scores, *, reverse):
    C = HGRN_CHUNK
    G = q.shape[0] // C
    rows = lambda x, i: x[i * C:(i + 1) * C]
    f = lb + (1.0 - lb) * _sigmoid(f_logit)
    kk = 1.0 - f
    g = jnp.log2(f)
    g_top = lax.bitcast_convert_type(
        lax.bitcast_convert_type(g, jnp.uint32) & jnp.uint32(0xFFFF0000), F32)
    gg = jnp.concatenate([g_top.astype(BF16), (g - g_top).astype(BF16)], axis=1)
    bb = [_dot(tri, rows(gg, i)) for i in range(G)]
    part = lambda k: jnp.concatenate(
        [x[k * C:(k + 1) * C, :HEAD_DIM] + x[k * C:(k + 1) * C, HEAD_DIM:] for x in bb], axis=0)
    b = part(0)
    fine = {n_: part(k + 1) for k, n_ in enumerate(HGRN_FINE_LEVELS)}
    odd = lax.broadcasted_iota(jnp.int32, (g.shape[0], 1), 0) % 2 == 1
    fine[2] = jnp.where(odd != reverse, g, 0.0)

    qb = q.astype(BF16)
    kb = kk.astype(BF16)
    n = 2
    level = 1
    while n <= C:
        a = n // 2 if reverse else n // 2 - 1
        d = fine[n] if n in fine else _neg_abs_anchor_diff(b, n, a, reverse)
        e = jnp.exp2(d).astype(BF16)
        qe = qb * e
        ke = kb * e
        scores = [jnp.where(lv == level, _dot_nt(rows(qe, i), rows(ke, i)), scores[i]) for i in range(G)]
        n *= 2
        level += 1

    b_end = _anchor_rows(b, C, 0 if reverse else C - 1)
    qd = qb * jnp.exp2(b).astype(BF16)
    kd = kb * jnp.exp2(b_end - b).astype(BF16)
    return scores, kk, qd, kd, jnp.exp2(b_end)


def _hgrn_kernel(qc, ic, gc, fc, bc, ql, il, gl, fl, bl, lb_ref, nw_ref, oc_ref, ol_ref, stf, stb):
    C = HGRN_CHUNK
    row = lax.broadcasted_iota(jnp.int32, (C, C), 0)
    col = lax.broadcasted_iota(jnp.int32, (C, C), 1)
    x = row ^ col
    lvl = jnp.zeros((C, C), jnp.int32)
    n = 1
    while n < C:
        lvl = lvl + (x >= n).astype(jnp.int32)
        n *= 2
    lv_f = jnp.where(row > col, lvl, 0)
    lv_b = jnp.where(row < col, lvl, 0)

    def cum_matrix(reverse):
        cum = lambda r: (col >= r) if reverse else (col <= r)
        mats = [cum(row).astype(F32)]
        for n_ in HGRN_FINE_LEVELS:
            anchor = row - row % n_ + (n_ // 2 if reverse else n_ // 2 - 1)
            upstream = (row % n_ < n_ // 2) != reverse
            diff = mats[0] - cum(anchor).astype(F32)
            mats.append(jnp.where(upstream, -diff, diff))
        return jnp.concatenate(mats, axis=0).astype(BF16)

    tri_f = cum_matrix(False)
    tri_b = cum_matrix(True)
    lb_f = lb_ref[0, 0:1, :]
    lb_b = lb_ref[0, 1:2, :]
    nw = nw_ref[...]
    stf[...] = jnp.zeros_like(stf)
    stb[...] = jnp.zeros_like(stb)

    def run(q_ref, i_ref, g_ref, f_ref, b_ref, o_ref):
        ld = lambda ref: ref[0].astype(F32)
        q_raw = ld(q_ref)
        q = q_raw * _sigmoid(q_raw)
        v = ld(i_ref)
        G = q.shape[0] // C
        rows = lambda x, i: x[i * C:(i + 1) * C]
        scores = [jnp.zeros((C, C), F32)] * G
        scores, k_f, qd_f, kd_f, dec_f = _hgrn_weights(q, ld(f_ref), lb_f, lv_f, tri_f, scores, reverse=False)
        scores, k_b, qd_b, kd_b, dec_b = _hgrn_weights(q, ld(b_ref), lb_b, lv_b, tri_b, scores, reverse=True)
        kd = jnp.concatenate([kd_f, kd_b], axis=1)
        qd = jnp.concatenate([qd_f, qd_b], axis=1)
        kv = [_dot(rows(v, i).T.astype(BF16), rows(kd, i)) for i in range(G)]
        before = [None] * G
        st = stf[...]
        for i in range(G):
            before[i] = st
            st = st * dec_f[i * C:i * C + 1] + kv[i][:, :HEAD_DIM]
        stf[...] = st
        st = stb[...]
        for i in range(G - 1, -1, -1):
            before[i] = jnp.concatenate([before[i], st], axis=1).astype(BF16)
            st = st * dec_b[i * C:i * C + 1] + kv[i][:, HEAD_DIM:]
        stb[...] = st
        vb = v.astype(BF16)
        diag = jnp.sum(q * (k_f + k_b), axis=-1, keepdims=True) * v
        o = jnp.concatenate(
            [_dot(scores[i].astype(BF16), rows(vb, i)) + _dot_nt(rows(qd, i), before[i]) for i in range(G)],
            axis=0) + diag
        o = o * lax.rsqrt(jnp.mean(o * o, axis=-1, keepdims=True) + NORM_EPS) * nw
        g = ld(g_ref)
        o_ref[0] = (o * (g * _sigmoid(g))).astype(o_ref.dtype)

    run(qc, ic, gc, fc, bc, oc_ref)
    run(ql, il, gl, fl, bl, ol_ref)


def _hgrn(hg_ctx, hg_lat, lb, norm_w):
    bsz, lc, n5 = hg_ctx.shape
    ll = hg_lat.shape[1]
    dg = n5 // 5
    nh = dg // HEAD_DIM
    C = HGRN_CHUNK
    assert lc % C == 0 and ll % C == 0 and max(lc, ll) <= HGRN_GROUP * C

    def slab(L, part):
        return pl.BlockSpec((1, L, HEAD_DIM), lambda b, h: (b, 0, part * nh + h))

    return pl.pallas_call(
        _hgrn_kernel,
        out_shape=(jax.ShapeDtypeStruct((bsz, lc, dg), BF16), jax.ShapeDtypeStruct((bsz, ll, dg), BF16)),
        grid=(bsz, nh),
        in_specs=[slab(lc, p) for p in range(5)] + [slab(ll, p) for p in range(5)] + [
            pl.BlockSpec((1, 2, HEAD_DIM), lambda b, h: (h, 0, 0)),
            pl.BlockSpec((1, HEAD_DIM), lambda b, h: (0, 0)),
        ],
        out_specs=(pl.BlockSpec((1, lc, HEAD_DIM), lambda b, h: (b, 0, h)),
                   pl.BlockSpec((1, ll, HEAD_DIM), lambda b, h: (b, 0, h))),
        scratch_shapes=[pltpu.VMEM((HEAD_DIM, HEAD_DIM), F32), pltpu.VMEM((HEAD_DIM, HEAD_DIM), F32)],
        compiler_params=_cparams("parallel", "parallel"),
        name="hgrn2",
    )(*([hg_ctx] * 5), *([hg_lat] * 5), lb, norm_w.reshape(1, HEAD_DIM))


def _first_argmax(vals, idx, sentinel):
    m = jnp.max(vals, axis=0, keepdims=True)
    first = jnp.min(jnp.where(vals == m, idx, sentinel), axis=0, keepdims=True)
    return m, first


def _moe_gates(logits_t, rbias):
    ne, tm = logits_t.shape
    neg = -jnp.inf
    mx = jnp.max(logits_t, axis=0, keepdims=True)
    ex = jnp.exp(logits_t - mx)
    scores = ex / jnp.sum(ex, axis=0, keepdims=True)
    sel = scores + rbias
    eidx = lax.broadcasted_iota(jnp.int32, (ne, tm), 0)
    grp = eidx // EXPERTS_PER_GROUP
    best_s = jnp.full((1, tm), neg, F32)
    best_g = jnp.zeros((1, tm), jnp.int32)
    for gi in range(N_GROUPS):
        mg = jnp.where(grp == gi, sel, neg)
        m1, i1 = _first_argmax(mg, eidx, ne)
        m2 = jnp.max(jnp.where(eidx == i1, neg, mg), axis=0, keepdims=True)
        gs = m1 + m2
        upd = gs > best_s
        best_g = jnp.where(upd, gi, best_g)
        best_s = jnp.where(upd, gs, best_s)
    sg = jnp.where(grp == best_g, sel, neg)
    _, i1 = _first_argmax(sg, eidx, ne)
    _, i2 = _first_argmax(jnp.where(eidx == i1, neg, sg), eidx, ne)
    chosen = (eidx == i1) | (eidx == i2)
    w = jnp.where(chosen, scores, 0.0)
    return w / jnp.sum(w, axis=0, keepdims=True), best_g


def _mix_moe_kernel(x_ref, hy_ref, hg_ref, mod_ref, w_ref, rw_ref, rb_ref, tri_ref, wg_ref, wu_ref, wd_ref,
                    *rest, cap, final):
    fw_ref, o_ref, t_s, g4_s = rest if final else (None,) + rest
    c = hy_ref.shape[-1]
    mix = _dot(hy_ref[0], w_ref[:c, :]) + _dot(hg_ref[0], w_ref[c:, :])
    m = mod_ref[0]
    x1 = x_ref[0] + m[2:3] * mix
    o_ref[0] = x1
    xn = x1 * lax.rsqrt(jnp.mean(x1 * x1, axis=-1, keepdims=True) + NORM_EPS)
    t = xn * (1.0 + m[4:5]) + m[3:4]
    t_hi = t.astype(BF16)
    t_s[...] = t_hi
    t_lo = (t - t_hi.astype(F32)).astype(BF16)
    r = _dot(t_hi, rw_ref[...])
    lg = r[:, :LANES] + r[:, LANES:] + _dot(t_lo, rw_ref[:, :LANES])
    ne, _, f = wg_ref.shape
    tm = t.shape[0]
    per_group = ne // N_GROUPS
    gates_t, best_g = _moe_gates(lg.T[:ne], rb_ref[...])

    row8 = lax.broadcasted_iota(jnp.int32, (8, tm), 0)
    member = row8 == best_g
    before = _dot(member.astype(BF16), tri_ref[...])
    rank_row = jnp.sum(jnp.where(member, before, 0.0), axis=0, keepdims=True)
    grp_row = best_g.astype(F32)
    count = jnp.sum(member.astype(F32), axis=1, keepdims=True)
    g4_t = gates_t
    for g in range(1, N_GROUPS):
        g4_t = g4_t + pltpu.roll(gates_t, g * per_group, 0)
    row_e = lax.broadcasted_iota(jnp.int32, (ne, tm), 0)
    info_t = jnp.where(row_e < per_group, g4_t,
                       jnp.where(row_e == per_group, rank_row, jnp.where(row_e == per_group + 1, grp_row, 0.0)))
    info = jnp.concatenate([info_t, jnp.zeros((LANES - ne, tm), F32)], axis=0).T
    g4_s[...] = info.astype(BF16)
    rank_col = info[:, per_group:per_group + 1]
    grp_col = info[:, per_group + 1:per_group + 2]
    g2 = m[5:6]

    slot_col = lax.broadcasted_iota(jnp.int32, (cap, 1), 0).astype(F32)
    slot_row = lax.broadcasted_iota(jnp.int32, (1, cap), 1).astype(F32)
    for g in range(N_GROUPS):
        n_g = count[g, 0].astype(jnp.int32)

        def body(s, carry, g=g):
            base = (s * cap).astype(F32)
            pick = ((rank_row - base == slot_col) & (grp_row == float(g))).astype(BF16)
            xg = _dot(pick, t_s[...]).astype(BF16)
            gs = _dot(pick, g4_s[...])
            acts = []
            for j in range(per_group):
                e = g * per_group + j
                hgate = _dot(xg, wg_ref[e])
                hup = _dot(xg, wu_ref[e])
                acts.append((hgate * _sigmoid(hgate) * hup * gs[:, j:j + 1]).astype(BF16))
            y = _dot(jnp.concatenate(acts, axis=1), wd_ref[g * per_group * f:(g + 1) * per_group * f, :])
            put = ((rank_col - base == slot_row) & (grp_col == float(g))).astype(BF16)
            o_ref[0] += _dot(put, (y * g2).astype(BF16))
            return carry

        lax.fori_loop(0, (n_g + cap - 1) // cap, body, 0)

    if final:
        o = o_ref[0]
        o_ref[0] = o * lax.rsqrt(jnp.mean(o * o, axis=-1, keepdims=True) + NORM_EPS) * fw_ref[...]


def _mix_moe(x, hy, hg, mod_l, mod_row, layer, w_out, router_w, rbias, wg, wu, wd, *, tm, final_w=None):
    bsz, L, d = x.shape
    c = hy.shape[-1]
    cg = hg.shape[-1]
    _, ne, _, f = wg.shape
    rw_hi = router_w.astype(BF16)
    rw_lo = (router_w - rw_hi.astype(F32)).astype(BF16)
    rw = jnp.zeros((d, 2 * LANES), BF16).at[:, :ne].set(rw_hi).at[:, LANES:LANES + ne].set(rw_lo)
    cap = 5 * tm // 16
    assert cap % 8 == 0
    pos = jnp.arange(tm, dtype=jnp.int32)
    tri = (pos[:, None] < pos[None, :]).astype(BF16)
    final = final_w is not None
    extra_specs = [_resident((1, d))] if final else []
    extra_args = [final_w.reshape(1, d)] if final else []
    return pl.pallas_call(
        functools.partial(_mix_moe_kernel, cap=cap, final=final),
        out_shape=jax.ShapeDtypeStruct((bsz, L, d), F32),
        grid=(bsz, L // tm),
        in_specs=[
            pl.BlockSpec((1, tm, d), lambda b, i: (b, i, 0)),
            pl.BlockSpec((1, tm, c), lambda b, i: (b, i, 0)),
            pl.BlockSpec((1, tm, cg), lambda b, i: (b, i, 0)),
            pl.BlockSpec((1, N_MOD, d), lambda b, i: (mod_row(b), 0, 0)),
            _resident((c + cg, d), layer),
            _resident((d, 2 * LANES)),
            _resident((ne, 1)),
            _resident((tm, tm)),
            _resident((ne, d, f), layer),
            _resident((ne, d, f), layer),
            _resident((ne * f, d), layer),
        ] + extra_specs,
        out_specs=pl.BlockSpec((1, tm, d), lambda b, i: (b, i, 0)),
        scratch_shapes=[pltpu.VMEM((tm, d), BF16), pltpu.VMEM((tm, LANES), BF16)],
        compiler_params=_cparams("parallel", "parallel"),
        name="mix_moe",
    )(x, hy, hg, mod_l, w_out, rw, rbias.reshape(ne, 1), tri, wg, wu, wd.reshape(-1, ne * f, d), *extra_args)


def kernel(x, c, ctx, c_ctx, w_mod, b_mod, w_in, w_out, hy_conv_w, hy_conv_b, hy_w1, hy_b1, hy_w2, hy_b2, hy_w3, hy_b3, hy_bias, hgrn_lower_bounds, hgrn_norm_w, router_w, router_bias, moe_w_gate, moe_w_up, moe_w_down, final_norm_w):
    bsz, seq, d = x.shape
    ctx_len = ctx.shape[1]
    depth = w_mod.shape[0]
    dg = hgrn_lower_bounds.shape[-1]
    nh = dg // HEAD_DIM
    assert bsz + 1 <= MOD_ROWS and seq % GRID_W == 0

    cc = jnp.zeros((MOD_ROWS, d), F32).at[:bsz].set(c).at[bsz].set(c_ctx)
    mod = _modulation(cc, w_mod, b_mod).reshape(depth, MOD_ROWS, N_MOD, d)
    lat_row = lambda b: b
    ctx_row = lambda b: bsz

    lb_soft = jax.nn.softmax(hgrn_lower_bounds.astype(F32), axis=1)
    lower = jnp.cumsum(lb_soft, axis=1) - lb_soft[:, :1]
    lower = lower.reshape(2, depth, nh, HEAD_DIM).transpose(1, 2, 0, 3)

    dft_lat = _dft_matrices(seq // _hyena_blocks(seq))
    dft_ctx = _dft_matrices(ctx_len // _hyena_blocks(ctx_len))
    tm_lat = min(seq, 1024)
    tm_moe = min(seq, 512)

    w_in_b = w_in.astype(BF16)
    moe_w = (w_out.astype(BF16), router_w, router_bias,
             moe_w_gate.astype(BF16), moe_w_up.astype(BF16), moe_w_down.astype(BF16))

    xc = ctx
    for l in range(depth):
        last = l == depth - 1
        u_lat, hg_lat = _inproj(x, mod[l], lat_row, w_in_b, l, hy_conv_w[l], hy_conv_b[l],
                                period=GRID_W, tm=tm_lat)
        u_ctx, hg_ctx = _inproj(xc, mod[l], ctx_row, w_in_b, l, hy_conv_w[l], hy_conv_b[l],
                                period=ctx_len, tm=ctx_len)
        o_ctx, o_lat = _hgrn(hg_ctx, hg_lat, lower[l], hgrn_norm_w[l])
        taps_w = (hy_w1[l], hy_b1[l], hy_w2[l], hy_b2[l], hy_w3[l], hy_b3[l])
        hy_lat = _hyena(u_lat, dft_lat, taps_w, hy_bias[l])
        x = _mix_moe(x, hy_lat, o_lat, mod[l], lat_row, l, *moe_w, tm=tm_moe,
                     final_w=final_norm_w if last else None)
        if not last:
            hy_ctx = _hyena(u_ctx, dft_ctx, taps_w, hy_bias[l])
            xc = _mix_moe(xc, hy_ctx, o_ctx, mod[l], ctx_row, l, *moe_w, tm=ctx_len)

    return x
```

```python
import functools
import math

import numpy as np
import jax
import jax.numpy as jnp
from jax import lax
from jax.experimental import pallas as pl
from jax.experimental.pallas import tpu as pltpu

F32 = jnp.float32
BF16 = jnp.bfloat16
HIGHEST = lax.Precision.HIGHEST

GRID_W = 64
NORM_EPS = 1e-6
N_MOD = 6
HY_ORDER = 2
HY_BANDS = 16
HY_TARGET = 1e-2
HY_FAST_PCT = 0.3
HY_SLOW_PCT = 1.5
HEAD_DIM = 128
HY_BLOCKS = 4
N_EXPERTS = 16
N_GROUPS = 4
EXPERTS_PER_GROUP = N_EXPERTS // N_GROUPS
LANES = 128
MOD_ROWS = 16
HGRN_CHUNK = 128
HGRN_GROUP = 16
HGRN_FINE_LEVELS = (2, 4)
VMEM_LIMIT = 56 << 20


def _cparams(*sem):
    return pltpu.CompilerParams(dimension_semantics=sem, vmem_limit_bytes=VMEM_LIMIT)


def _sigmoid(x):
    return 1.0 / (1.0 + jnp.exp2(x * (-1.0 / math.log(2.0))))


def _dot(a, b, **kw):
    return jnp.dot(a, b, preferred_element_type=F32, **kw)


def _resident(shape, layer=None):
    if layer is None:
        return pl.BlockSpec(shape, lambda *_: (0,) * len(shape), pipeline_mode=pl.Buffered(1))
    return pl.BlockSpec((None,) + tuple(shape), lambda *_: (layer,) + (0,) * len(shape),
                        pipeline_mode=pl.Buffered(1))


def _dot_nt(a, b):
    return lax.dot_general(a, b, (((1,), (1,)), ((), ())), preferred_element_type=F32)


def _mod_kernel(c_ref, w_ref, b_ref, o_ref):
    c = c_ref[...]
    cs = c * _sigmoid(c)
    o_ref[0] = _dot(cs, w_ref[0], precision=HIGHEST) + b_ref[0]


def _modulation(cc, w_mod, b_mod):
    depth, d, n = w_mod.shape
    tn = n // 3
    return pl.pallas_call(
        _mod_kernel,
        out_shape=jax.ShapeDtypeStruct((depth, MOD_ROWS, n), F32),
        grid=(depth, n // tn),
        in_specs=[
            pl.BlockSpec((MOD_ROWS, d), lambda l, j: (0, 0)),
            pl.BlockSpec((1, d, tn), lambda l, j: (l, 0, j)),
            pl.BlockSpec((1, 1, tn), lambda l, j: (l, 0, j)),
        ],
        out_specs=pl.BlockSpec((1, MOD_ROWS, tn), lambda l, j: (l, 0, j)),
        compiler_params=_cparams("parallel", "parallel"),
        name="modulation",
    )(cc, w_mod, b_mod.reshape(depth, 1, n))


def _dft_kernel(ac_ref, as_ref, bc_ref, bs_ref, c_ref, s_ref, *, n1):
    bc = bc_ref[...]
    bs = bs_ref[...]
    ac = ac_ref[...]
    as_ = as_ref[...]
    for j in range(n1):
        a_c = ac[:, j:j + 1]
        a_s = as_[:, j:j + 1]
        c_ref[:, j * LANES:(j + 1) * LANES] = (a_c * bc - a_s * bs).astype(BF16)
        s_ref[:, j * LANES:(j + 1) * LANES] = (a_s * bc + a_c * bs).astype(BF16)


def _dft_tables(L):
    n1 = L // LANES
    period = 4 * L
    r = np.arange(L, dtype=np.int64)[:, None]
    c1 = np.arange(n1, dtype=np.int64)[None, :]
    c0 = np.arange(LANES, dtype=np.int64)[None, :]

    def cs(phase):
        ang = (phase % period).astype(np.float64) * (2.0 * np.pi / period)
        return np.cos(ang).astype(np.float32), np.sin(ang).astype(np.float32)

    fa = cs((2 * r + 1) * (LANES * c1))
    fb = cs((2 * r + 1) * c0)
    ta = cs((2 * LANES * c1) * r)
    tb = cs((2 * c0 + 1) * r)
    return (fa, fb), (ta, tb)


def _dft_matrices(L):
    n1 = L // LANES
    tr = min(L, 256)
    outs = []
    for (ac, as_), (bc, bs) in _dft_tables(L):
        c, s = pl.pallas_call(
            functools.partial(_dft_kernel, n1=n1),
            out_shape=(jax.ShapeDtypeStruct((L, L), BF16), jax.ShapeDtypeStruct((L, L), BF16)),
            grid=(L // tr,),
            in_specs=[
                pl.BlockSpec((tr, n1), lambda i: (i, 0)),
                pl.BlockSpec((tr, n1), lambda i: (i, 0)),
                pl.BlockSpec((tr, LANES), lambda i: (i, 0)),
                pl.BlockSpec((tr, LANES), lambda i: (i, 0)),
            ],
            out_specs=(pl.BlockSpec((tr, L), lambda i: (i, 0)), pl.BlockSpec((tr, L), lambda i: (i, 0))),
            compiler_params=_cparams("parallel"),
            name="dft_tables",
        )(jnp.asarray(ac), jnp.asarray(as_), jnp.asarray(bc), jnp.asarray(bs))
        outs.append((c, s))
    return outs


def _inproj_kernel(x_ref, mod_ref, w_ref, cw_ref, cb_ref, u_ref, hg_ref, *, period, n_hy, tn):
    x = x_ref[0]
    tm = x.shape[0]
    xn = x * lax.rsqrt(jnp.mean(x * x, axis=-1, keepdims=True) + NORM_EPS)
    m = mod_ref[0]
    a = (xn * (1.0 + m[1:2]) + m[0:1]).astype(BF16)
    pos = lax.broadcasted_iota(jnp.int32, (tm, 1), 0) % period
    first = pos == 0
    last = pos == period - 1
    n_total = w_ref.shape[1]
    for j in range(n_total // tn):
        p = _dot(a, w_ref[:, j * tn:(j + 1) * tn])
        if j * tn < n_hy:
            cw = cw_ref[:, j * tn:(j + 1) * tn]
            prev = jnp.where(first, 0.0, pltpu.roll(p, 1, 0))
            nxt = jnp.where(last, 0.0, pltpu.roll(p, tm - 1, 0))
            u_ref[0, :, j * tn:(j + 1) * tn] = (
                prev * cw[0:1] + p * cw[1:2] + nxt * cw[2:3] + cb_ref[:, j * tn:(j + 1) * tn]
            ).astype(u_ref.dtype)
        else:
            hg_ref[0, :, j * tn - n_hy:(j + 1) * tn - n_hy] = p.astype(hg_ref.dtype)


def _inproj(x, mod_l, mod_row, w_in, layer, conv_w, conv_b, *, period, tm):
    bsz, L, d = x.shape
    n_hy = conv_w.shape[1]
    n_all = w_in.shape[-1]
    tn = n_hy // 3
    assert tm % period == 0 or period == L == tm
    return pl.pallas_call(
        functools.partial(_inproj_kernel, period=period, n_hy=n_hy, tn=tn),
        out_shape=(jax.ShapeDtypeStruct((bsz, L, n_hy), BF16),
                   jax.ShapeDtypeStruct((bsz, L, n_all - n_hy), BF16)),
        grid=(bsz, L // tm),
        in_specs=[
            pl.BlockSpec((1, tm, d), lambda b, i: (b, i, 0)),
            pl.BlockSpec((1, N_MOD, d), lambda b, i: (mod_row(b), 0, 0)),
            _resident((d, n_all), layer),
            _resident((3, n_hy)),
            _resident((1, n_hy)),
        ],
        out_specs=(pl.BlockSpec((1, tm, n_hy), lambda b, i: (b, i, 0)),
                   pl.BlockSpec((1, tm, n_all - n_hy), lambda b, i: (b, i, 0))),
        compiler_params=_cparams("parallel", "parallel"),
        name="inproj",
    )(x, mod_l, w_in, conv_w, conv_b.reshape(1, n_hy))


def _rev_rows(x):
    nblk = x.shape[0] // LANES
    r = lax.broadcasted_iota(jnp.int32, (LANES, LANES), 0)
    c_ = lax.broadcasted_iota(jnp.int32, (LANES, LANES), 1)
    exch = (r + c_ == LANES - 1).astype(BF16)
    xb = x.astype(BF16)
    return jnp.concatenate(
        [_dot(exch, xb[(nblk - 1 - i) * LANES:(nblk - i) * LANES]) for i in range(nblk)], axis=0)


def _filter_kernel(z_ref, w1_ref, b1_ref, w2_ref, b2_ref, w3f_ref, b3f_ref, w3b_ref, b3b_ref,
                   dl_ref, kap_ref, rev_ref, h_s):
    z = z_ref[...]
    L = z.shape[0]
    hid = w2_ref.shape[0]

    @pl.when((pl.program_id(0) == 0) & (pl.program_id(1) == 0))
    def _():
        h1 = jnp.sin(_dot(z, w1_ref[...], precision=HIGHEST) + b1_ref[...])
        h2 = jnp.sin(_dot(h1, w2_ref[...], precision=HIGHEST) + b2_ref[...])
        h_hi = h2.astype(BF16)
        h_s[...] = jnp.concatenate([h_hi, (h2 - h_hi.astype(F32)).astype(BF16)], axis=1)

    def last_layer(w_ref, b_ref):
        return (_dot(h_s[...], w_ref[:2 * hid, :]) + _dot(h_s[:, :hid], w_ref[2 * hid:, :])) + b_ref[...]

    hf = last_layer(w3f_ref, b3f_ref)
    hb = last_layer(w3b_ref, b3b_ref)
    win = jnp.exp(-z[:, 0:1] * dl_ref[...])
    hf = hf * win
    hb = hb * win
    nrm = (jnp.sum(jnp.abs(hf), axis=0, keepdims=True)
           + jnp.sum(jnp.abs(hb), axis=0, keepdims=True))
    inv = 1.0 / nrm
    hf = hf * inv
    hb = hb * inv
    first = lax.broadcasted_iota(jnp.int32, (L, 1), 0) == 0
    down1 = lambda y: jnp.where(first, 0.0, pltpu.roll(y, 1, 0))
    kap_ref[0, 0:L, :] = down1(_rev_rows(hb)).astype(BF16)
    kap_ref[0, L:2 * L, :] = hf.astype(BF16)
    rev_ref[0, 0:L, :] = down1(_rev_rows(hf)).astype(BF16)
    rev_ref[0, L:2 * L, :] = jnp.where(first, hf[0:1], hb).astype(BF16)


def _hyena_filter_taps(L, w1, b1, w2, b2, w3, b3):
    nfeat, hid = w1.shape
    c = w3.shape[1] // (2 * HY_ORDER)
    tc = min(c, 256)
    nct = c // tc
    t = jnp.linspace(0.0, 1.0, L, dtype=F32)
    n = jnp.arange(L, dtype=F32)
    freqs = jnp.linspace(1e-4, HY_BANDS - 1, HY_BANDS, dtype=F32)
    ang = (2.0 * math.pi / L) * n[:, None] * freqs[None, :]
    z = jnp.concatenate([t[:, None], jnp.cos(ang), -jnp.sin(ang)], axis=-1)
    z = jnp.pad(z, ((0, 0), (0, LANES - nfeat)))
    w1p = jnp.pad(w1, ((0, LANES - nfeat), (0, 0)))
    deltas = jnp.abs(jnp.linspace(math.log(HY_TARGET) / HY_FAST_PCT, math.log(HY_TARGET) / HY_SLOW_PCT,
                                  c, dtype=F32)).reshape(1, c)
    w3_hi = w3.astype(BF16)
    w3p = jnp.concatenate([w3_hi, w3_hi, (w3 - w3_hi.astype(F32)).astype(BF16)], axis=0)
    full = lambda shape: pl.BlockSpec(shape, lambda o, j: (0,) * len(shape))
    out = jax.ShapeDtypeStruct((HY_ORDER, 2 * L, c), BF16)
    return pl.pallas_call(
        _filter_kernel,
        out_shape=(out, out),
        grid=(HY_ORDER, nct),
        in_specs=[
            full((L, LANES)), full((LANES, hid)), full((1, hid)), full((hid, hid)), full((1, hid)),
            pl.BlockSpec((3 * hid, tc), lambda o, j: (0, o * 2 * nct + j)),
            pl.BlockSpec((1, tc), lambda o, j: (0, o * 2 * nct + j)),
            pl.BlockSpec((3 * hid, tc), lambda o, j: (0, o * 2 * nct + nct + j)),
            pl.BlockSpec((1, tc), lambda o, j: (0, o * 2 * nct + nct + j)),
            pl.BlockSpec((1, tc), lambda o, j: (0, j)),
        ],
        out_specs=(pl.BlockSpec((1, 2 * L, tc), lambda o, j: (o, 0, j)),
                   pl.BlockSpec((1, 2 * L, tc), lambda o, j: (o, 0, j))),
        scratch_shapes=[pltpu.VMEM((L, 2 * hid), BF16)],
        compiler_params=_cparams("arbitrary", "arbitrary"),
        name="hyena_filter",
    )(z, w1p, b1.reshape(1, hid), w2, b2.reshape(1, hid), w3p, b3.reshape(1, -1), w3p, b3.reshape(1, -1),
      deltas)


def _kspec_kernel(cf_ref, sf_ref, pos_ref, neg_ref, o_ref, *, scale):
    pos = pos_ref[0].astype(F32)
    neg = neg_ref[0].astype(F32)
    first = lax.broadcasted_iota(jnp.int32, (pos.shape[0], 1), 0) == 0
    neg = jnp.where(first, 0.0, neg)
    o_ref[0, 0, 0] = _dot(cf_ref[...], (pos + neg).astype(BF16)) * scale
    o_ref[0, 0, 1] = _dot(sf_ref[...], (pos - neg).astype(BF16)) * scale


def _filter_spectrum(cft, sft, kap, rev, nb):
    _, two_l, c = kap.shape
    bk = two_l // (2 * nb)
    n_off = 2 * nb - 1
    return pl.pallas_call(
        functools.partial(_kspec_kernel, scale=1.0 / bk),
        out_shape=jax.ShapeDtypeStruct((HY_ORDER, n_off, 2, bk, c), F32),
        grid=(HY_ORDER, n_off),
        in_specs=[
            _resident((bk, bk)),
            _resident((bk, bk)),
            pl.BlockSpec((1, bk, c), lambda o, k: (o, k + 1, 0)),
            pl.BlockSpec((1, bk, c), lambda o, k: (o, 2 * nb - 1 - k, 0)),
        ],
        out_specs=pl.BlockSpec((1, 1, 2, bk, c), lambda o, k: (o, k, 0, 0, 0)),
        compiler_params=_cparams("parallel", "parallel"),
        name="hyena_filter_spectrum",
    )(cft, sft, kap, rev)


def _hy_fwd_kernel(cf_ref, sf_ref, v_ref, a_ref, b_ref, *, nb, bk):
    for j in range(nb):
        rows = slice(j * bk, (j + 1) * bk)
        v = v_ref[0, rows, :].astype(BF16)
        a_ref[0, rows, :] = _dot(cf_ref[...], v).astype(BF16)
        b_ref[0, rows, :] = _dot(sf_ref[...], v).astype(BF16)


def _hy_forward(cft, sft, src, src_col, c, nb):
    bsz, L, _ = src.shape
    bk = L // nb
    out = jax.ShapeDtypeStruct((bsz, L, c), BF16)
    return pl.pallas_call(
        functools.partial(_hy_fwd_kernel, nb=nb, bk=bk),
        out_shape=(out, out),
        grid=(bsz,),
        in_specs=[
            _resident((bk, bk)),
            _resident((bk, bk)),
            pl.BlockSpec((1, L, c), lambda b: (b, 0, src_col)),
        ],
        out_specs=(pl.BlockSpec((1, L, c), lambda b: (b, 0, 0)), pl.BlockSpec((1, L, c), lambda b: (b, 0, 0))),
        compiler_params=_cparams("parallel"),
        name="hyena_spectrum",
    )(cft, sft, src)


def _hy_inv_kernel(cf_ref, sf_ref, a_ref, b_ref, k_ref, gate_ref, src_ref, d_ref, o_ref, *, nb, bk):
    for i in range(nb):
        p = q = None
        for j in range(nb):
            rows = slice(j * bk, (j + 1) * bk)
            a = a_ref[0, rows, :].astype(F32)
            b = b_ref[0, rows, :].astype(F32)
            kr = k_ref[i - j + nb - 1, 0]
            ks = k_ref[i - j + nb - 1, 1]
            pj = a * kr - b * ks
            qj = a * ks + b * kr
            p = pj if p is None else p + pj
            q = qj if q is None else q + qj
        conv = _dot(cf_ref[...], p.astype(BF16)) + _dot(sf_ref[...], q.astype(BF16))
        rows = slice(i * bk, (i + 1) * bk)
        o_ref[0, rows, :] = (gate_ref[0, rows, :] * (conv + src_ref[0, rows, :] * d_ref[0])).astype(o_ref.dtype)


def _hy_inverse(cf, sf, a, b, kspec, gate, gate_col, src, src_col, d, order):
    bsz, L, c = a.shape
    _, n_off, _, bk, _ = kspec.shape
    nb = (n_off + 1) // 2
    return pl.pallas_call(
        functools.partial(_hy_inv_kernel, nb=nb, bk=bk),
        out_shape=jax.ShapeDtypeStruct((bsz, L, c), BF16),
        grid=(bsz,),
        in_specs=[
            _resident((bk, bk)),
            _resident((bk, bk)),
            pl.BlockSpec((1, L, c), lambda b: (b, 0, 0)),
            pl.BlockSpec((1, L, c), lambda b: (b, 0, 0)),
            _resident((n_off, 2, bk, c), order),
            pl.BlockSpec((1, L, c), lambda b: (b, 0, gate_col)),
            pl.BlockSpec((1, L, c), lambda b: (b, 0, src_col)),
            pl.BlockSpec((1, 1, c), lambda b: (order, 0, 0)),
        ],
        out_specs=pl.BlockSpec((1, L, c), lambda b: (b, 0, 0)),
        compiler_params=_cparams("parallel"),
        name="hyena_inverse",
    )(cf, sf, a, b, kspec, gate, src, d.reshape(HY_ORDER, 1, c))


def _hyena_blocks(L):
    bk = max(L // HY_BLOCKS, LANES)
    return L // bk


def _hyena(u, dft, taps_w, d):
    (cft, sft), (cf, sf) = dft
    L = u.shape[1]
    c = u.shape[2] // 3
    nb = _hyena_blocks(L)
    kap, rev = _hyena_filter_taps(L, *taps_w)
    kspec = _filter_spectrum(cft, sft, kap, rev, nb)
    a, b = _hy_forward(cft, sft, u, 0, c, nb)
    z = _hy_inverse(cf, sf, a, b, kspec, u, 1, u, 0, d, 0)
    a, b = _hy_forward(cft, sft, z, 0, c, nb)
    return _hy_inverse(cf, sf, a, b, kspec, u, 2, z, 0, d, 1)


def _anchor_rows(b, n, a):
    assert n % 8 == 0
    parts = [jnp.broadcast_to(b[s + a:s + a + 1, :], (n, b.shape[1])) for s in range(0, b.shape[0], n)]
    return parts[0] if len(parts) == 1 else jnp.concatenate(parts, axis=0)


def _neg_abs_anchor_diff(b, n, a, reverse):
    h = n // 2
    m = _anchor_rows(b, n, a)
    if h % 8:
        bits = lax.bitcast_convert_type(b - m, jnp.uint32) | jnp.uint32(0x80000000)
        return lax.bitcast_convert_type(bits, F32)
    pieces = []
    for s in range(0, b.shape[0], h):
        first_half = (s // h) % 2 == 0
        upstream = first_half != reverse
        x, y = (m, b) if upstream else (b, m)
        pieces.append(x[s:s + h] - y[s:s + h])
    return jnp.concatenate(pieces, axis=0)


def _hgrn_weights(q, f_logit, lb, lv, tri, scores, *, reverse):
    C = HGRN_CHUNK
    G = q.shape[0] // C
    rows = lambda x, i: x[i * C:(i + 1) * C]
    f = lb + (1.0 - lb) * _sigmoid(f_logit)
    kk = 1.0 - f
    g = jnp.log2(f)
    g_top = lax.bitcast_convert_type(
        lax.bitcast_convert_type(g, jnp.uint32) & jnp.uint32(0xFFFF0000), F32)
    gg = jnp.concatenate([g_top.astype(BF16), (g - g_top).astype(BF16)], axis=1)
    bb = [_dot(tri, rows(gg, i)) for i in range(G)]
    part = lambda k: jnp.concatenate(
        [x[k * C:(k + 1) * C, :HEAD_DIM] + x[k * C:(k + 1) * C, HEAD_DIM:] for x in bb], axis=0)
    b = part(0)
    fine = {n_: part(k + 1) for k, n_ in enumerate(HGRN_FINE_LEVELS)}

    qb = q.astype(BF16)
    kb = kk.astype(BF16)
    n = 2
    level = 1
    while n <= C:
        a = n // 2 if reverse else n // 2 - 1
        d = fine[n] if n in fine else _neg_abs_anchor_diff(b, n, a, reverse)
        e = jnp.exp2(d).astype(BF16)
        qe = qb * e
        ke = kb * e
        scores = [jnp.where(lv == level, _dot_nt(rows(qe, i), rows(ke, i)), scores[i]) for i in range(G)]
        n *= 2
        level += 1

    b_end = _anchor_rows(b, C, 0 if reverse else C - 1)
    qd = qb * jnp.exp2(b).astype(BF16)
    kd = kb * jnp.exp2(b_end - b).astype(BF16)
    return scores, kk, qd, kd, jnp.exp2(b_end)


def _hgrn_kernel(qc, ic, gc, fc, bc, ql, il, gl, fl, bl, lb_ref, nw_ref, oc_ref, ol_ref, stf, stb):
    C = HGRN_CHUNK
    row = lax.broadcasted_iota(jnp.int32, (C, C), 0)
    col = lax.broadcasted_iota(jnp.int32, (C, C), 1)
    x = row ^ col
    lvl = jnp.zeros((C, C), jnp.int32)
    n = 1
    while n < C:
        lvl = lvl + (x >= n).astype(jnp.int32)
        n *= 2
    lv_f = jnp.where(row > col, lvl, 0)
    lv_b = jnp.where(row < col, lvl, 0)

    def cum_matrix(reverse):
        cum = lambda r: (col >= r) if reverse else (col <= r)
        mats = [cum(row).astype(F32)]
        for n_ in HGRN_FINE_LEVELS:
            anchor = row - row % n_ + (n_ // 2 if reverse else n_ // 2 - 1)
            upstream = (row % n_ < n_ // 2) != reverse
            diff = mats[0] - cum(anchor).astype(F32)
            mats.append(jnp.where(upstream, -diff, diff))
        return jnp.concatenate(mats, axis=0).astype(BF16)

    tri_f = cum_matrix(False)
    tri_b = cum_matrix(True)
    lb_f = lb_ref[0, 0:1, :]
    lb_b = lb_ref[0, 1:2, :]
    nw = nw_ref[...]
    stf[...] = jnp.zeros_like(stf)
    stb[...] = jnp.zeros_like(stb)

    def run(q_ref, i_ref, g_ref, f_ref, b_ref, o_ref):
        ld = lambda ref: ref[0].astype(F32)
        q_raw = ld(q_ref)
        q = q_raw * _sigmoid(q_raw)
        v = ld(i_ref)
        G = q.shape[0] // C
        rows = lambda x, i: x[i * C:(i + 1) * C]
        scores = [jnp.zeros((C, C), F32)] * G
        scores, k_f, qd_f, kd_f, dec_f = _hgrn_weights(q, ld(f_ref), lb_f, lv_f, tri_f, scores, reverse=False)
        scores, k_b, qd_b, kd_b, dec_b = _hgrn_weights(q, ld(b_ref), lb_b, lv_b, tri_b, scores, reverse=True)
        kd = jnp.concatenate([kd_f, kd_b], axis=1)
        qd = jnp.concatenate([qd_f, qd_b], axis=1)
        kv = [_dot(rows(v, i).T.astype(BF16), rows(kd, i)) for i in range(G)]
        before = [None] * G
        st = stf[...]
        for i in range(G):
            before[i] = st
            st = st * dec_f[i * C:i * C + 1] + kv[i][:, :HEAD_DIM]
        stf[...] = st
        st = stb[...]
        for i in range(G - 1, -1, -1):
            before[i] = jnp.concatenate([before[i], st], axis=1).astype(BF16)
            st = st * dec_b[i * C:i * C + 1] + kv[i][:, HEAD_DIM:]
        stb[...] = st
        vb = v.astype(BF16)
        diag = jnp.sum(q * (k_f + k_b), axis=-1, keepdims=True) * v
        o = jnp.concatenate(
            [_dot(scores[i].astype(BF16), rows(vb, i)) + _dot_nt(rows(qd, i), before[i]) for i in range(G)],
            axis=0) + diag
        o = o * lax.rsqrt(jnp.mean(o * o, axis=-1, keepdims=True) + NORM_EPS) * nw
        g = ld(g_ref)
        o_ref[0] = (o * (g * _sigmoid(g))).astype(o_ref.dtype)

    run(qc, ic, gc, fc, bc, oc_ref)
    run(ql, il, gl, fl, bl, ol_ref)


def _hgrn(hg_ctx, hg_lat, lb, norm_w):
    bsz, lc, n5 = hg_ctx.shape
    ll = hg_lat.shape[1]
    dg = n5 // 5
    nh = dg // HEAD_DIM
    C = HGRN_CHUNK
    assert lc % C == 0 and ll % C == 0 and max(lc, ll) <= HGRN_GROUP * C

    def slab(L, part):
        return pl.BlockSpec((1, L, HEAD_DIM), lambda b, h: (b, 0, part * nh + h))

    return pl.pallas_call(
        _hgrn_kernel,
        out_shape=(jax.ShapeDtypeStruct((bsz, lc, dg), BF16), jax.ShapeDtypeStruct((bsz, ll, dg), BF16)),
        grid=(bsz, nh),
        in_specs=[slab(lc, p) for p in range(5)] + [slab(ll, p) for p in range(5)] + [
            pl.BlockSpec((1, 2, HEAD_DIM), lambda b, h: (h, 0, 0)),
            pl.BlockSpec((1, HEAD_DIM), lambda b, h: (0, 0)),
        ],
        out_specs=(pl.BlockSpec((1, lc, HEAD_DIM), lambda b, h: (b, 0, h)),
                   pl.BlockSpec((1, ll, HEAD_DIM), lambda b, h: (b, 0, h))),
        scratch_shapes=[pltpu.VMEM((HEAD_DIM, HEAD_DIM), F32), pltpu.VMEM((HEAD_DIM, HEAD_DIM), F32)],
        compiler_params=_cparams("parallel", "parallel"),
        name="hgrn2",
    )(*([hg_ctx] * 5), *([hg_lat] * 5), lb, norm_w.reshape(1, HEAD_DIM))


def _first_argmax(vals, idx, sentinel):
    m = jnp.max(vals, axis=0, keepdims=True)
    first = jnp.min(jnp.where(vals == m, idx, sentinel), axis=0, keepdims=True)
    return m, first


def _moe_gates(logits_t, rbias):
    ne, tm = logits_t.shape
    neg = -jnp.inf
    mx = jnp.max(logits_t, axis=0, keepdims=True)
    ex = jnp.exp(logits_t - mx)
    scores = ex / jnp.sum(ex, axis=0, keepdims=True)
    sel = scores + rbias
    eidx = lax.broadcasted_iota(jnp.int32, (ne, tm), 0)
    grp = eidx // EXPERTS_PER_GROUP
    best_s = jnp.full((1, tm), neg, F32)
    best_g = jnp.zeros((1, tm), jnp.int32)
    for gi in range(N_GROUPS):
        mg = jnp.where(grp == gi, sel, neg)
        m1, i1 = _first_argmax(mg, eidx, ne)
        m2 = jnp.max(jnp.where(eidx == i1, neg, mg), axis=0, keepdims=True)
        gs = m1 + m2
        upd = gs > best_s
        best_g = jnp.where(upd, gi, best_g)
        best_s = jnp.where(upd, gs, best_s)
    sg = jnp.where(grp == best_g, sel, neg)
    _, i1 = _first_argmax(sg, eidx, ne)
    _, i2 = _first_argmax(jnp.where(eidx == i1, neg, sg), eidx, ne)
    chosen = (eidx == i1) | (eidx == i2)
    w = jnp.where(chosen, scores, 0.0)
    return w / jnp.sum(w, axis=0, keepdims=True), best_g


def _mix_moe_kernel(x_ref, hy_ref, hg_ref, mod_ref, w_ref, rw_ref, rb_ref, tri_ref, wg_ref, wu_ref, wd_ref,
                    *rest, cap, final):
    fw_ref, o_ref, t_s, g4_s = rest if final else (None,) + rest
    c = hy_ref.shape[-1]
    mix = _dot(hy_ref[0], w_ref[:c, :]) + _dot(hg_ref[0], w_ref[c:, :])
    m = mod_ref[0]
    x1 = x_ref[0] + m[2:3] * mix
    o_ref[0] = x1
    xn = x1 * lax.rsqrt(jnp.mean(x1 * x1, axis=-1, keepdims=True) + NORM_EPS)
    t = xn * (1.0 + m[4:5]) + m[3:4]
    t_hi = t.astype(BF16)
    t_s[...] = t_hi
    t_lo = (t - t_hi.astype(F32)).astype(BF16)
    r = _dot(t_hi, rw_ref[...])
    lg = r[:, :LANES] + r[:, LANES:] + _dot(t_lo, rw_ref[:, :LANES])
    ne, _, f = wg_ref.shape
    tm = t.shape[0]
    per_group = ne // N_GROUPS
    gates_t, best_g = _moe_gates(lg.T[:ne], rb_ref[...])

    row8 = lax.broadcasted_iota(jnp.int32, (8, tm), 0)
    member = row8 == best_g
    before = _dot(member.astype(BF16), tri_ref[...])
    rank_row = jnp.sum(jnp.where(member, before, 0.0), axis=0, keepdims=True)
    grp_row = best_g.astype(F32)
    count = jnp.sum(member.astype(F32), axis=1, keepdims=True)
    g4_t = gates_t
    for g in range(1, N_GROUPS):
        g4_t = g4_t + pltpu.roll(gates_t, g * per_group, 0)
    row_e = lax.broadcasted_iota(jnp.int32, (ne, tm), 0)
    info_t = jnp.where(row_e < per_group, g4_t,
                       jnp.where(row_e == per_group, rank_row, jnp.where(row_e == per_group + 1, grp_row, 0.0)))
    info = jnp.concatenate([info_t, jnp.zeros((LANES - ne, tm), F32)], axis=0).T
    g4_s[...] = info.astype(BF16)
    rank_col = info[:, per_group:per_group + 1]
    grp_col = info[:, per_group + 1:per_group + 2]
    g2 = m[5:6]

    slot_col = lax.broadcasted_iota(jnp.int32, (cap, 1), 0).astype(F32)
    slot_row = lax.broadcasted_iota(jnp.int32, (1, cap), 1).astype(F32)
    for g in range(N_GROUPS):
        n_g = count[g, 0].astype(jnp.int32)

        def body(s, carry, g=g):
            base = (s * cap).astype(F32)
            pick = ((rank_row - base == slot_col) & (grp_row == float(g))).astype(BF16)
            xg = _dot(pick, t_s[...]).astype(BF16)
            gs = _dot(pick, g4_s[...])
            acts = []
            for j in range(per_group):
                e = g * per_group + j
                hgate = _dot(xg, wg_ref[e])
                hup = _dot(xg, wu_ref[e])
                acts.append((hgate * _sigmoid(hgate) * hup * gs[:, j:j + 1]).astype(BF16))
            y = _dot(jnp.concatenate(acts, axis=1), wd_ref[g * per_group * f:(g + 1) * per_group * f, :])
            put = ((rank_col - base == slot_row) & (grp_col == float(g))).astype(BF16)
            o_ref[0] += _dot(put, (y * g2).astype(BF16))
            return carry

        lax.fori_loop(0, (n_g + cap - 1) // cap, body, 0)

    if final:
        o = o_ref[0]
        o_ref[0] = o * lax.rsqrt(jnp.mean(o * o, axis=-1, keepdims=True) + NORM_EPS) * fw_ref[...]


def _mix_moe(x, hy, hg, mod_l, mod_row, layer, w_out, router_w, rbias, wg, wu, wd, *, tm, final_w=None):
    bsz, L, d = x.shape
    c = hy.shape[-1]
    cg = hg.shape[-1]
    _, ne, _, f = wg.shape
    rw_hi = router_w.astype(BF16)
    rw_lo = (router_w - rw_hi.astype(F32)).astype(BF16)
    rw = jnp.zeros((d, 2 * LANES), BF16).at[:, :ne].set(rw_hi).at[:, LANES:LANES + ne].set(rw_lo)
    cap = 5 * tm // 16
    assert cap % 8 == 0
    pos = jnp.arange(tm, dtype=jnp.int32)
    tri = (pos[:, None] < pos[None, :]).astype(BF16)
    final = final_w is not None
    extra_specs = [_resident((1, d))] if final else []
    extra_args = [final_w.reshape(1, d)] if final else []
    return pl.pallas_call(
        functools.partial(_mix_moe_kernel, cap=cap, final=final),
        out_shape=jax.ShapeDtypeStruct((bsz, L, d), F32),
        grid=(bsz, L // tm),
        in_specs=[
            pl.BlockSpec((1, tm, d), lambda b, i: (b, i, 0)),
            pl.BlockSpec((1, tm, c), lambda b, i: (b, i, 0)),
            pl.BlockSpec((1, tm, cg), lambda b, i: (b, i, 0)),
            pl.BlockSpec((1, N_MOD, d), lambda b, i: (mod_row(b), 0, 0)),
            _resident((c + cg, d), layer),
            _resident((d, 2 * LANES)),
            _resident((ne, 1)),
            _resident((tm, tm)),
            _resident((ne, d, f), layer),
            _resident((ne, d, f), layer),
            _resident((ne * f, d), layer),
        ] + extra_specs,
        out_specs=pl.BlockSpec((1, tm, d), lambda b, i: (b, i, 0)),
        scratch_shapes=[pltpu.VMEM((tm, d), BF16), pltpu.VMEM((tm, LANES), BF16)],
        compiler_params=_cparams("parallel", "parallel"),
        name="mix_moe",
    )(x, hy, hg, mod_l, w_out, rw, rbias.reshape(ne, 1), tri, wg, wu, wd.reshape(-1, ne * f, d), *extra_args)


def kernel(x, c, ctx, c_ctx, w_mod, b_mod, w_in, w_out, hy_conv_w, hy_conv_b, hy_w1, hy_b1, hy_w2, hy_b2, hy_w3, hy_b3, hy_bias, hgrn_lower_bounds, hgrn_norm_w, router_w, router_bias, moe_w_gate, moe_w_up, moe_w_down, final_norm_w):
    bsz, seq, d = x.shape
    ctx_len = ctx.shape[1]
    depth = w_mod.shape[0]
    dg = hgrn_lower_bounds.shape[-1]
    nh = dg // HEAD_DIM
    assert bsz + 1 <= MOD_ROWS and seq % GRID_W == 0

    cc = jnp.zeros((MOD_ROWS, d), F32).at[:bsz].set(c).at[bsz].set(c_ctx)
    mod = _modulation(cc, w_mod, b_mod).reshape(depth, MOD_ROWS, N_MOD, d)
    lat_row = lambda b: b
    ctx_row = lambda b: bsz

    lb_soft = jax.nn.softmax(hgrn_lower_bounds.astype(F32), axis=1)
    lower = jnp.cumsum(lb_soft, axis=1) - lb_soft[:, :1]
    lower = lower.reshape(2, depth, nh, HEAD_DIM).transpose(1, 2, 0, 3)

    dft_lat = _dft_matrices(seq // _hyena_blocks(seq))
    dft_ctx = _dft_matrices(ctx_len // _hyena_blocks(ctx_len))
    tm_lat = min(seq, 1024)
    tm_moe = min(seq, 512)

    w_in_b = w_in.astype(BF16)
    moe_w = (w_out.astype(BF16), router_w, router_bias,
             moe_w_gate.astype(BF16), moe_w_up.astype(BF16), moe_w_down.astype(BF16))

    xc = ctx
    for l in range(depth):
        last = l == depth - 1
        u_lat, hg_lat = _inproj(x, mod[l], lat_row, w_in_b, l, hy_conv_w[l], hy_conv_b[l],
                                period=GRID_W, tm=tm_lat)
        u_ctx, hg_ctx = _inproj(xc, mod[l], ctx_row, w_in_b, l, hy_conv_w[l], hy_conv_b[l],
                                period=ctx_len, tm=ctx_len)
        o_ctx, o_lat = _hgrn(hg_ctx, hg_lat, lower[l], hgrn_norm_w[l])
        taps_w = (hy_w1[l], hy_b1[l], hy_w2[l], hy_b2[l], hy_w3[l], hy_b3[l])
        hy_lat = _hyena(u_lat, dft_lat, taps_w, hy_bias[l])
        x = _mix_moe(x, hy_lat, o_lat, mod[l], lat_row, l, *moe_w, tm=tm_moe,
                     final_w=final_norm_w if last else None)
        if not last:
            hy_ctx = _hyena(u_ctx, dft_ctx, taps_w, hy_bias[l])
            xc = _mix_moe(xc, hy_ctx, o_ctx, mod[l], ctx_row, l, *moe_w, tm=ctx_len)

    return x
```
